```python
import math
import jax, jax.numpy as jnp
from jax import lax
import numpy as np

D_MODEL = 1024
BATCH = 2
SEQ = 8192
DEPTH = 2
DEC_BATCH = 32
DEC_SEQ = 16
PAST_LEN = 2048

CHUNK = 64
Q_BLOCK = 128
MIX_WIDTH = D_MODEL
ATT_WIDTH = MIX_WIDTH // 2
N_HEADS_A = 4
HD_V = ATT_WIDTH // N_HEADS_A
HD_QK = HD_V // 2
RWKV_WIDTH = MIX_WIDTH - ATT_WIDTH
HEAD_R = 64
N_HEADS_R = RWKV_WIDTH // HEAD_R
LORA_W = 64
LORA_A = 64
LORA_G = 128
SHIFT_W = 2
ATT_COLS = 3 * ATT_WIDTH
RW_COLS = 3 * RWKV_WIDTH + LORA_W + LORA_A + LORA_G
PROJ_COLS = ATT_COLS + RW_COLS
D_FF = 2816
N_EXPERTS = 8
TOP_K = 2
D_FF_E = 1408
N_DENSE = (DEPTH + 1) // 2
N_MOE = DEPTH // 2
NORM_EPS = 1e-6
LNX_EPS = 64e-5
F32 = jnp.float32

kernel_name = "hybrid_diffattn_rwkv7_stream_step"


def rms_norm(x, g, eps=NORM_EPS):
    xf = x.astype(F32)
    y = xf * lax.rsqrt(jnp.mean(xf * xf, axis=-1, keepdims=True) + eps)
    return (y * g.astype(F32)).astype(x.dtype)


def alibi_slopes():
    return jnp.asarray([2.0 ** (-8.0 * (h + 1) / N_HEADS_A) for h in range(N_HEADS_A)], F32)


def diff_attend(q, k, v, q_pos, k_pos, lam):
    s = jnp.einsum('bqhme,bkhme->bhmqk', q.astype(F32), k.astype(F32)) * (HD_QK ** -0.5)
    dist = jnp.abs(q_pos[:, None] - k_pos[None, :]).astype(F32)
    bias = -alibi_slopes()[:, None, None] * dist
    visible = (k_pos[None, :] // CHUNK) <= (q_pos[:, None] // CHUNK)
    s = jnp.where(visible[None, None, None], s + bias[None, :, None], -jnp.inf)
    p = jax.nn.softmax(s, axis=-1)
    wgt = p[:, :, 0] - lam * p[:, :, 1]
    return jnp.einsum('bhqk,bkhd->bqhd', wgt, v.astype(F32))


def diff_attn_prompt(q, k, v, lam):
    B, T = q.shape[0], q.shape[1]
    nb = T // Q_BLOCK
    qb = jnp.swapaxes(q.reshape(B, nb, Q_BLOCK, N_HEADS_A, 2, HD_QK), 0, 1)
    k_pos = jnp.arange(T, dtype=jnp.int32)

    def one_block(args):
        q_blk, i = args
        q_pos = i * Q_BLOCK + jnp.arange(Q_BLOCK, dtype=jnp.int32)
        return diff_attend(q_blk, k, v, q_pos, k_pos, lam)

    o = lax.map(one_block, (qb, jnp.arange(nb, dtype=jnp.int32)))
    return jnp.swapaxes(o, 0, 1).reshape(B, T, N_HEADS_A, HD_V)


def rwkv7_step(S, inp):
    r_t, w_t, k_t, v_t, kk_t, a_t = inp
    sa = jnp.einsum('bhvk,bhk->bhv', S, kk_t)
    S = S * w_t[:, :, None, :] - sa[..., None] * (kk_t * a_t)[:, :, None, :] + v_t[..., None] * k_t[:, :, None, :]
    y = jnp.einsum('bhvk,bhk->bhv', S, r_t)
    return S, y


def rwkv7_mix(cols, shift_prev, s0, mu, w0, w2, a0, a2, g2, k_k, k_a, r_k, lnx_w, lnx_b):
    B, T, _ = cols.shape
    prev = jnp.concatenate([shift_prev.astype(cols.dtype), cols[:, :-1]], axis=1)
    xs = (cols.astype(F32) + (prev.astype(F32) - cols.astype(F32)) * mu.astype(F32))
    o1, o2, o3 = RWKV_WIDTH, 2 * RWKV_WIDTH, 3 * RWKV_WIDTH
    r, k, v = xs[..., :o1], xs[..., o1:o2], xs[..., o2:o3]
    wd = xs[..., o3:o3 + LORA_W]
    ad = xs[..., o3 + LORA_W:o3 + LORA_W + LORA_A]
    gd = xs[..., o3 + LORA_W + LORA_A:]
    w_raw = -jax.nn.softplus(-(w0.astype(F32) + jnp.tanh(wd) @ w2.astype(F32))) - 0.5
    decay = jnp.exp(-jnp.exp(w_raw))
    a = jax.nn.sigmoid(a0.astype(F32) + ad @ a2.astype(F32))
    g = jax.nn.sigmoid(gd) @ g2.astype(F32)

    def heads(t):
        return t.reshape(B, T, N_HEADS_R, HEAD_R)

    kk = heads(k * k_k.astype(F32))
    kk = kk / jnp.maximum(jnp.sqrt(jnp.sum(kk * kk, axis=-1, keepdims=True)), 1e-12)
    k = heads(k * (1.0 + (a - 1.0) * k_a.astype(F32)))
    r, v, decay, a = heads(r), heads(v), heads(decay), heads(a)
    seq_first = tuple(jnp.swapaxes(t, 0, 1) for t in (r, decay, k, v, kk, a))
    s_fin, y = lax.scan(rwkv7_step, s0.astype(F32), seq_first)
    y = jnp.swapaxes(y, 0, 1)
    mean = jnp.mean(y, axis=-1, keepdims=True)
    var = jnp.mean(jnp.square(y - mean), axis=-1, keepdims=True)
    yn = ((y - mean) * lax.rsqrt(var + LNX_EPS)).reshape(B, T, RWKV_WIDTH) * lnx_w.astype(F32) + lnx_b.astype(F32)
    bonus = (jnp.sum(r * k * r_k.astype(F32), axis=-1, keepdims=True) * v).reshape(B, T, RWKV_WIDTH)
    out = (yn + bonus) * g
    return out.astype(cols.dtype), cols[:, -(SHIFT_W - 1):], s_fin.astype(cols.dtype)


def swiglu(h, wg, wu, wd):
    return (jax.nn.silu(h @ wg) * (h @ wu)) @ wd


def moe_swiglu(h, w_router, wg, wu, wd):
    B, T, D = h.shape
    hf = h.reshape(B * T, D)
    logits = (hf @ w_router).astype(F32)
    top_v, top_i = lax.top_k(logits, TOP_K)
    gates = jax.nn.softmax(top_v, axis=-1)
    comb = jnp.sum(jax.nn.one_hot(top_i, N_EXPERTS, dtype=F32) * gates[..., None], axis=1)
    y = jnp.zeros_like(hf)
    for e in range(N_EXPERTS):
        y = y + comb[:, e:e + 1].astype(h.dtype) * swiglu(hf, wg[e], wu[e], wd[e])
    return y.reshape(B, T, D)


def run_group(x, c, past_k, past_v, wkv_init, shift_init, P):
    B, T, _ = x.shape
    cs = jax.nn.silu(c)
    ks, vs, wkvs, shifts = [], [], [], []
    for l in range(DEPTH):
        mod = (cs @ P['w_mod'][l] + P['b_mod'][l])[:, None, :]
        sh1, sc1, g1, sh2, sc2, g2 = jnp.split(mod, 6, axis=-1)
        h = rms_norm(x, P['g_norm1'][l]) * (1 + sc1) + sh1
        cols = h @ P['w_in'][l]
        q = rms_norm(cols[..., :ATT_WIDTH].reshape(B, T, N_HEADS_A, 2, HD_QK), P['q_gain'][l])
        k = rms_norm(cols[..., ATT_WIDTH:2 * ATT_WIDTH].reshape(B, T, N_HEADS_A, 2, HD_QK), P['k_gain'][l])
        v = cols[..., 2 * ATT_WIDTH:ATT_COLS].reshape(B, T, N_HEADS_A, HD_V)
        lam_init = 0.8 - 0.6 * math.exp(-0.3 * l)
        lam = (jnp.exp(jnp.sum(P['lambda_q1'][l].astype(F32) * P['lambda_k1'][l].astype(F32)))
               - jnp.exp(jnp.sum(P['lambda_q2'][l].astype(F32) * P['lambda_k2'][l].astype(F32))) + lam_init)
        if past_k is None:
            o = diff_attn_prompt(q, k, v, lam)
        else:
            n_past = past_k.shape[2]
            k_all = jnp.concatenate([past_k[l].astype(k.dtype), k], axis=1)
            v_all = jnp.concatenate([past_v[l].astype(v.dtype), v], axis=1)
            q_pos = n_past + jnp.arange(T, dtype=jnp.int32)
            k_pos = jnp.arange(n_past + T, dtype=jnp.int32)
            o = diff_attend(q, k_all, v_all, q_pos, k_pos, lam)
        o = (rms_norm(o.astype(x.dtype), P['subln'][l]) * (1 - lam_init)).reshape(B, T, ATT_WIDTH)
        y_r, last_cols, s_fin = rwkv7_mix(cols[..., ATT_COLS:], shift_init[l], wkv_init[l], P['mu'][l],
                                          P['w0'][l], P['w2'][l], P['a0'][l], P['a2'][l], P['g2'][l],
                                          P['k_k'][l], P['k_a'][l], P['r_k'][l], P['lnx_w'][l], P['lnx_b'][l])
        mix = jnp.concatenate([o, y_r], axis=-1) @ P['w_out'][l]
        x = x + g1 * mix
        h2 = rms_norm(x, P['g_norm2'][l]) * (1 + sc2) + sh2
        if l % 2 == 0:
            j = l // 2
            f = swiglu(h2, P['w_ff_gate'][j], P['w_ff_up'][j], P['w_ff_down'][j])
        else:
            j = l // 2
            f = moe_swiglu(h2, P['w_router'][j], P['w_moe_gate'][j], P['w_moe_up'][j], P['w_moe_down'][j])
        x = x + g2 * f
        ks.append(k)
        vs.append(v)
        wkvs.append(s_fin)
        shifts.append(last_cols)
    return x, jnp.stack(ks), jnp.stack(vs), jnp.stack(wkvs), jnp.stack(shifts)


def setup_inputs(seed: int = 0) -> dict:
    key = jax.random.key(seed)
    kit = iter(jax.random.split(key, 64))

    def nrm(shape, scale):
        return jax.random.normal(next(kit), shape, F32) * scale

    def gain(shape):
        return 1.0 + nrm(shape, 0.02)

    D = D_MODEL
    return {
        "x_prompt": nrm((BATCH, SEQ, D), 1.0),
        "x_sample": nrm((DEC_BATCH, DEC_SEQ, D), 1.0),
        "cache_k": nrm((DEPTH, DEC_BATCH, PAST_LEN, N_HEADS_A, 2, HD_QK), 1.0),
        "cache_v": nrm((DEPTH, DEC_BATCH, PAST_LEN, N_HEADS_A, HD_V), 1.0),
        "state_wkv": nrm((DEPTH, DEC_BATCH, N_HEADS_R, HEAD_R, HEAD_R), 0.5),
        "state_shift": nrm((DEPTH, DEC_BATCH, SHIFT_W - 1, RW_COLS), 1.0),
        "c_prompt": nrm((BATCH, D), 1.0),
        "c_sample": nrm((DEC_BATCH, D), 1.0),
        "w_mod": nrm((DEPTH, D, 6 * D), 0.5 * D ** -0.5),
        "b_mod": nrm((DEPTH, 6 * D), 0.02),
        "g_norm1": gain((DEPTH, D)),
        "g_norm2": gain((DEPTH, D)),
        "w_in": nrm((DEPTH, D, PROJ_COLS), D ** -0.5),
        "q_gain": gain((DEPTH, 2, HD_QK)),
        "k_gain": gain((DEPTH, 2, HD_QK)),
        "lambda_q1": nrm((DEPTH, HD_QK), 0.1),
        "lambda_k1": nrm((DEPTH, HD_QK), 0.1),
        "lambda_q2": nrm((DEPTH, HD_QK), 0.1),
        "lambda_k2": nrm((DEPTH, HD_QK), 0.1),
        "subln": gain((DEPTH, HD_V)),
        "mu": jax.random.uniform(next(kit), (DEPTH, RW_COLS), F32),
        "w0": jax.random.uniform(next(kit), (DEPTH, RWKV_WIDTH), F32, -6.0, 1.0),
        "w2": nrm((DEPTH, LORA_W, RWKV_WIDTH), 0.5 * LORA_W ** -0.5),
        "a0": nrm((DEPTH, RWKV_WIDTH), 0.1),
        "a2": nrm((DEPTH, LORA_A, RWKV_WIDTH), 0.5 * LORA_A ** -0.5),
        "g2": nrm((DEPTH, LORA_G, RWKV_WIDTH), LORA_G ** -0.5),
        "k_k": 0.85 + nrm((DEPTH, RWKV_WIDTH), 0.05),
        "k_a": gain((DEPTH, RWKV_WIDTH)),
        "r_k": nrm((DEPTH, N_HEADS_R, HEAD_R), 0.1),
        "lnx_w": gain((DEPTH, RWKV_WIDTH)),
        "lnx_b": nrm((DEPTH, RWKV_WIDTH), 0.02),
        "w_out": nrm((DEPTH, MIX_WIDTH, D), MIX_WIDTH ** -0.5),
        "w_ff_gate": nrm((N_DENSE, D, D_FF), D ** -0.5),
        "w_ff_up": nrm((N_DENSE, D, D_FF), D ** -0.5),
        "w_ff_down": nrm((N_DENSE, D_FF, D), D_FF ** -0.5),
        "w_router": nrm((N_MOE, D, N_EXPERTS), D ** -0.5),
        "w_moe_gate": nrm((N_MOE, N_EXPERTS, D, D_FF_E), D ** -0.5),
        "w_moe_up": nrm((N_MOE, N_EXPERTS, D, D_FF_E), D ** -0.5),
        "w_moe_down": nrm((N_MOE, N_EXPERTS, D_FF_E, D), D_FF_E ** -0.5),
    }


def reference(x_prompt, x_sample, cache_k, cache_v, state_wkv, state_shift, c_prompt, c_sample,
              w_mod, b_mod, g_norm1, g_norm2, w_in, q_gain, k_gain, lambda_q1, lambda_k1, lambda_q2,
              lambda_k2, subln, mu, w0, w2, a0, a2, g2, k_k, k_a, r_k, lnx_w, lnx_b, w_out,
              w_ff_gate, w_ff_up, w_ff_down, w_router, w_moe_gate, w_moe_up, w_moe_down):
    P = dict(w_mod=w_mod, b_mod=b_mod, g_norm1=g_norm1, g_norm2=g_norm2, w_in=w_in, q_gain=q_gain,
             k_gain=k_gain, lambda_q1=lambda_q1, lambda_k1=lambda_k1, lambda_q2=lambda_q2,
             lambda_k2=lambda_k2, subln=subln, mu=mu, w0=w0, w2=w2, a0=a0, a2=a2, g2=g2, k_k=k_k,
             k_a=k_a, r_k=r_k, lnx_w=lnx_w, lnx_b=lnx_b, w_out=w_out, w_ff_gate=w_ff_gate,
             w_ff_up=w_ff_up, w_ff_down=w_ff_down, w_router=w_router, w_moe_gate=w_moe_gate,
             w_moe_up=w_moe_up, w_moe_down=w_moe_down)
    bp = x_prompt.shape[0]
    wkv0 = jnp.zeros((DEPTH, bp, N_HEADS_R, HEAD_R, HEAD_R), F32)
    shift0 = jnp.zeros((DEPTH, bp, SHIFT_W - 1, RW_COLS), x_prompt.dtype)
    y_prompt, k_prompt, v_prompt, wkv_prompt, shift_prompt = run_group(
        x_prompt, c_prompt, None, None, wkv0, shift0, P)
    y_sample, k_sample, v_sample, wkv_sample, shift_sample = run_group(
        x_sample, c_sample, cache_k, cache_v, state_wkv, state_shift, P)
    return (y_prompt, y_sample, k_prompt, v_prompt, wkv_prompt, shift_prompt,
            k_sample, v_sample, wkv_sample, shift_sample)
```

```python
import functools
import math

import jax
import jax.numpy as jnp
from jax import lax
from jax.experimental import pallas as pl
from jax.experimental.pallas import tpu as pltpu

F32 = jnp.float32
BF16 = jnp.bfloat16

D_MODEL = 1024
CHUNK = 64
N_HEADS_A = 4
HD_V = 128
HD_QK = 64
ATT_WIDTH = 512
RWKV_WIDTH = 512
HEAD_R = 64
N_HEADS_R = 8
LORA_W = 64
LORA_A = 64
LORA_G = 128
RW_COLS = 3 * RWKV_WIDTH + LORA_W + LORA_A + LORA_G
D_FF = 2816
N_EXPERTS = 8
D_FF_E = 1408
NORM_EPS = 1e-6
LNX_EPS = 64e-5
NEG_BIG = -1e30
LANES = 128
VMEM_LIMIT = 56 * 1024 * 1024


def _cparams(sem):
    return pltpu.CompilerParams(dimension_semantics=sem, vmem_limit_bytes=VMEM_LIMIT)


def _dot(a, b):
    return jnp.dot(a, b, preferred_element_type=F32)


def _dot_nt(a, b):
    return lax.dot_general(a, b, (((1,), (1,)), ((), ())), preferred_element_type=F32)


def _dot_tn(a, b):
    return lax.dot_general(a, b, (((0,), (0,)), ((), ())), preferred_element_type=F32)


def _split2(x):
    hi = x.astype(BF16)
    lo = (x - hi.astype(F32)).astype(BF16)
    return hi, lo


def _seg_sum(x, e):
    hi, lo = _split2(x)
    return _dot(hi, e) + _dot(lo, e)


def _sigmoid(x):
    return 1.0 / (1.0 + jnp.exp(-x))


def _mod_body(c_ref, w_ref, b_ref, o_ref):
    c = c_ref[...]
    cs = c * _sigmoid(c)
    o_ref[...] = _dot(cs.astype(BF16), w_ref[...]) + b_ref[...]


def _mod(c, w_mod, b_mod):
    nb = c.shape[0]
    n = w_mod.shape[1]
    tn = 1536
    return pl.pallas_call(
        _mod_body,
        grid=(n // tn,),
        in_specs=[pl.BlockSpec((nb, D_MODEL), lambda j: (0, 0)),
                  pl.BlockSpec((D_MODEL, tn), lambda j: (0, j)),
                  pl.BlockSpec((1, tn), lambda j: (0, j))],
        out_specs=pl.BlockSpec((nb, tn), lambda j: (0, j)),
        out_shape=jax.ShapeDtypeStruct((nb, n), F32),
        compiler_params=_cparams(("arbitrary",)),
        name="mod",
    )(c, w_mod, b_mod.reshape(1, n))


def _in_body(x_ref, sc_ref, sh_ref, gn_ref, wq_ref, wk_ref, wv_ref, wr_ref, qg_ref, kg_ref, e_ref,
             q_ref, k_ref, v_ref, rw_ref, kb_ref, vb_ref):
    tb, tt, _ = x_ref.shape
    x = x_ref[...]
    ms = jnp.mean(x * x, axis=-1, keepdims=True)
    h = x * lax.rsqrt(ms + NORM_EPS) * gn_ref[...]
    h = h * (1.0 + sc_ref[...]) + sh_ref[...]
    hb = h.reshape(tb * tt, D_MODEL).astype(BF16)
    e = e_ref[...]

    def group_norm(c, gain):
        msq = _seg_sum(c * c, e) * (1.0 / HD_QK)
        return c * lax.rsqrt(msq + NORM_EPS) * gain

    q = group_norm(_dot(hb, wq_ref[...]), qg_ref[...])
    q_ref[...] = q.reshape(tb, tt, ATT_WIDTH).astype(BF16)
    k = group_norm(_dot(hb, wk_ref[...]), kg_ref[...])
    k_ref[...] = k.reshape(tb, tt, ATT_WIDTH)
    kb_ref[...] = k.reshape(tb, tt, ATT_WIDTH).astype(BF16)
    v = _dot(hb, wv_ref[...])
    v_ref[...] = v.reshape(tb, tt, ATT_WIDTH)
    vb_ref[...] = v.reshape(tb, tt, ATT_WIDTH).astype(BF16)
    rw_ref[...] = _dot(hb, wr_ref[...]).reshape(tb, tt, RW_COLS)


def _in_proj(x, sc, sh, gn, wq, wk, wv, wr, qg, kg, eseg, tb, tt):
    nb, t, _ = x.shape
    row = lambda w: pl.BlockSpec((tb, tt, w), lambda b, i: (b, i, 0))
    modspec = pl.BlockSpec((tb, 1, D_MODEL), lambda b, i: (b, 0, 0))
    full = lambda a: pl.BlockSpec(a.shape, lambda b, i: (0,) * a.ndim)
    return pl.pallas_call(
        _in_body,
        grid=(nb // tb, t // tt),
        in_specs=[row(D_MODEL), modspec, modspec, full(gn), full(wq), full(wk), full(wv), full(wr),
                  full(qg), full(kg), full(eseg)],
        out_specs=[row(ATT_WIDTH), row(ATT_WIDTH), row(ATT_WIDTH), row(RW_COLS), row(ATT_WIDTH),
                   row(ATT_WIDTH)],
        out_shape=[jax.ShapeDtypeStruct((nb, t, ATT_WIDTH), BF16),
                   jax.ShapeDtypeStruct((nb, t, ATT_WIDTH), F32),
                   jax.ShapeDtypeStruct((nb, t, ATT_WIDTH), F32),
                   jax.ShapeDtypeStruct((nb, t, RW_COLS), F32),
                   jax.ShapeDtypeStruct((nb, t, ATT_WIDTH), BF16),
                   jax.ShapeDtypeStruct((nb, t, ATT_WIDTH), BF16)],
        compiler_params=_cparams(("arbitrary", "arbitrary")),
        name="in_proj",
    )(x, sc, sh, gn, wq, wk, wv, wr, qg, kg, eseg)


def _alibi_slope(h):
    return 2.0 ** (-8.0 * (h + 1) / N_HEADS_A)


def _lambda_value(lam_ref, lam_init):
    lv = lam_ref[...]
    s1 = jnp.sum(lv[0:1] * lv[1:2], axis=-1, keepdims=True)
    s2 = jnp.sum(lv[2:3] * lv[3:4], axis=-1, keepdims=True)
    return jnp.exp(s1) - jnp.exp(s2) + lam_init


def _stack_maps(qh):
    lane = lax.broadcasted_iota(jnp.int32, qh.shape, 1)
    qs = qh * (HD_QK ** -0.5)
    zero = jnp.zeros_like(qs)
    return jnp.concatenate([jnp.where(lane < HD_QK, qs, zero), jnp.where(lane >= HD_QK, qs, zero)],
                           axis=0)


def _sub_norm(o, gain, lam_init):
    ms = jnp.mean(o * o, axis=-1, keepdims=True)
    return o * lax.rsqrt(ms + NORM_EPS) * gain * (1.0 - lam_init)


def _attn_body(q_ref, k_ref, v_ref, lam_ref, sub_ref, o_ref, q2_ref, m_ref, l_ref, acc_ref, *,
               bq, lam_init):
    i = pl.program_id(1)
    j = pl.program_id(2)

    @pl.when(j == 0)
    def _():
        m_ref[...] = jnp.full(m_ref.shape, NEG_BIG, F32)
        l_ref[...] = jnp.zeros(l_ref.shape, F32)
        acc_ref[...] = jnp.zeros(acc_ref.shape, F32)
        for h in range(N_HEADS_A):
            q2_ref[h] = _stack_maps(q_ref[0, :, h * HD_V:(h + 1) * HD_V])

    def step(masked):
        r = lax.broadcasted_iota(jnp.int32, (bq, bq), 0)
        c = lax.broadcasted_iota(jnp.int32, (bq, bq), 1)
        dist = jnp.abs((r - c + (i - j) * bq).astype(F32))
        if masked:
            vis = (c // CHUNK) <= (r // CHUNK)
        for h in range(N_HEADS_A):
            kh = k_ref[0, :, h * HD_V:(h + 1) * HD_V]
            vh = v_ref[0, :, h * HD_V:(h + 1) * HD_V]
            s_all = _dot_nt(q2_ref[h], kh)
            bias = dist * (-_alibi_slope(h))
            for mp in range(2):
                idx = 2 * h + mp
                s = s_all[mp * bq:(mp + 1) * bq] + bias
                if masked:
                    s = jnp.where(vis, s, NEG_BIG)
                m_old = m_ref[idx]
                m_new = jnp.maximum(m_old, jnp.max(s, axis=-1, keepdims=True))
                p = jnp.exp(s - m_new)
                alpha = jnp.exp(m_old - m_new)
                l_ref[idx] = alpha * l_ref[idx] + jnp.sum(p, axis=-1, keepdims=True)
                acc_ref[idx] = alpha * acc_ref[idx] + _dot(p.astype(BF16), vh)
                m_ref[idx] = m_new

    @pl.when(j < i)
    def _():
        step(False)

    @pl.when(j == i)
    def _():
        step(True)
        lam = _lambda_value(lam_ref, lam_init)
        for h in range(N_HEADS_A):
            o = acc_ref[2 * h] / l_ref[2 * h] - lam * (acc_ref[2 * h + 1] / l_ref[2 * h + 1])
            o_ref[0, :, h * HD_V:(h + 1) * HD_V] = _sub_norm(o, sub_ref[...], lam_init)


def _attn_prompt(q, kb, vb, lam_vecs, subln, lam_init, bq):
    nb, t, _ = q.shape
    nq = t // bq
    qspec = pl.BlockSpec((1, bq, ATT_WIDTH), lambda b, i, j: (b, i, 0))
    kspec = pl.BlockSpec((1, bq, ATT_WIDTH), lambda b, i, j: (b, jnp.minimum(j, i), 0))
    return pl.pallas_call(
        functools.partial(_attn_body, bq=bq, lam_init=lam_init),
        grid=(nb, nq, nq),
        in_specs=[qspec, kspec, kspec,
                  pl.BlockSpec(lam_vecs.shape, lambda b, i, j: (0, 0)),
                  pl.BlockSpec(subln.shape, lambda b, i, j: (0, 0))],
        out_specs=qspec,
        out_shape=jax.ShapeDtypeStruct((nb, t, ATT_WIDTH), F32),
        scratch_shapes=[pltpu.VMEM((N_HEADS_A, 2 * bq, HD_V), BF16),
                        pltpu.VMEM((2 * N_HEADS_A, bq, 1), F32),
                        pltpu.VMEM((2 * N_HEADS_A, bq, 1), F32),
                        pltpu.VMEM((2 * N_HEADS_A, bq, HD_V), F32)],
        compiler_params=_cparams(("arbitrary", "arbitrary", "arbitrary")),
        name="attn_prompt",
    )(q, kb, vb, lam_vecs, subln)


def _attn_dec_body(q_ref, ck_ref, cv_ref, kn_ref, vn_ref, lam_ref, sub_ref, o_ref, *, lam_init):
    tq = q_ref.shape[1]
    n_past = ck_ref.shape[0]
    lam = _lambda_value(lam_ref, lam_init)
    r = lax.broadcasted_iota(jnp.int32, (tq, n_past), 0)
    c = lax.broadcasted_iota(jnp.int32, (tq, n_past), 1)
    dist_c = jnp.abs((r + n_past - c).astype(F32))
    rn = lax.broadcasted_iota(jnp.int32, (tq, tq), 0)
    cn = lax.broadcasted_iota(jnp.int32, (tq, tq), 1)
    dist_n = jnp.abs((rn - cn).astype(F32))
    for h in range(N_HEADS_A):
        sl = slice(h * HD_V, (h + 1) * HD_V)
        q2 = _stack_maps(q_ref[0, :, sl])
        kc = ck_ref[:, sl].astype(BF16)
        vc = cv_ref[:, sl].astype(BF16)
        kn = kn_ref[0, :, sl].astype(BF16)
        vn = vn_ref[0, :, sl].astype(BF16)
        sc_all = _dot_nt(q2, kc)
        sn_all = _dot_nt(q2, kn)
        slope = -_alibi_slope(h)
        outs = []
        for mp in range(2):
            s_c = sc_all[mp * tq:(mp + 1) * tq] + dist_c * slope
            s_n = sn_all[mp * tq:(mp + 1) * tq] + dist_n * slope
            m = jnp.maximum(jnp.max(s_c, axis=-1, keepdims=True), jnp.max(s_n, axis=-1, keepdims=True))
            p_c = jnp.exp(s_c - m)
            p_n = jnp.exp(s_n - m)
            l = jnp.sum(p_c, axis=-1, keepdims=True) + jnp.sum(p_n, axis=-1, keepdims=True)
            acc = _dot(p_c.astype(BF16), vc) + _dot(p_n.astype(BF16), vn)
            outs.append(acc / l)
        o = outs[0] - lam * outs[1]
        o_ref[0, :, sl] = _sub_norm(o, sub_ref[...], lam_init)


def _attn_sample(q, cache_k, cache_v, layer, k_new, v_new, lam_vecs, subln, lam_init):
    nb, tq, _ = q.shape
    n_past = cache_k.shape[2]
    row = pl.BlockSpec((1, tq, ATT_WIDTH), lambda b: (b, 0, 0))
    cspec = pl.BlockSpec((None, None, n_past, ATT_WIDTH), lambda b: (layer, b, 0, 0))
    return pl.pallas_call(
        functools.partial(_attn_dec_body, lam_init=lam_init),
        grid=(nb,),
        in_specs=[row, cspec, cspec, row, row,
                  pl.BlockSpec(lam_vecs.shape, lambda b: (0, 0)),
                  pl.BlockSpec(subln.shape, lambda b: (0, 0))],
        out_specs=row,
        out_shape=jax.ShapeDtypeStruct((nb, tq, ATT_WIDTH), F32),
        compiler_params=_cparams(("arbitrary",)),
        name="attn_sample",
    )(q, cache_k, cache_v, k_new, v_new, lam_vecs, subln)


def _prep_body(rw_ref, shift_ref, mu_ref, w0_ref, w2_ref, a0_ref, a2_ref, g2_ref, kk_ref, ka_ref,
               rk_ref, e_ref, r_o, k_o, v_o, kk_o, bh_o, lw_o, g_o, bon_o, carry_ref):
    tb, tt, w = rw_ref.shape
    i = pl.program_id(1)
    rows = tb * tt
    cols = rw_ref[...].reshape(rows, w)
    rolled = pltpu.roll(cols, 1, 0)
    rowi = lax.broadcasted_iota(jnp.int32, (rows, 1), 0)
    if tb == 1:
        @pl.when(i == 0)
        def _():
            carry_ref[...] = shift_ref[0]

        prev = jnp.where(rowi == 0, carry_ref[...], rolled)
        carry_ref[...] = cols[tt - 1:tt, :]
    else:
        sh = jnp.broadcast_to(shift_ref[...], (tb, tt, w)).reshape(rows, w)
        prev = jnp.where((rowi % tt) == 0, sh, rolled)
    xs = cols + (prev - cols) * mu_ref[...]
    o1, o2, o3 = RWKV_WIDTH, 2 * RWKV_WIDTH, 3 * RWKV_WIDTH
    r = xs[:, :o1]
    k = xs[:, o1:o2]
    v = xs[:, o2:o3]
    da = xs[:, o3:o3 + LORA_W + LORA_A]
    gd = xs[:, o3 + LORA_W + LORA_A:]
    e = e_ref[...]
    z = w0_ref[...] + _dot(jnp.tanh(da).astype(BF16), w2_ref[...])
    lw_o[...] = (-math.exp(-0.5) * _sigmoid(z)).reshape(tb, tt, o1)
    a = _sigmoid(a0_ref[...] + _dot(da.astype(BF16), a2_ref[...]))
    g_o[...] = _dot(_sigmoid(gd).astype(BF16), g2_ref[...]).reshape(tb, tt, o1)
    kk = k * kk_ref[...]
    nrm = jnp.sqrt(_seg_sum(kk * kk, e))
    kk = kk / jnp.maximum(nrm, 1e-12)
    kp = k * (1.0 + (a - 1.0) * ka_ref[...])
    bon = _seg_sum(r * kp * rk_ref[...], e) * v
    r_o[...] = r.reshape(tb, tt, o1)
    k_o[...] = kp.reshape(tb, tt, o1)
    v_o[...] = v.reshape(tb, tt, o1)
    kk_o[...] = kk.reshape(tb, tt, o1)
    bh_o[...] = (kk * a).reshape(tb, tt, o1)
    bon_o[...] = bon.reshape(tb, tt, o1)


def _rwkv_prep(rw, shift, mu, w0, w2p, a0, a2p, g2, k_k, k_a, r_k, eseg, tb, tt):
    nb, t, _ = rw.shape
    row = lambda w: pl.BlockSpec((tb, tt, w), lambda b, i: (b, i, 0))
    full = lambda a: pl.BlockSpec(a.shape, lambda b, i: (0,) * a.ndim)
    out = jax.ShapeDtypeStruct((nb, t, RWKV_WIDTH), F32)
    return pl.pallas_call(
        _prep_body,
        grid=(nb // tb, t // tt),
        in_specs=[row(RW_COLS), pl.BlockSpec((tb, 1, RW_COLS), lambda b, i: (b, 0, 0)), full(mu), full(w0),
                  full(w2p), full(a0), full(a2p), full(g2), full(k_k), full(k_a), full(r_k), full(eseg)],
        out_specs=[row(RWKV_WIDTH)] * 8,
        out_shape=[out] * 8,
        scratch_shapes=[pltpu.VMEM((1, RW_COLS), F32)],
        compiler_params=_cparams(("arbitrary", "arbitrary")),
        name="rwkv_prep",
    )(rw, shift, mu, w0, w2p, a0, a2p, g2, k_k, k_a, r_k, eseg)


def _scan_body(r_ref, k_ref, v_ref, kk_ref, bh_ref, lw_ref, g_ref, bon_ref, s0_ref, lnw_ref, lnb_ref,
               e_ref, y_ref, sout_ref, st_ref, *, C):
    ci = pl.program_id(1)
    nc = pl.num_programs(1)

    @pl.when(ci == 0)
    def _():
        st_ref[...] = s0_ref[0]

    lw = lw_ref[0]
    ti = lax.broadcasted_iota(jnp.int32, (C, C), 0)
    si = lax.broadcasted_iota(jnp.int32, (C, C), 1)
    ltri = (si <= ti).astype(BF16)
    hi = lw.astype(BF16)
    r1 = lw - hi.astype(F32)
    mid = r1.astype(BF16)
    lo = (r1 - mid.astype(F32)).astype(BF16)
    cs = _dot(ltri, hi) + _dot(ltri, mid) + _dot(ltri, lo)
    cend = cs[C - 1:C, :]
    kk = kk_ref[0]
    bh = bh_ref[0]
    kp = k_ref[0]
    e_neg = jnp.exp(-cs)
    e_end = jnp.exp(cend - cs)
    at = (-kk * jnp.exp(cs - lw)).astype(BF16)
    rt = (r_ref[0] * jnp.exp(cs)).astype(BF16)
    bt = (bh * e_neg).astype(BF16)
    kt = (kp * e_neg).astype(BF16)
    bb = (bh * e_end).astype(BF16)
    kb = (kp * e_end).astype(BF16)
    g_c = jnp.exp(cend)
    vb = v_ref[0].astype(BF16)

    C2 = 2 * C
    hm = ((lax.broadcasted_iota(jnp.int32, (C2, LANES), 0) >= C)
          == (lax.broadcasted_iota(jnp.int32, (C2, LANES), 1) >= HEAD_R))
    hm2 = ((lax.broadcasted_iota(jnp.int32, (C2, C2), 0) >= C)
           == (lax.broadcasted_iota(jnp.int32, (C2, C2), 1) >= C))
    hm128 = ((lax.broadcasted_iota(jnp.int32, (LANES, LANES), 0) >= HEAD_R)
             == (lax.broadcasted_iota(jnp.int32, (LANES, LANES), 1) >= HEAD_R))
    tt = lax.broadcasted_iota(jnp.int32, (C, C2), 0)
    ss = lax.broadcasted_iota(jnp.int32, (C, C2), 1) & (C - 1)
    strict = ss < tt
    incl = ss <= tt
    eye = (ss == tt).astype(F32)

    def bd(x, mask):
        xx = jnp.concatenate([x, x], axis=0)
        return jnp.where(mask, xx, jnp.zeros_like(xx)).astype(BF16)

    def mm(x, y):
        return _dot(x.astype(BF16), bd(y, hm2))

    ys = []
    for p in range(N_HEADS_R // 2):
        sl = slice(p * LANES, (p + 1) * LANES)
        a_p, r_p, v_p = at[:, sl], rt[:, sl], vb[:, sl]
        ar = jnp.concatenate([a_p, r_p], axis=0)
        gb = _dot_nt(ar, bd(bt[:, sl], hm))
        gk = _dot_nt(ar, bd(kt[:, sl], hm))
        zero = jnp.zeros((C, C2), F32)
        n_mat = jnp.where(strict, gb[:C], zero)
        aks = jnp.where(strict, gk[:C], zero).astype(BF16)
        rbi = jnp.where(incl, gb[C:], zero).astype(BF16)
        rki = jnp.where(incl, gk[C:], zero).astype(BF16)
        nd = jnp.where((ss >> 3) == (tt >> 3), n_mat, zero)
        nd2 = mm(nd, nd)
        nd4 = mm(nd2, nd2)
        t_mat = mm(mm(eye + nd, eye + nd2), eye + nd4)
        lvl = 3
        while (1 << lvl) < C:
            tb_ = tt >> lvl
            ml = ((tb_ & 1) == 1) & ((ss >> lvl) == tb_ - 1)
            t_mat = t_mat + mm(mm(t_mat, jnp.where(ml, n_mat, zero)), t_mat)
            lvl += 1
        s_p = st_ref[p]
        s_b = s_p.astype(BF16)
        v_bd = bd(v_p, hm)
        x = _dot_nt(a_p, s_b) + _dot(aks, v_bd)
        u = _dot(t_mat.astype(BF16), bd(x, hm))
        y = _dot_nt(r_p, s_b) + _dot(rbi, bd(u, hm)) + _dot(rki, v_bd)
        ys.append(y)
        uv = jnp.concatenate([u.astype(BF16), v_p], axis=0)
        bk = jnp.concatenate([bb[:, sl], kb[:, sl]], axis=0)
        s_add = _dot_tn(uv, bk)
        st_ref[p] = s_p * g_c[:, sl] + jnp.where(hm128, s_add, jnp.zeros_like(s_add))

    ycat = jnp.concatenate(ys, axis=1)
    e = e_ref[...]
    mean = _seg_sum(ycat, e) * (1.0 / HEAD_R)
    d = ycat - mean
    var = _seg_sum(d * d, e) * (1.0 / HEAD_R)
    yn = d * lax.rsqrt(var + LNX_EPS) * lnw_ref[...] + lnb_ref[...]
    y_ref[0] = (yn + bon_ref[0]) * g_ref[0]

    @pl.when(ci == nc - 1)
    def _():
        sout_ref[0] = st_ref[...]


def _rwkv_scan(prep, s0_bd, lnx_w, lnx_b, eseg, C):
    nb, t, _ = prep[0].shape
    row = pl.BlockSpec((1, C, RWKV_WIDTH), lambda b, c: (b, c, 0))
    sspec = pl.BlockSpec((1, N_HEADS_R // 2, LANES, LANES), lambda b, c: (b, 0, 0, 0))
    full = lambda a: pl.BlockSpec(a.shape, lambda b, c: (0,) * a.ndim)
    return pl.pallas_call(
        functools.partial(_scan_body, C=C),
        grid=(nb, t // C),
        in_specs=[row] * 8 + [sspec, full(lnx_w), full(lnx_b), full(eseg)],
        out_specs=[row, sspec],
        out_shape=[jax.ShapeDtypeStruct((nb, t, RWKV_WIDTH), F32),
                   jax.ShapeDtypeStruct(s0_bd.shape, F32)],
        scratch_shapes=[pltpu.VMEM((N_HEADS_R // 2, LANES, LANES), F32)],
        compiler_params=_cparams(("arbitrary", "arbitrary")),
        name="rwkv_scan",
    )(*prep, s0_bd, lnx_w, lnx_b, eseg)


def _state_to_bd(s):
    nb = s.shape[0]
    s = s.reshape(nb, N_HEADS_R // 2, 2, HEAD_R, HEAD_R)
    eye = jnp.eye(2, dtype=s.dtype)
    out = jnp.einsum('bphvk,hg->bphvgk', s, eye)
    return out.reshape(nb, N_HEADS_R // 2, LANES, LANES)


def _state_from_bd(s):
    nb = s.shape[0]
    s = s.reshape(nb, N_HEADS_R // 2, 2, HEAD_R, 2, HEAD_R)
    out = jnp.stack([s[:, :, 0, :, 0, :], s[:, :, 1, :, 1, :]], axis=2)
    return out.reshape(nb, N_HEADS_R, HEAD_R, HEAD_R)


def _out_body(*refs, routed):
    if routed:
        (o_ref, y_ref, x_ref, g1_ref, sc_ref, sh_ref, gn_ref, wa_ref, wr_ref, rh_ref, rl_ref,
         x1_ref, h2_ref, comb_ref) = refs
    else:
        (o_ref, y_ref, x_ref, g1_ref, sc_ref, sh_ref, gn_ref, wa_ref, wr_ref, x1_ref, h2_ref) = refs
    tb, tt, _ = x_ref.shape
    rows = tb * tt
    ob = o_ref[...].reshape(rows, ATT_WIDTH).astype(BF16)
    yb = y_ref[...].reshape(rows, RWKV_WIDTH).astype(BF16)
    mix = _dot(ob, wa_ref[...]) + _dot(yb, wr_ref[...])
    x1 = x_ref[...] + g1_ref[...] * mix.reshape(tb, tt, D_MODEL)
    x1_ref[...] = x1
    ms = jnp.mean(x1 * x1, axis=-1, keepdims=True)
    h2 = x1 * lax.rsqrt(ms + NORM_EPS) * gn_ref[...]
    h2 = h2 * (1.0 + sc_ref[...]) + sh_ref[...]
    h2_ref[...] = h2.astype(BF16)
    if routed:
        hf = h2.reshape(rows, D_MODEL)
        hi, lo = _split2(hf)
        logits = _dot(hi, rh_ref[...]) + _dot(hi, rl_ref[...]) + _dot(lo, rh_ref[...])
        lane = lax.broadcasted_iota(jnp.int32, logits.shape, 1)
        logits = jnp.where(lane < N_EXPERTS, logits, NEG_BIG)
        m1 = jnp.max(logits, axis=-1, keepdims=True)
        i1 = jnp.min(jnp.where(logits == m1, lane, LANES), axis=-1, keepdims=True)
        rest = jnp.where(lane == i1, NEG_BIG, logits)
        m2 = jnp.max(rest, axis=-1, keepdims=True)
        i2 = jnp.min(jnp.where(rest == m2, lane, LANES), axis=-1, keepdims=True)
        ex = jnp.exp(m2 - m1)
        gate1 = 1.0 / (1.0 + ex)
        gate2 = ex / (1.0 + ex)
        comb = jnp.where(lane == i1, gate1, 0.0) + jnp.where(lane == i2, gate2, 0.0)
        comb_ref[...] = comb.reshape(tb, tt, LANES)


def _out_proj(o, y, x, g1, sc2, sh2, gn2, wa, wr, router, tb, tt):
    nb, t, _ = x.shape
    routed = router is not None
    row = lambda w: pl.BlockSpec((tb, tt, w), lambda b, i: (b, i, 0))
    modspec = pl.BlockSpec((tb, 1, D_MODEL), lambda b, i: (b, 0, 0))
    full = lambda a: pl.BlockSpec(a.shape, lambda b, i: (0,) * a.ndim)
    ins = [o, y, x, g1, sc2, sh2, gn2, wa, wr]
    in_specs = [row(ATT_WIDTH), row(RWKV_WIDTH), row(D_MODEL), modspec, modspec, modspec, full(gn2),
                full(wa), full(wr)]
    out_specs = [row(D_MODEL), row(D_MODEL)]
    out_shape = [jax.ShapeDtypeStruct((nb, t, D_MODEL), F32), jax.ShapeDtypeStruct((nb, t, D_MODEL), BF16)]
    if routed:
        ins += list(router)
        in_specs += [full(router[0]), full(router[1])]
        out_specs.append(row(LANES))
        out_shape.append(jax.ShapeDtypeStruct((nb, t, LANES), F32))
    return pl.pallas_call(
        functools.partial(_out_body, routed=routed),
        grid=(nb // tb, t // tt),
        in_specs=in_specs,
        out_specs=out_specs,
        out_shape=out_shape,
        compiler_params=_cparams(("arbitrary", "arbitrary")),
        name="out_proj_routed" if routed else "out_proj",
    )(*ins)


def _glu_body(*refs, routed):
    if routed:
        h_ref, x_ref, g2_ref, comb_ref, wg_ref, wu_ref, wd_ref, o_ref, acc_ref = refs
    else:
        h_ref, x_ref, g2_ref, wg_ref, wu_ref, wd_ref, o_ref, acc_ref = refs
    tb, tt, _ = x_ref.shape
    rows = tb * tt
    e = pl.program_id(2)
    ne = pl.num_programs(2)

    @pl.when(e == 0)
    def _():
        acc_ref[...] = jnp.zeros(acc_ref.shape, F32)

    hb = h_ref[...].reshape(rows, D_MODEL)
    gate = _dot(hb, wg_ref[...])
    up = _dot(hb, wu_ref[...])
    act = gate * _sigmoid(gate) * up
    if routed:
        comb = comb_ref[...].reshape(rows, LANES)
        lane = lax.broadcasted_iota(jnp.int32, comb.shape, 1)
        w = jnp.sum(jnp.where(lane == e, comb, 0.0), axis=-1, keepdims=True)
        act = act * w
    acc_ref[...] += _dot(act.astype(BF16), wd_ref[...])

    @pl.when(e == ne - 1)
    def _():
        o_ref[...] = x_ref[...] + g2_ref[...] * acc_ref[...].reshape(tb, tt, D_MODEL)


def _glu(h2, x1, g2, comb, wg, wu, wd, tb, tt):
    nb, t, _ = x1.shape
    routed = comb is not None
    row = lambda w: pl.BlockSpec((tb, tt, w), lambda b, i, e: (b, i, 0))
    modspec = pl.BlockSpec((tb, 1, D_MODEL), lambda b, i, e: (b, 0, 0))
    if routed:
        ne = N_EXPERTS
        wspec_in = pl.BlockSpec((None, D_MODEL, D_FF_E), lambda b, i, e: (e, 0, 0))
        wspec_out = pl.BlockSpec((None, D_FF_E, D_MODEL), lambda b, i, e: (e, 0, 0))
        ins = [h2, x1, g2, comb, wg, wu, wd]
        in_specs = [row(D_MODEL), row(D_MODEL), modspec, row(LANES), wspec_in, wspec_in, wspec_out]
    else:
        ne = D_FF // D_FF_E
        wspec_in = pl.BlockSpec((D_MODEL, D_FF_E), lambda b, i, e: (0, e))
        wspec_out = pl.BlockSpec((D_FF_E, D_MODEL), lambda b, i, e: (e, 0))
        ins = [h2, x1, g2, wg, wu, wd]
        in_specs = [row(D_MODEL), row(D_MODEL), modspec, wspec_in, wspec_in, wspec_out]
    return pl.pallas_call(
        functools.partial(_glu_body, routed=routed),
        grid=(nb // tb, t // tt, ne),
        in_specs=in_specs,
        out_specs=row(D_MODEL),
        out_shape=jax.ShapeDtypeStruct((nb, t, D_MODEL), F32),
        scratch_shapes=[pltpu.VMEM((tb * tt, D_MODEL), F32)],
        compiler_params=_cparams(("arbitrary", "arbitrary", "arbitrary")),
        name="glu_routed" if routed else "glu_dense",
    )(*ins)


def _prepare_weights(P):
    W = {}
    W['w_mod'] = P['w_mod'].astype(BF16)
    w_in = P['w_in'].astype(BF16)
    W['wq'] = w_in[:, :, :ATT_WIDTH]
    W['wk'] = w_in[:, :, ATT_WIDTH:2 * ATT_WIDTH]
    W['wv'] = w_in[:, :, 2 * ATT_WIDTH:3 * ATT_WIDTH]
    W['wrw'] = w_in[:, :, 3 * ATT_WIDTH:]
    depth = P['w_in'].shape[0]
    zw = jnp.zeros((depth, LORA_A, RWKV_WIDTH), BF16)
    W['w2p'] = jnp.concatenate([P['w2'].astype(BF16), zw], axis=1)
    W['a2p'] = jnp.concatenate([zw, P['a2'].astype(BF16)], axis=1)
    W['g2'] = P['g2'].astype(BF16)
    w_out = P['w_out'].astype(BF16)
    W['wo_a'] = w_out[:, :ATT_WIDTH]
    W['wo_r'] = w_out[:, ATT_WIDTH:]
    W['w_ff_gate'] = P['w_ff_gate'].astype(BF16)
    W['w_ff_up'] = P['w_ff_up'].astype(BF16)
    W['w_ff_down'] = P['w_ff_down'].astype(BF16)
    W['w_moe_gate'] = P['w_moe_gate'].astype(BF16)
    W['w_moe_up'] = P['w_moe_up'].astype(BF16)
    W['w_moe_down'] = P['w_moe_down'].astype(BF16)
    wr = jnp.pad(P['w_router'], ((0, 0), (0, 0), (0, LANES - N_EXPERTS)))
    W['router_hi'] = wr.astype(BF16)
    W['router_lo'] = (wr - W['router_hi'].astype(F32)).astype(BF16)
    seg = jnp.arange(RWKV_WIDTH) // HEAD_R
    W['eseg'] = (seg[:, None] == seg[None, :]).astype(BF16)
    return W


def _run_group(x, c, cache_k, cache_v, wkv_init, shift_init, P, W, tb, tt, scan_chunk, bq):
    nb, t, _ = x.shape
    depth = P['w_in'].shape[0]
    eseg = W['eseg']
    ks, vs, wkvs, shifts = [], [], [], []
    for l in range(depth):
        mod = _mod(c, W['w_mod'][l], P['b_mod'][l])
        sh1, sc1, g1, sh2, sc2, g2 = [mod[:, None, i * D_MODEL:(i + 1) * D_MODEL] for i in range(6)]
        qg = jnp.tile(P['q_gain'][l].reshape(1, 2 * HD_QK), (1, N_HEADS_A))
        kg = jnp.tile(P['k_gain'][l].reshape(1, 2 * HD_QK), (1, N_HEADS_A))
        q, k, v, rw, kb, vb = _in_proj(x, sc1, sh1, P['g_norm1'][l].reshape(1, D_MODEL), W['wq'][l], W['wk'][l],
                                       W['wv'][l], W['wrw'][l], qg, kg, eseg, tb, tt)
        lam_init = 0.8 - 0.6 * math.exp(-0.3 * l)
        lam_vecs = jnp.stack([P['lambda_q1'][l], P['lambda_k1'][l], P['lambda_q2'][l], P['lambda_k2'][l]])
        subln = P['subln'][l].reshape(1, HD_V)
        if cache_k is None:
            o = _attn_prompt(q, kb, vb, lam_vecs, subln, lam_init, bq)
        else:
            o = _attn_sample(q, cache_k, cache_v, l, k, v, lam_vecs, subln, lam_init)
        vec = lambda a: a.reshape(1, RWKV_WIDTH)
        prep = _rwkv_prep(rw, shift_init[l], P['mu'][l].reshape(1, RW_COLS), vec(P['w0'][l]), W['w2p'][l],
                          vec(P['a0'][l]), W['a2p'][l], W['g2'][l], vec(P['k_k'][l]), vec(P['k_a'][l]),
                          vec(P['r_k'][l]), eseg, tb, tt)
        y_r, s_bd = _rwkv_scan(prep, _state_to_bd(wkv_init[l]), vec(P['lnx_w'][l]), vec(P['lnx_b'][l]), eseg,
                               scan_chunk)
        routed = (l % 2 == 1)
        j = l // 2
        router = (W['router_hi'][j], W['router_lo'][j]) if routed else None
        res = _out_proj(o, y_r, x, g1, sc2, sh2, P['g_norm2'][l].reshape(1, D_MODEL), W['wo_a'][l], W['wo_r'][l],
                        router, tb, tt)
        if routed:
            x1, h2, comb = res
            x = _glu(h2, x1, g2, comb, W['w_moe_gate'][j], W['w_moe_up'][j], W['w_moe_down'][j], tb, tt)
        else:
            x1, h2 = res
            x = _glu(h2, x1, g2, None, W['w_ff_gate'][j], W['w_ff_up'][j], W['w_ff_down'][j], tb, tt)
        ks.append(k.reshape(nb, t, N_HEADS_A, 2, HD_QK))
        vs.append(v.reshape(nb, t, N_HEADS_A, HD_V))
        wkvs.append(_state_from_bd(s_bd))
        shifts.append(rw[:, t - 1:, :])
    return x, jnp.stack(ks), jnp.stack(vs), jnp.stack(wkvs), jnp.stack(shifts)


def kernel(x_prompt, x_sample, cache_k, cache_v, state_wkv, state_shift, c_prompt, c_sample, w_mod, b_mod, g_norm1, g_norm2, w_in, q_gain, k_gain, lambda_q1, lambda_k1, lambda_q2, lambda_k2, subln, mu, w0, w2, a0, a2, g2, k_k, k_a, r_k, lnx_w, lnx_b, w_out, w_ff_gate, w_ff_up, w_ff_down, w_router, w_moe_gate, w_moe_up, w_moe_down):
    P = dict(w_mod=w_mod, b_mod=b_mod, g_norm1=g_norm1, g_norm2=g_norm2, w_in=w_in, q_gain=q_gain,
             k_gain=k_gain, lambda_q1=lambda_q1, lambda_k1=lambda_k1, lambda_q2=lambda_q2,
             lambda_k2=lambda_k2, subln=subln, mu=mu, w0=w0, w2=w2, a0=a0, a2=a2, g2=g2, k_k=k_k,
             k_a=k_a, r_k=r_k, lnx_w=lnx_w, lnx_b=lnx_b, w_out=w_out, w_ff_gate=w_ff_gate,
             w_ff_up=w_ff_up, w_ff_down=w_ff_down, w_router=w_router, w_moe_gate=w_moe_gate,
             w_moe_up=w_moe_up, w_moe_down=w_moe_down)
    W = _prepare_weights(P)
    depth = w_in.shape[0]
    bp, tp, _ = x_prompt.shape
    bs, ts, _ = x_sample.shape
    n_past = cache_k.shape[2]
    wkv0 = jnp.zeros((depth, bp, N_HEADS_R, HEAD_R, HEAD_R), F32)
    shift0 = jnp.zeros((depth, bp, 1, RW_COLS), F32)
    y_p, k_p, v_p, wkv_p, shift_p = _run_group(x_prompt, c_prompt, None, None, wkv0, shift0, P, W,
                                               tb=1, tt=512, scan_chunk=CHUNK, bq=512)
    ck = cache_k.reshape(depth, bs, n_past, ATT_WIDTH)
    cv = cache_v.reshape(depth, bs, n_past, ATT_WIDTH)
    y_s, k_s, v_s, wkv_s, shift_s = _run_group(x_sample, c_sample, ck, cv, state_wkv, state_shift, P, W,
                                               tb=bs, tt=ts, scan_chunk=ts, bq=None)
    return (y_p, y_s, k_p, v_p, wkv_p, shift_p, k_s, v_s, wkv_s, shift_s)
```

```python
import functools
import math

import jax
import jax.numpy as jnp
from jax import lax
from jax.experimental import pallas as pl
from jax.experimental.pallas import tpu as pltpu

F32 = jnp.float32
BF16 = jnp.bfloat16

D_MODEL = 1024
CHUNK = 64
N_HEADS_A = 4
HD_V = 128
HD_QK = 64
ATT_WIDTH = 512
RWKV_WIDTH = 512
HEAD_R = 64
N_HEADS_R = 8
LORA_W = 64
LORA_A = 64
LORA_G = 128
RW_COLS = 3 * RWKV_WIDTH + LORA_W + LORA_A + LORA_G
D_FF = 2816
N_EXPERTS = 8
D_FF_E = 1408
NORM_EPS = 1e-6
LNX_EPS = 64e-5
NEG_BIG = -1e30
LANES = 128
VMEM_LIMIT = 56 * 1024 * 1024

def _cparams(sem):
    return pltpu.CompilerParams(dimension_semantics=sem, vmem_limit_bytes=VMEM_LIMIT)


def _dot(a, b):
    return jnp.dot(a, b, preferred_element_type=F32)


def _dot_nt(a, b):
    return lax.dot_general(a, b, (((1,), (1,)), ((), ())), preferred_element_type=F32)


def _dot_tn(a, b):
    return lax.dot_general(a, b, (((0,), (0,)), ((), ())), preferred_element_type=F32)


def _split2(x):
    hi = x.astype(BF16)
    lo = (x - hi.astype(F32)).astype(BF16)
    return hi, lo


def _seg_sum(x, e):
    hi, lo = _split2(x)
    return _dot(hi, e) + _dot(lo, e)


def _sigmoid(x):
    return 1.0 / (1.0 + jnp.exp(-x))


def _mod_body(c_ref, w_ref, b_ref, o_ref):
    c = c_ref[...]
    cs = c * _sigmoid(c)
    o_ref[...] = _dot(cs.astype(BF16), w_ref[...]) + b_ref[...]


def _mod(c, w_mod, b_mod):
    nb = c.shape[0]
    n = w_mod.shape[1]
    tn = 1536
    return pl.pallas_call(
        _mod_body,
        grid=(n // tn,),
        in_specs=[pl.BlockSpec((nb, D_MODEL), lambda j: (0, 0)),
                  pl.BlockSpec((D_MODEL, tn), lambda j: (0, j)),
                  pl.BlockSpec((1, tn), lambda j: (0, j))],
        out_specs=pl.BlockSpec((nb, tn), lambda j: (0, j)),
        out_shape=jax.ShapeDtypeStruct((nb, n), F32),
        compiler_params=_cparams(("arbitrary",)),
        name="mod",
    )(c, w_mod, b_mod.reshape(1, n))


def _in_body(x_ref, sc_ref, sh_ref, gn_ref, wq_ref, wk_ref, wv_ref, wr_ref, qg_ref, kg_ref, e_ref,
             q_ref, k_ref, v_ref, rw_ref, *kv_refs):
    tb, tt, _ = x_ref.shape
    x = x_ref[...]
    ms = jnp.mean(x * x, axis=-1, keepdims=True)
    h = x * lax.rsqrt(ms + NORM_EPS) * gn_ref[...]
    h = h * (1.0 + sc_ref[...]) + sh_ref[...]
    hb = h.reshape(tb * tt, D_MODEL).astype(BF16)
    e = e_ref[...]

    def group_norm(c, gain):
        msq = _seg_sum(c * c, e) * (1.0 / HD_QK)
        return c * lax.rsqrt(msq + NORM_EPS) * gain

    q = group_norm(_dot(hb, wq_ref[...]), qg_ref[...])
    q_ref[...] = q.reshape(tb, tt, ATT_WIDTH).astype(BF16)
    k = group_norm(_dot(hb, wk_ref[...]), kg_ref[...])
    k_ref[...] = k.reshape(tb, tt, ATT_WIDTH)
    v = _dot(hb, wv_ref[...])
    v_ref[...] = v.reshape(tb, tt, ATT_WIDTH)
    if kv_refs:
        kb_ref, vt_ref = kv_refs
        kb_ref[...] = k.reshape(tb, tt, ATT_WIDTH).astype(BF16)
        vt_ref[0] = v.T.astype(BF16)
    rw_ref[...] = _dot(hb, wr_ref[...]).reshape(tb, tt, RW_COLS)


def _in_proj(x, sc, sh, gn, wq, wk, wv, wr, qg, kg, eseg, tb, tt, flash_operands):
    nb, t, _ = x.shape
    row = lambda w: pl.BlockSpec((tb, tt, w), lambda b, i: (b, i, 0))
    modspec = pl.BlockSpec((tb, 1, D_MODEL), lambda b, i: (b, 0, 0))
    full = lambda a: pl.BlockSpec(a.shape, lambda b, i: (0,) * a.ndim)
    out_specs = [row(ATT_WIDTH), row(ATT_WIDTH), row(ATT_WIDTH), row(RW_COLS)]
    out_shape = [jax.ShapeDtypeStruct((nb, t, ATT_WIDTH), BF16),
                 jax.ShapeDtypeStruct((nb, t, ATT_WIDTH), F32),
                 jax.ShapeDtypeStruct((nb, t, ATT_WIDTH), F32),
                 jax.ShapeDtypeStruct((nb, t, RW_COLS), F32)]
    if flash_operands:
        assert tb == 1
        out_specs += [row(ATT_WIDTH), pl.BlockSpec((1, ATT_WIDTH, tt), lambda b, i: (b, 0, i))]
        out_shape += [jax.ShapeDtypeStruct((nb, t, ATT_WIDTH), BF16),
                      jax.ShapeDtypeStruct((nb, ATT_WIDTH, t), BF16)]
    return pl.pallas_call(
        _in_body,
        grid=(nb // tb, t // tt),
        in_specs=[row(D_MODEL), modspec, modspec, full(gn), full(wq), full(wk), full(wv), full(wr),
                  full(qg), full(kg), full(eseg)],
        out_specs=out_specs,
        out_shape=out_shape,
        compiler_params=_cparams(("arbitrary", "arbitrary")),
        name="in_proj",
    )(x, sc, sh, gn, wq, wk, wv, wr, qg, kg, eseg)


def _alibi_slope(h):
    return 2.0 ** (-8.0 * (h + 1) / N_HEADS_A)


def _lambda_value(lam_ref, lam_init):
    lv = lam_ref[...]
    s1 = jnp.sum(lv[0:1] * lv[1:2], axis=-1, keepdims=True)
    s2 = jnp.sum(lv[2:3] * lv[3:4], axis=-1, keepdims=True)
    return jnp.exp(s1) - jnp.exp(s2) + lam_init


def _stack_maps(qh):
    lane = lax.broadcasted_iota(jnp.int32, qh.shape, 1)
    qs = qh * (HD_QK ** -0.5)
    zero = jnp.zeros_like(qs)
    return jnp.concatenate([jnp.where(lane < HD_QK, qs, zero), jnp.where(lane >= HD_QK, qs, zero)],
                           axis=0)


def _sub_norm(o, gain, lam_init):
    ms = jnp.mean(o * o, axis=-1, keepdims=True)
    return o * lax.rsqrt(ms + NORM_EPS) * gain * (1.0 - lam_init)


def _attn_body(q_ref, k_ref, vt_ref, lam_ref, sub_ref, o_ref, q2_ref, m_ref, l_ref, acc_ref, s_ref,
               p_ref, *, bq, lam_init):
    i = pl.program_id(1)
    j = pl.program_id(2)
    bk = bq
    nq2 = 2 * bq
    kc = LANES
    lane = lax.broadcasted_iota(jnp.int32, (bq, LANES), 1)

    @pl.when(j == 0)
    def _():
        m_ref[...] = jnp.full(m_ref.shape, NEG_BIG, F32)
        l_ref[...] = jnp.zeros(l_ref.shape, F32)
        acc_ref[...] = jnp.zeros(acc_ref.shape, F32)
        lane2 = lax.broadcasted_iota(jnp.int32, (nq2, LANES), 1)
        for h in range(N_HEADS_A):
            slope = _alibi_slope(h)
            feat = jnp.where(lane2 == 0, slope * CHUNK, jnp.where(lane2 == 1, slope, 0.0)).astype(BF16)
            q2_ref[h] = jnp.concatenate([_stack_maps(q_ref[0, :, h * HD_V:(h + 1) * HD_V]), feat], axis=1)

    def step(diag):
        krel = lax.broadcasted_iota(jnp.int32, (bk, LANES), 0) + (j - i) * bq
        kfeat = jnp.where(lane == 0, krel >> 6, jnp.where(lane == 1, krel & (CHUNK - 1), 0))
        kfeat = kfeat.astype(F32).astype(BF16)
        for h in range(N_HEADS_A):
            slope = _alibi_slope(h)
            kh = jnp.concatenate([k_ref[0, :, h * HD_V:(h + 1) * HD_V], kfeat], axis=1)
            s_ref[...] = _dot_nt(kh, q2_ref[h])
            strips = range(nq2 // LANES)
            subs = range(bk // kc)

            def load(st, kb):
                s = s_ref[kb * kc:(kb + 1) * kc, st * LANES:(st + 1) * LANES]
                if diag:
                    c = lax.broadcasted_iota(jnp.int32, (kc, LANES), 0) + kb * kc
                    r = lax.broadcasted_iota(jnp.int32, (kc, LANES), 1) + (st * LANES) % bq
                    ahead = jnp.maximum(c - r, 0).astype(F32) * (-2.0 * slope)
                    s = jnp.where((c >> 6) <= (r >> 6), s + ahead, NEG_BIG)
                return s

            m_new, alpha = [], []
            for st in strips:
                mx = load(st, 0)
                for kb in subs[1:]:
                    mx = jnp.maximum(mx, load(st, kb))
                m_old = m_ref[h, :, st * LANES:(st + 1) * LANES]
                mn = jnp.maximum(m_old, jnp.max(mx, axis=0, keepdims=True))
                m_new.append(mn)
                alpha.append(jnp.exp(m_old - mn))
            for st in strips:
                psum = None
                for kb in subs:
                    p = jnp.exp(load(st, kb) - m_new[st])
                    psum = p if psum is None else psum + p
                    p_ref[kb * kc:(kb + 1) * kc, st * LANES:(st + 1) * LANES] = p.astype(BF16)
                sl = slice(st * LANES, (st + 1) * LANES)
                l_ref[h, :, sl] = alpha[st] * l_ref[h, :, sl] + jnp.sum(psum, axis=0, keepdims=True)
                m_ref[h, :, sl] = m_new[st]
            a_row = jnp.concatenate(alpha, axis=1)
            pv = _dot(vt_ref[0, h * HD_V:(h + 1) * HD_V, :], p_ref[...])
            acc_ref[h] = acc_ref[h] * a_row + pv

    @pl.when(j < i)
    def _():
        step(False)

    @pl.when(j == i)
    def _():
        step(True)
        lam = _lambda_value(lam_ref, lam_init)
        for h in range(N_HEADS_A):
            o1 = acc_ref[h, :, :bq] / l_ref[h, :, :bq]
            o2 = acc_ref[h, :, bq:] / l_ref[h, :, bq:]
            o = (o1 - lam * o2).T
            o_ref[0, :, h * HD_V:(h + 1) * HD_V] = _sub_norm(o, sub_ref[...], lam_init)


def _attn_prompt(q, kb, vt, lam_vecs, subln, lam_init, bq):
    nb, t, _ = q.shape
    nq = t // bq
    qspec = pl.BlockSpec((1, bq, ATT_WIDTH), lambda b, i, j: (b, i, 0))
    kspec = pl.BlockSpec((1, bq, ATT_WIDTH), lambda b, i, j: (b, jnp.minimum(j, i), 0))
    vspec = pl.BlockSpec((1, ATT_WIDTH, bq), lambda b, i, j: (b, 0, jnp.minimum(j, i)))
    return pl.pallas_call(
        functools.partial(_attn_body, bq=bq, lam_init=lam_init),
        grid=(nb, nq, nq),
        in_specs=[qspec, kspec, vspec,
                  pl.BlockSpec(lam_vecs.shape, lambda b, i, j: (0, 0)),
                  pl.BlockSpec(subln.shape, lambda b, i, j: (0, 0))],
        out_specs=qspec,
        out_shape=jax.ShapeDtypeStruct((nb, t, ATT_WIDTH), F32),
        scratch_shapes=[pltpu.VMEM((N_HEADS_A, 2 * bq, 2 * HD_V), BF16),
                        pltpu.VMEM((N_HEADS_A, 1, 2 * bq), F32),
                        pltpu.VMEM((N_HEADS_A, 1, 2 * bq), F32),
                        pltpu.VMEM((N_HEADS_A, HD_V, 2 * bq), F32),
                        pltpu.VMEM((bq, 2 * bq), F32),
                        pltpu.VMEM((bq, 2 * bq), BF16)],
        compiler_params=_cparams(("arbitrary", "arbitrary", "arbitrary")),
        name="attn_prompt",
    )(q, kb, vt, lam_vecs, subln)


def _attn_dec_body(q_ref, ck_ref, cv_ref, kn_ref, vn_ref, lam_ref, sub_ref, o_ref, *, lam_init):
    tq = q_ref.shape[1]
    n_past = ck_ref.shape[0]
    lam = _lambda_value(lam_ref, lam_init)
    r = lax.broadcasted_iota(jnp.int32, (tq, n_past), 0)
    c = lax.broadcasted_iota(jnp.int32, (tq, n_past), 1)
    dist_c = jnp.abs((r + n_past - c).astype(F32))
    rn = lax.broadcasted_iota(jnp.int32, (tq, tq), 0)
    cn = lax.broadcasted_iota(jnp.int32, (tq, tq), 1)
    dist_n = jnp.abs((rn - cn).astype(F32))
    for h in range(N_HEADS_A):
        sl = slice(h * HD_V, (h + 1) * HD_V)
        q2 = _stack_maps(q_ref[0, :, sl])
        kc = ck_ref[:, sl].astype(BF16)
        vc = cv_ref[:, sl].astype(BF16)
        kn = kn_ref[0, :, sl].astype(BF16)
        vn = vn_ref[0, :, sl].astype(BF16)
        sc_all = _dot_nt(q2, kc)
        sn_all = _dot_nt(q2, kn)
        slope = -_alibi_slope(h)
        outs = []
        for mp in range(2):
            s_c = sc_all[mp * tq:(mp + 1) * tq] + dist_c * slope
            s_n = sn_all[mp * tq:(mp + 1) * tq] + dist_n * slope
            m = jnp.maximum(jnp.max(s_c, axis=-1, keepdims=True), jnp.max(s_n, axis=-1, keepdims=True))
            p_c = jnp.exp(s_c - m)
            p_n = jnp.exp(s_n - m)
            l = jnp.sum(p_c, axis=-1, keepdims=True) + jnp.sum(p_n, axis=-1, keepdims=True)
            acc = _dot(p_c.astype(BF16), vc) + _dot(p_n.astype(BF16), vn)
            outs.append(acc / l)
        o = outs[0] - lam * outs[1]
        o_ref[0, :, sl] = _sub_norm(o, sub_ref[...], lam_init)


def _attn_sample(q, cache_k, cache_v, layer, k_new, v_new, lam_vecs, subln, lam_init):
    nb, tq, _ = q.shape
    n_past = cache_k.shape[2]
    row = pl.BlockSpec((1, tq, ATT_WIDTH), lambda b: (b, 0, 0))
    cspec = pl.BlockSpec((None, None, n_past, ATT_WIDTH), lambda b: (layer, b, 0, 0))
    return pl.pallas_call(
        functools.partial(_attn_dec_body, lam_init=lam_init),
        grid=(nb,),
        in_specs=[row, cspec, cspec, row, row,
                  pl.BlockSpec(lam_vecs.shape, lambda b: (0, 0)),
                  pl.BlockSpec(subln.shape, lambda b: (0, 0))],
        out_specs=row,
        out_shape=jax.ShapeDtypeStruct((nb, tq, ATT_WIDTH), F32),
        compiler_params=_cparams(("arbitrary",)),
        name="attn_sample",
    )(q, cache_k, cache_v, k_new, v_new, lam_vecs, subln)


def _prep_body(rw_ref, shift_ref, mu_ref, w0_ref, w2_ref, a0_ref, a2_ref, g2_ref, kk_ref, ka_ref,
               rk_ref, e_ref, r_o, k_o, v_o, kk_o, bh_o, lw_o, g_o, bon_o, carry_ref):
    tb, tt, w = rw_ref.shape
    i = pl.program_id(1)
    rows = tb * tt
    cols = rw_ref[...].reshape(rows, w)
    rolled = pltpu.roll(cols, 1, 0)
    rowi = lax.broadcasted_iota(jnp.int32, (rows, 1), 0)
    if tb == 1:
        @pl.when(i == 0)
        def _():
            carry_ref[...] = shift_ref[0]

        prev = jnp.where(rowi == 0, carry_ref[...], rolled)
        carry_ref[...] = cols[tt - 1:tt, :]
    else:
        sh = jnp.broadcast_to(shift_ref[...], (tb, tt, w)).reshape(rows, w)
        prev = jnp.where((rowi % tt) == 0, sh, rolled)
    xs = cols + (prev - cols) * mu_ref[...]
    o1, o2, o3 = RWKV_WIDTH, 2 * RWKV_WIDTH, 3 * RWKV_WIDTH
    r = xs[:, :o1]
    k = xs[:, o1:o2]
    v = xs[:, o2:o3]
    da = xs[:, o3:o3 + LORA_W + LORA_A]
    gd = xs[:, o3 + LORA_W + LORA_A:]
    e = e_ref[...]
    z = w0_ref[...] + _dot(jnp.tanh(da).astype(BF16), w2_ref[...])
    lw_o[...] = (-math.exp(-0.5) * _sigmoid(z)).reshape(tb, tt, o1)
    a = _sigmoid(a0_ref[...] + _dot(da.astype(BF16), a2_ref[...]))
    g_o[...] = _dot(_sigmoid(gd).astype(BF16), g2_ref[...]).reshape(tb, tt, o1)
    kk = k * kk_ref[...]
    nrm = jnp.sqrt(_seg_sum(kk * kk, e))
    kk = kk / jnp.maximum(nrm, 1e-12)
    kp = k * (1.0 + (a - 1.0) * ka_ref[...])
    bon = _seg_sum(r * kp * rk_ref[...], e) * v
    r_o[...] = r.reshape(tb, tt, o1)
    k_o[...] = kp.reshape(tb, tt, o1)
    v_o[...] = v.reshape(tb, tt, o1)
    kk_o[...] = kk.reshape(tb, tt, o1)
    bh_o[...] = (kk * a).reshape(tb, tt, o1)
    bon_o[...] = bon.reshape(tb, tt, o1)


def _rwkv_prep(rw, shift, mu, w0, w2p, a0, a2p, g2, k_k, k_a, r_k, eseg, tb, tt):
    nb, t, _ = rw.shape
    row = lambda w: pl.BlockSpec((tb, tt, w), lambda b, i: (b, i, 0))
    full = lambda a: pl.BlockSpec(a.shape, lambda b, i: (0,) * a.ndim)
    out = jax.ShapeDtypeStruct((nb, t, RWKV_WIDTH), F32)
    return pl.pallas_call(
        _prep_body,
        grid=(nb // tb, t // tt),
        in_specs=[row(RW_COLS), pl.BlockSpec((tb, 1, RW_COLS), lambda b, i: (b, 0, 0)), full(mu), full(w0),
                  full(w2p), full(a0), full(a2p), full(g2), full(k_k), full(k_a), full(r_k), full(eseg)],
        out_specs=[row(RWKV_WIDTH)] * 8,
        out_shape=[out] * 8,
        scratch_shapes=[pltpu.VMEM((1, RW_COLS), F32)],
        compiler_params=_cparams(("arbitrary", "arbitrary")),
        name="rwkv_prep",
    )(rw, shift, mu, w0, w2p, a0, a2p, g2, k_k, k_a, r_k, eseg)


def _scan_body(r_ref, k_ref, v_ref, kk_ref, bh_ref, lw_ref, g_ref, bon_ref, s0_ref, lnw_ref, lnb_ref,
               e_ref, y_ref, sout_ref, st_ref, *, C):
    ci = pl.program_id(1)
    nc = pl.num_programs(1)

    nbk = lw_ref.shape[0]

    @pl.when(ci == 0)
    def _():
        st_ref[...] = s0_ref[...]

    ti = lax.broadcasted_iota(jnp.int32, (C, C), 0)
    si = lax.broadcasted_iota(jnp.int32, (C, C), 1)
    ltri = (si <= ti).astype(BF16)
    at, rt, bt, kt, bb, kb, g_c, vb = [], [], [], [], [], [], [], []
    for b in range(nbk):
        lw = lw_ref[b]
        hi = lw.astype(BF16)
        r1 = lw - hi.astype(F32)
        mid = r1.astype(BF16)
        lo = (r1 - mid.astype(F32)).astype(BF16)
        cs = _dot(ltri, hi) + _dot(ltri, mid) + _dot(ltri, lo)
        cend = cs[C - 1:C, :]
        kk = kk_ref[b]
        bh = bh_ref[b]
        kp = k_ref[b]
        e_neg = jnp.exp(-cs)
        e_end = jnp.exp(cend - cs)
        at.append((-kk * jnp.exp(cs - lw)).astype(BF16))
        rt.append((r_ref[b] * jnp.exp(cs)).astype(BF16))
        bt.append((bh * e_neg).astype(BF16))
        kt.append((kp * e_neg).astype(BF16))
        bb.append((bh * e_end).astype(BF16))
        kb.append((kp * e_end).astype(BF16))
        g_c.append(jnp.exp(cend))
        vb.append(v_ref[b].astype(BF16))

    C2 = 2 * C
    hm = ((lax.broadcasted_iota(jnp.int32, (C2, LANES), 0) >= C)
          == (lax.broadcasted_iota(jnp.int32, (C2, LANES), 1) >= HEAD_R))
    hm2 = ((lax.broadcasted_iota(jnp.int32, (C2, C2), 0) >= C)
           == (lax.broadcasted_iota(jnp.int32, (C2, C2), 1) >= C))
    hm128 = ((lax.broadcasted_iota(jnp.int32, (LANES, LANES), 0) >= HEAD_R)
             == (lax.broadcasted_iota(jnp.int32, (LANES, LANES), 1) >= HEAD_R))
    tt = lax.broadcasted_iota(jnp.int32, (C, C2), 0)
    ss = lax.broadcasted_iota(jnp.int32, (C, C2), 1) & (C - 1)
    strict = ss < tt
    incl = ss <= tt
    eye = (ss == tt).astype(F32)

    def bd(x, mask):
        xx = jnp.concatenate([x, x], axis=0)
        return jnp.where(mask, xx, jnp.zeros_like(xx)).astype(BF16)

    def mm(x, y):
        return _dot(x.astype(BF16), bd(y, hm2))

    npair = N_HEADS_R // 2
    chains = [(b, q) for b in range(nbk) for q in range(npair)]
    pairs = range(len(chains))
    sls = [slice(q * LANES, (q + 1) * LANES) for (_, q) in chains]
    zero = jnp.zeros((C, C2), F32)
    a_l = [at[b][:, sls[p]] for p, (b, _) in enumerate(chains)]
    r_l = [rt[b][:, sls[p]] for p, (b, _) in enumerate(chains)]
    v_l = [vb[b][:, sls[p]] for p, (b, _) in enumerate(chains)]
    ar_l = [jnp.concatenate([a_l[p], r_l[p]], axis=0) for p in pairs]
    gb_l = [_dot_nt(ar_l[p], bd(bt[chains[p][0]][:, sls[p]], hm)) for p in pairs]
    gk_l = [_dot_nt(ar_l[p], bd(kt[chains[p][0]][:, sls[p]], hm)) for p in pairs]
    n_l = [jnp.where(strict, gb_l[p][:C], zero) for p in pairs]
    aks_l = [jnp.where(strict, gk_l[p][:C], zero).astype(BF16) for p in pairs]
    rbi_l = [jnp.where(incl, gb_l[p][C:], zero).astype(BF16) for p in pairs]
    rki_l = [jnp.where(incl, gk_l[p][C:], zero).astype(BF16) for p in pairs]
    nd_l = [jnp.where((ss >> 3) == (tt >> 3), n_l[p], zero) for p in pairs]
    nd2_l = [mm(nd_l[p], nd_l[p]) for p in pairs]
    nd4_l = [mm(nd2_l[p], nd2_l[p]) for p in pairs]
    t_l = [mm(eye + nd_l[p], eye + nd2_l[p]) for p in pairs]
    t_l = [mm(t_l[p], eye + nd4_l[p]) for p in pairs]
    lvl = 3
    while (1 << lvl) < C:
        tb_ = tt >> lvl
        ml = ((tb_ & 1) == 1) & ((ss >> lvl) == tb_ - 1)
        tn_l = [mm(t_l[p], jnp.where(ml, n_l[p], zero)) for p in pairs]
        t_l = [t_l[p] + mm(tn_l[p], t_l[p]) for p in pairs]
        lvl += 1
    s_l = [st_ref[b, q] for (b, q) in chains]
    sb_l = [s.astype(BF16) for s in s_l]
    vbd_l = [bd(v_l[p], hm) for p in pairs]
    x_l = [_dot_nt(a_l[p], sb_l[p]) + _dot(aks_l[p], vbd_l[p]) for p in pairs]
    u_l = [_dot(t_l[p].astype(BF16), bd(x_l[p], hm)) for p in pairs]
    ys = [_dot_nt(r_l[p], sb_l[p]) + _dot(rbi_l[p], bd(u_l[p], hm)) + _dot(rki_l[p], vbd_l[p]) for p in pairs]
    for p, (b, q) in enumerate(chains):
        uv = jnp.concatenate([u_l[p].astype(BF16), v_l[p]], axis=0)
        bk = jnp.concatenate([bb[b][:, sls[p]], kb[b][:, sls[p]]], axis=0)
        s_add = _dot_tn(uv, bk)
        st_ref[b, q] = s_l[p] * g_c[b][:, sls[p]] + jnp.where(hm128, s_add, jnp.zeros_like(s_add))

    e = e_ref[...]
    ycat = [jnp.concatenate(ys[b * npair:(b + 1) * npair], axis=1) for b in range(nbk)]
    mean = [_seg_sum(y, e) * (1.0 / HEAD_R) for y in ycat]
    d = [ycat[b] - mean[b] for b in range(nbk)]
    var = [_seg_sum(x * x, e) * (1.0 / HEAD_R) for x in d]
    for b in range(nbk):
        yn = d[b] * lax.rsqrt(var[b] + LNX_EPS) * lnw_ref[...] + lnb_ref[...]
        y_ref[b] = (yn + bon_ref[b]) * g_ref[b]

    @pl.when(ci == nc - 1)
    def _():
        sout_ref[...] = st_ref[...]


def _rwkv_scan(prep, s0_bd, lnx_w, lnx_b, eseg, C, nbk):
    nb, t, _ = prep[0].shape
    row = pl.BlockSpec((nbk, C, RWKV_WIDTH), lambda b, c: (b, c, 0))
    sspec = pl.BlockSpec((nbk, N_HEADS_R // 2, LANES, LANES), lambda b, c: (b, 0, 0, 0))
    full = lambda a: pl.BlockSpec(a.shape, lambda b, c: (0,) * a.ndim)
    return pl.pallas_call(
        functools.partial(_scan_body, C=C),
        grid=(nb // nbk, t // C),
        in_specs=[row] * 8 + [sspec, full(lnx_w), full(lnx_b), full(eseg)],
        out_specs=[row, sspec],
        out_shape=[jax.ShapeDtypeStruct((nb, t, RWKV_WIDTH), F32),
                   jax.ShapeDtypeStruct(s0_bd.shape, F32)],
        scratch_shapes=[pltpu.VMEM((nbk, N_HEADS_R // 2, LANES, LANES), F32)],
        compiler_params=_cparams(("arbitrary", "arbitrary")),
        name="rwkv_scan",
    )(*prep, s0_bd, lnx_w, lnx_b, eseg)


def _state_to_bd(s):
    nb = s.shape[0]
    s = s.reshape(nb, N_HEADS_R // 2, 2, HEAD_R, HEAD_R)
    eye = jnp.eye(2, dtype=s.dtype)
    out = jnp.einsum('bphvk,hg->bphvgk', s, eye)
    return out.reshape(nb, N_HEADS_R // 2, LANES, LANES)


def _state_from_bd(s):
    nb = s.shape[0]
    s = s.reshape(nb, N_HEADS_R // 2, 2, HEAD_R, 2, HEAD_R)
    out = jnp.stack([s[:, :, 0, :, 0, :], s[:, :, 1, :, 1, :]], axis=2)
    return out.reshape(nb, N_HEADS_R, HEAD_R, HEAD_R)


def _out_body(*refs, routed):
    if routed:
        (o_ref, y_ref, x_ref, g1_ref, sc_ref, sh_ref, gn_ref, wa_ref, wr_ref, rh_ref, rl_ref,
         x1_ref, h2_ref, comb_ref) = refs
    else:
        (o_ref, y_ref, x_ref, g1_ref, sc_ref, sh_ref, gn_ref, wa_ref, wr_ref, x1_ref, h2_ref) = refs
    tb, tt, _ = x_ref.shape
    rows = tb * tt
    ob = o_ref[...].reshape(rows, ATT_WIDTH).astype(BF16)
    yb = y_ref[...].reshape(rows, RWKV_WIDTH).astype(BF16)
    mix = _dot(ob, wa_ref[...]) + _dot(yb, wr_ref[...])
    x1 = x_ref[...] + g1_ref[...] * mix.reshape(tb, tt, D_MODEL)
    x1_ref[...] = x1
    ms = jnp.mean(x1 * x1, axis=-1, keepdims=True)
    h2 = x1 * lax.rsqrt(ms + NORM_EPS) * gn_ref[...]
    h2 = h2 * (1.0 + sc_ref[...]) + sh_ref[...]
    h2_ref[...] = h2.astype(BF16)
    if routed:
        hf = h2.reshape(rows, D_MODEL)
        hi, lo = _split2(hf)
        logits = _dot(hi, rh_ref[...]) + _dot(hi, rl_ref[...]) + _dot(lo, rh_ref[...])
        lane = lax.broadcasted_iota(jnp.int32, logits.shape, 1)
        logits = jnp.where(lane < N_EXPERTS, logits, NEG_BIG)
        m1 = jnp.max(logits, axis=-1, keepdims=True)
        i1 = jnp.min(jnp.where(logits == m1, lane, LANES), axis=-1, keepdims=True)
        rest = jnp.where(lane == i1, NEG_BIG, logits)
        m2 = jnp.max(rest, axis=-1, keepdims=True)
        i2 = jnp.min(jnp.where(rest == m2, lane, LANES), axis=-1, keepdims=True)
        ex = jnp.exp(m2 - m1)
        gate1 = 1.0 / (1.0 + ex)
        gate2 = ex / (1.0 + ex)
        comb = jnp.where(lane == i1, gate1, 0.0) + jnp.where(lane == i2, gate2, 0.0)
        comb_ref[...] = comb.reshape(tb, tt, LANES)


def _out_proj(o, y, x, g1, sc2, sh2, gn2, wa, wr, router, tb, tt):
    nb, t, _ = x.shape
    routed = router is not None
    row = lambda w: pl.BlockSpec((tb, tt, w), lambda b, i: (b, i, 0))
    modspec = pl.BlockSpec((tb, 1, D_MODEL), lambda b, i: (b, 0, 0))
    full = lambda a: pl.BlockSpec(a.shape, lambda b, i: (0,) * a.ndim)
    ins = [o, y, x, g1, sc2, sh2, gn2, wa, wr]
    in_specs = [row(ATT_WIDTH), row(RWKV_WIDTH), row(D_MODEL), modspec, modspec, modspec, full(gn2),
                full(wa), full(wr)]
    out_specs = [row(D_MODEL), row(D_MODEL)]
    out_shape = [jax.ShapeDtypeStruct((nb, t, D_MODEL), F32), jax.ShapeDtypeStruct((nb, t, D_MODEL), BF16)]
    if routed:
        ins += list(router)
        in_specs += [full(router[0]), full(router[1])]
        out_specs.append(row(LANES))
        out_shape.append(jax.ShapeDtypeStruct((nb, t, LANES), F32))
    return pl.pallas_call(
        functools.partial(_out_body, routed=routed),
        grid=(nb // tb, t // tt),
        in_specs=in_specs,
        out_specs=out_specs,
        out_shape=out_shape,
        compiler_params=_cparams(("arbitrary", "arbitrary")),
        name="out_proj_routed" if routed else "out_proj",
    )(*ins)


def _glu_body(*refs, routed):
    if routed:
        h_ref, x_ref, g2_ref, comb_ref, wg_ref, wu_ref, wd_ref, o_ref, acc_ref = refs
    else:
        h_ref, x_ref, g2_ref, wg_ref, wu_ref, wd_ref, o_ref, acc_ref = refs
    tb, tt, _ = x_ref.shape
    rows = tb * tt
    e = pl.program_id(2)
    ne = pl.num_programs(2)

    @pl.when(e == 0)
    def _():
        acc_ref[...] = jnp.zeros(acc_ref.shape, F32)

    hb = h_ref[...].reshape(rows, D_MODEL)
    gate = _dot(hb, wg_ref[...])
    up = _dot(hb, wu_ref[...])
    act = gate * _sigmoid(gate) * up
    if routed:
        comb = comb_ref[...].reshape(rows, LANES)
        lane = lax.broadcasted_iota(jnp.int32, comb.shape, 1)
        w = jnp.sum(jnp.where(lane == e, comb, 0.0), axis=-1, keepdims=True)
        act = act * w
    acc_ref[...] += _dot(act.astype(BF16), wd_ref[...])

    @pl.when(e == ne - 1)
    def _():
        o_ref[...] = x_ref[...] + g2_ref[...] * acc_ref[...].reshape(tb, tt, D_MODEL)


def _glu(h2, x1, g2, comb, wg, wu, wd, tb, tt):
    nb, t, _ = x1.shape
    routed = comb is not None
    row = lambda w: pl.BlockSpec((tb, tt, w), lambda b, i, e: (b, i, 0))
    modspec = pl.BlockSpec((tb, 1, D_MODEL), lambda b, i, e: (b, 0, 0))
    if routed:
        ne = N_EXPERTS
        wspec_in = pl.BlockSpec((None, D_MODEL, D_FF_E), lambda b, i, e: (e, 0, 0))
        wspec_out = pl.BlockSpec((None, D_FF_E, D_MODEL), lambda b, i, e: (e, 0, 0))
        ins = [h2, x1, g2, comb, wg, wu, wd]
        in_specs = [row(D_MODEL), row(D_MODEL), modspec, row(LANES), wspec_in, wspec_in, wspec_out]
    else:
        ne = D_FF // D_FF_E
        wspec_in = pl.BlockSpec((D_MODEL, D_FF_E), lambda b, i, e: (0, e))
        wspec_out = pl.BlockSpec((D_FF_E, D_MODEL), lambda b, i, e: (e, 0))
        ins = [h2, x1, g2, wg, wu, wd]
        in_specs = [row(D_MODEL), row(D_MODEL), modspec, wspec_in, wspec_in, wspec_out]
    return pl.pallas_call(
        functools.partial(_glu_body, routed=routed),
        grid=(nb // tb, t // tt, ne),
        in_specs=in_specs,
        out_specs=row(D_MODEL),
        out_shape=jax.ShapeDtypeStruct((nb, t, D_MODEL), F32),
        scratch_shapes=[pltpu.VMEM((tb * tt, D_MODEL), F32)],
        compiler_params=_cparams(("arbitrary", "arbitrary", "arbitrary")),
        name="glu_routed" if routed else "glu_dense",
    )(*ins)


def _prepare_weights(P):
    W = {}
    W['w_mod'] = P['w_mod'].astype(BF16)
    w_in = P['w_in'].astype(BF16)
    W['wq'] = w_in[:, :, :ATT_WIDTH]
    W['wk'] = w_in[:, :, ATT_WIDTH:2 * ATT_WIDTH]
    W['wv'] = w_in[:, :, 2 * ATT_WIDTH:3 * ATT_WIDTH]
    W['wrw'] = w_in[:, :, 3 * ATT_WIDTH:]
    depth = P['w_in'].shape[0]
    zw = jnp.zeros((depth, LORA_A, RWKV_WIDTH), BF16)
    W['w2p'] = jnp.concatenate([P['w2'].astype(BF16), zw], axis=1)
    W['a2p'] = jnp.concatenate([zw, P['a2'].astype(BF16)], axis=1)
    W['g2'] = P['g2'].astype(BF16)
    w_out = P['w_out'].astype(BF16)
    W['wo_a'] = w_out[:, :ATT_WIDTH]
    W['wo_r'] = w_out[:, ATT_WIDTH:]
    W['w_ff_gate'] = P['w_ff_gate'].astype(BF16)
    W['w_ff_up'] = P['w_ff_up'].astype(BF16)
    W['w_ff_down'] = P['w_ff_down'].astype(BF16)
    W['w_moe_gate'] = P['w_moe_gate'].astype(BF16)
    W['w_moe_up'] = P['w_moe_up'].astype(BF16)
    W['w_moe_down'] = P['w_moe_down'].astype(BF16)
    wr = jnp.pad(P['w_router'], ((0, 0), (0, 0), (0, LANES - N_EXPERTS)))
    W['router_hi'] = wr.astype(BF16)
    W['router_lo'] = (wr - W['router_hi'].astype(F32)).astype(BF16)
    seg = jnp.arange(RWKV_WIDTH) // HEAD_R
    W['eseg'] = (seg[:, None] == seg[None, :]).astype(BF16)
    return W


def _run_group(x, c, cache_k, cache_v, wkv_init, shift_init, P, W, tb, tt, scan_chunk, scan_nb, bq):
    nb, t, _ = x.shape
    depth = P['w_in'].shape[0]
    eseg = W['eseg']
    ks, vs, wkvs, shifts = [], [], [], []
    for l in range(depth):
        mod = _mod(c, W['w_mod'][l], P['b_mod'][l])
        sh1, sc1, g1, sh2, sc2, g2 = [mod[:, None, i * D_MODEL:(i + 1) * D_MODEL] for i in range(6)]
        qg = jnp.tile(P['q_gain'][l].reshape(1, 2 * HD_QK), (1, N_HEADS_A))
        kg = jnp.tile(P['k_gain'][l].reshape(1, 2 * HD_QK), (1, N_HEADS_A))
        proj = _in_proj(x, sc1, sh1, P['g_norm1'][l].reshape(1, D_MODEL), W['wq'][l], W['wk'][l],
                        W['wv'][l], W['wrw'][l], qg, kg, eseg, tb, tt, flash_operands=cache_k is None)
        q, k, v, rw = proj[:4]
        lam_init = 0.8 - 0.6 * math.exp(-0.3 * l)
        lam_vecs = jnp.stack([P['lambda_q1'][l], P['lambda_k1'][l], P['lambda_q2'][l], P['lambda_k2'][l]])
        subln = P['subln'][l].reshape(1, HD_V)
        if cache_k is None:
            o = _attn_prompt(q, proj[4], proj[5], lam_vecs, subln, lam_init, bq)
        else:
            o = _attn_sample(q, cache_k, cache_v, l, k, v, lam_vecs, subln, lam_init)
        vec = lambda a: a.reshape(1, RWKV_WIDTH)
        prep = _rwkv_prep(rw, shift_init[l], P['mu'][l].reshape(1, RW_COLS), vec(P['w0'][l]), W['w2p'][l],
                          vec(P['a0'][l]), W['a2p'][l], W['g2'][l], vec(P['k_k'][l]), vec(P['k_a'][l]),
                          vec(P['r_k'][l]), eseg, tb, tt)
        y_r, s_bd = _rwkv_scan(prep, _state_to_bd(wkv_init[l]), vec(P['lnx_w'][l]), vec(P['lnx_b'][l]), eseg,
                               scan_chunk, scan_nb)
        routed = (l % 2 == 1)
        j = l // 2
        router = (W['router_hi'][j], W['router_lo'][j]) if routed else None
        res = _out_proj(o, y_r, x, g1, sc2, sh2, P['g_norm2'][l].reshape(1, D_MODEL), W['wo_a'][l], W['wo_r'][l],
                        router, tb, tt)
        if routed:
            x1, h2, comb = res
            x = _glu(h2, x1, g2, comb, W['w_moe_gate'][j], W['w_moe_up'][j], W['w_moe_down'][j], tb, tt)
        else:
            x1, h2 = res
            x = _glu(h2, x1, g2, None, W['w_ff_gate'][j], W['w_ff_up'][j], W['w_ff_down'][j], tb, tt)
        ks.append(k.reshape(nb, t, N_HEADS_A, 2, HD_QK))
        vs.append(v.reshape(nb, t, N_HEADS_A, HD_V))
        wkvs.append(_state_from_bd(s_bd))
        shifts.append(rw[:, t - 1:, :])
    return x, jnp.stack(ks), jnp.stack(vs), jnp.stack(wkvs), jnp.stack(shifts)


def kernel(x_prompt, x_sample, cache_k, cache_v, state_wkv, state_shift, c_prompt, c_sample, w_mod, b_mod, g_norm1, g_norm2, w_in, q_gain, k_gain, lambda_q1, lambda_k1, lambda_q2, lambda_k2, subln, mu, w0, w2, a0, a2, g2, k_k, k_a, r_k, lnx_w, lnx_b, w_out, w_ff_gate, w_ff_up, w_ff_down, w_router, w_moe_gate, w_moe_up, w_moe_down):
    P = dict(w_mod=w_mod, b_mod=b_mod, g_norm1=g_norm1, g_norm2=g_norm2, w_in=w_in, q_gain=q_gain,
             k_gain=k_gain, lambda_q1=lambda_q1, lambda_k1=lambda_k1, lambda_q2=lambda_q2,
             lambda_k2=lambda_k2, subln=subln, mu=mu, w0=w0, w2=w2, a0=a0, a2=a2, g2=g2, k_k=k_k,
             k_a=k_a, r_k=r_k, lnx_w=lnx_w, lnx_b=lnx_b, w_out=w_out, w_ff_gate=w_ff_gate,
             w_ff_up=w_ff_up, w_ff_down=w_ff_down, w_router=w_router, w_moe_gate=w_moe_gate,
             w_moe_up=w_moe_up, w_moe_down=w_moe_down)
    W = _prepare_weights(P)
    depth = w_in.shape[0]
    bp, tp, _ = x_prompt.shape
    bs, ts, _ = x_sample.shape
    n_past = cache_k.shape[2]
    wkv0 = jnp.zeros((depth, bp, N_HEADS_R, HEAD_R, HEAD_R), F32)
    shift0 = jnp.zeros((depth, bp, 1, RW_COLS), F32)
    y_p, k_p, v_p, wkv_p, shift_p = _run_group(x_prompt, c_prompt, None, None, wkv0, shift0, P, W,
                                               tb=1, tt=512, scan_chunk=CHUNK, scan_nb=bp, bq=512)
    ck = cache_k.reshape(depth, bs, n_past, ATT_WIDTH)
    cv = cache_v.reshape(depth, bs, n_past, ATT_WIDTH)
    y_s, k_s, v_s, wkv_s, shift_s = _run_group(x_sample, c_sample, ck, cv, state_wkv, state_shift, P, W,
                                               tb=bs, tt=ts, scan_chunk=ts, scan_nb=4, bq=None)
    return (y_p, y_s, k_p, v_p, wkv_p, shift_p, k_s, v_s, wkv_s, shift_s)
```

```python
import functools
import math

import jax
import jax.numpy as jnp
from jax import lax
from jax.experimental import pallas as pl
from jax.experimental.pallas import tpu as pltpu

F32 = jnp.float32
BF16 = jnp.bfloat16

D_MODEL = 1024
CHUNK = 64
N_HEADS_A = 4
HD_V = 128
HD_QK = 64
ATT_WIDTH = 512
RWKV_WIDTH = 512
HEAD_R = 64
N_HEADS_R = 8
LORA_W = 64
LORA_A = 64
LORA_G = 128
RW_COLS = 3 * RWKV_WIDTH + LORA_W + LORA_A + LORA_G
D_FF = 2816
N_EXPERTS = 8
D_FF_E = 1408
NORM_EPS = 1e-6
LNX_EPS = 64e-5
NEG_BIG = -1e30
LANES = 128
VMEM_LIMIT = 56 * 1024 * 1024

def _cparams(sem):
    return pltpu.CompilerParams(dimension_semantics=sem, vmem_limit_bytes=VMEM_LIMIT)


def _dot(a, b):
    return jnp.dot(a, b, preferred_element_type=F32)


def _dot_nt(a, b):
    return lax.dot_general(a, b, (((1,), (1,)), ((), ())), preferred_element_type=F32)


def _dot_tn(a, b):
    return lax.dot_general(a, b, (((0,), (0,)), ((), ())), preferred_element_type=F32)


def _split2(x):
    hi = x.astype(BF16)
    lo = (x - hi.astype(F32)).astype(BF16)
    return hi, lo


def _seg_sum(x, e):
    hi, lo = _split2(x)
    return _dot(hi, e) + _dot(lo, e)


def _sigmoid(x):
    return 1.0 / (1.0 + jnp.exp(-x))


def _mod_body(c_ref, w_ref, b_ref, o_ref):
    c = c_ref[...]
    cs = c * _sigmoid(c)
    o_ref[...] = _dot(cs.astype(BF16), w_ref[...]) + b_ref[...]


def _mod(c, w_mod, b_mod):
    nb = c.shape[0]
    n = w_mod.shape[1]
    tn = 1536
    return pl.pallas_call(
        _mod_body,
        grid=(n // tn,),
        in_specs=[pl.BlockSpec((nb, D_MODEL), lambda j: (0, 0)),
                  pl.BlockSpec((D_MODEL, tn), lambda j: (0, j)),
                  pl.BlockSpec((1, tn), lambda j: (0, j))],
        out_specs=pl.BlockSpec((nb, tn), lambda j: (0, j)),
        out_shape=jax.ShapeDtypeStruct((nb, n), F32),
        compiler_params=_cparams(("arbitrary",)),
        name="mod",
    )(c, w_mod, b_mod.reshape(1, n))


def _in_body(*refs, prompt, n_alias):
    x_ref, sc_ref, sh_ref, gn_ref, wq_ref, wk_ref, wv_ref, wr_ref, qg_ref, kg_ref, e_ref = refs[:11]
    outs = refs[11 + n_alias:]
    tb, tt, _ = x_ref.shape
    x = x_ref[...]
    ms = jnp.mean(x * x, axis=-1, keepdims=True)
    h = x * lax.rsqrt(ms + NORM_EPS) * gn_ref[...]
    h = h * (1.0 + sc_ref[...]) + sh_ref[...]
    hb = h.reshape(tb * tt, D_MODEL).astype(BF16)
    e = e_ref[...]

    def group_norm(c, gain):
        msq = _seg_sum(c * c, e) * (1.0 / HD_QK)
        return c * lax.rsqrt(msq + NORM_EPS) * gain

    q = group_norm(_dot(hb, wq_ref[...]), qg_ref[...])
    k = group_norm(_dot(hb, wk_ref[...]), kg_ref[...])
    v = _dot(hb, wv_ref[...])
    rw = _dot(hb, wr_ref[...]).reshape(tb, tt, RW_COLS)
    if prompt:
        q_ref, rw_ref, kb_ref, vt_ref, kt_ref, vr_ref = outs
        kb_ref[...] = k.reshape(tb, tt, ATT_WIDTH).astype(BF16)
        vt_ref[0] = v.T.astype(BF16)
        kt_ref[0] = k.T
        for hd in range(N_HEADS_A):
            vr_ref[0, pl.ds(hd, tt, stride=N_HEADS_A), :] = v[:, hd * HD_V:(hd + 1) * HD_V]
    else:
        q_ref, rw_ref, k_ref, v_ref = outs
        k_ref[...] = k.reshape(tb, tt, ATT_WIDTH)
        v_ref[...] = v.reshape(tb, tt, ATT_WIDTH)
    q_ref[...] = q.reshape(tb, tt, ATT_WIDTH).astype(BF16)
    rw_ref[...] = rw


def _in_proj(x, sc, sh, gn, wq, wk, wv, wr, qg, kg, eseg, tb, tt, layer, depth, stacked):
    nb, t, _ = x.shape
    prompt = stacked is not None
    row = lambda w: pl.BlockSpec((tb, tt, w), lambda b, i: (b, i, 0))
    modspec = pl.BlockSpec((tb, 1, D_MODEL), lambda b, i: (b, 0, 0))
    full = lambda a: pl.BlockSpec(a.shape, lambda b, i: (0,) * a.ndim)
    ins = [x, sc, sh, gn, wq, wk, wv, wr, qg, kg, eseg]
    in_specs = [row(D_MODEL), modspec, modspec, full(gn), full(wq), full(wk), full(wv), full(wr),
                full(qg), full(kg), full(eseg)]
    out_specs = [row(ATT_WIDTH), row(RW_COLS)]
    out_shape = [jax.ShapeDtypeStruct((nb, t, ATT_WIDTH), BF16), jax.ShapeDtypeStruct((nb, t, RW_COLS), F32)]
    aliases = {}
    if prompt:
        assert tb == 1
        out_specs += [row(ATT_WIDTH), pl.BlockSpec((1, ATT_WIDTH, tt), lambda b, i: (b, 0, i)),
                      pl.BlockSpec((None, 1, ATT_WIDTH, tt), lambda b, i: (layer, b, 0, i)),
                      pl.BlockSpec((None, 1, N_HEADS_A * tt, HD_V), lambda b, i: (layer, b, i, 0))]
        out_shape += [jax.ShapeDtypeStruct((nb, t, ATT_WIDTH), BF16),
                      jax.ShapeDtypeStruct((nb, ATT_WIDTH, t), BF16),
                      jax.ShapeDtypeStruct((depth, nb, ATT_WIDTH, t), F32),
                      jax.ShapeDtypeStruct((depth, nb, N_HEADS_A * t, HD_V), F32)]
        for n, prev in enumerate(stacked):
            aliases[len(ins)] = 4 + n
            ins.append(prev)
            in_specs.append(pl.BlockSpec(memory_space=pl.ANY))
    else:
        out_specs += [row(ATT_WIDTH), row(ATT_WIDTH)]
        out_shape += [jax.ShapeDtypeStruct((nb, t, ATT_WIDTH), F32)] * 2
    return pl.pallas_call(
        functools.partial(_in_body, prompt=prompt, n_alias=len(aliases)),
        grid=(nb // tb, t // tt),
        in_specs=in_specs,
        out_specs=out_specs,
        out_shape=out_shape,
        input_output_aliases=aliases,
        compiler_params=_cparams(("arbitrary", "arbitrary")),
        name="in_proj",
    )(*ins)


def _alibi_slope(h):
    return 2.0 ** (-8.0 * (h + 1) / N_HEADS_A)


def _lambda_value(lam_ref, lam_init):
    lv = lam_ref[...]
    s1 = jnp.sum(lv[0:1] * lv[1:2], axis=-1, keepdims=True)
    s2 = jnp.sum(lv[2:3] * lv[3:4], axis=-1, keepdims=True)
    return jnp.exp(s1) - jnp.exp(s2) + lam_init


def _stack_maps(qh):
    lane = lax.broadcasted_iota(jnp.int32, qh.shape, 1)
    qs = qh * (HD_QK ** -0.5)
    zero = jnp.zeros_like(qs)
    return jnp.concatenate([jnp.where(lane < HD_QK, qs, zero), jnp.where(lane >= HD_QK, qs, zero)],
                           axis=0)


def _sub_norm(o, gain, lam_init):
    ms = jnp.mean(o * o, axis=-1, keepdims=True)
    return o * lax.rsqrt(ms + NORM_EPS) * gain * (1.0 - lam_init)


def _attn_body(it_ref, jt_ref, q_ref, k_ref, vt_ref, lam_ref, sub_ref, o_ref, q2_ref, m_ref, l_ref,
               acc_ref, s_ref, p_ref, *, bq, lam_init):
    step_id = pl.program_id(1)
    i = it_ref[step_id]
    j = jt_ref[step_id]
    bk = bq
    nq2 = 2 * bq
    kc = LANES
    lane = lax.broadcasted_iota(jnp.int32, (bq, LANES), 1)

    @pl.when(j == 0)
    def _():
        m_ref[...] = jnp.full(m_ref.shape, NEG_BIG, F32)
        l_ref[...] = jnp.zeros(l_ref.shape, F32)
        acc_ref[...] = jnp.zeros(acc_ref.shape, F32)
        lane2 = lax.broadcasted_iota(jnp.int32, (nq2, LANES), 1)
        for h in range(N_HEADS_A):
            slope = _alibi_slope(h)
            feat = jnp.where(lane2 == 0, slope * CHUNK, jnp.where(lane2 == 1, slope, 0.0)).astype(BF16)
            q2_ref[h] = jnp.concatenate([_stack_maps(q_ref[0, :, h * HD_V:(h + 1) * HD_V]), feat], axis=1)

    def step(diag):
        krel = lax.broadcasted_iota(jnp.int32, (bk, LANES), 0) + (j - i) * bq
        kfeat = jnp.where(lane == 0, krel >> 6, jnp.where(lane == 1, krel & (CHUNK - 1), 0))
        kfeat = kfeat.astype(F32).astype(BF16)
        for h in range(N_HEADS_A):
            slope = _alibi_slope(h)
            kh = jnp.concatenate([k_ref[0, :, h * HD_V:(h + 1) * HD_V], kfeat], axis=1)
            s_ref[...] = _dot_nt(kh, q2_ref[h])
            strips = range(nq2 // LANES)
            subs = range(bk // kc)

            def load(st, kb):
                s = s_ref[kb * kc:(kb + 1) * kc, st * LANES:(st + 1) * LANES]
                if diag:
                    c = lax.broadcasted_iota(jnp.int32, (kc, LANES), 0) + kb * kc
                    r = lax.broadcasted_iota(jnp.int32, (kc, LANES), 1) + (st * LANES) % bq
                    ahead = jnp.maximum(c - r, 0).astype(F32) * (-2.0 * slope)
                    s = jnp.where((c >> 6) <= (r >> 6), s + ahead, NEG_BIG)
                return s

            m_new, alpha = [], []
            for st in strips:
                mx = load(st, 0)
                for kb in subs[1:]:
                    mx = jnp.maximum(mx, load(st, kb))
                m_old = m_ref[h, :, st * LANES:(st + 1) * LANES]
                mn = jnp.maximum(m_old, jnp.max(mx, axis=0, keepdims=True))
                m_new.append(mn)
                alpha.append(jnp.exp(m_old - mn))
            for st in strips:
                psum = None
                for kb in subs:
                    p = jnp.exp(load(st, kb) - m_new[st])
                    psum = p if psum is None else psum + p
                    p_ref[kb * kc:(kb + 1) * kc, st * LANES:(st + 1) * LANES] = p.astype(BF16)
                sl = slice(st * LANES, (st + 1) * LANES)
                l_ref[h, :, sl] = alpha[st] * l_ref[h, :, sl] + jnp.sum(psum, axis=0, keepdims=True)
                m_ref[h, :, sl] = m_new[st]
            a_row = jnp.concatenate(alpha, axis=1)
            pv = _dot(vt_ref[0, h * HD_V:(h + 1) * HD_V, :], p_ref[...])
            acc_ref[h] = acc_ref[h] * a_row + pv

    @pl.when(j < i)
    def _():
        step(False)

    @pl.when(j == i)
    def _():
        step(True)
        lam = _lambda_value(lam_ref, lam_init)
        for h in range(N_HEADS_A):
            o1 = acc_ref[h, :, :bq] / l_ref[h, :, :bq]
            o2 = acc_ref[h, :, bq:] / l_ref[h, :, bq:]
            o = (o1 - lam * o2).T
            o_ref[0, :, h * HD_V:(h + 1) * HD_V] = _sub_norm(o, sub_ref[...], lam_init)


def _attn_prompt(q, kb, vt, lam_vecs, subln, lam_init, bq):
    nb, t, _ = q.shape
    nq = t // bq
    pairs = [(i, j) for i in range(nq) for j in range(i + 1)]
    i_tab = jnp.asarray([p[0] for p in pairs], jnp.int32)
    j_tab = jnp.asarray([p[1] for p in pairs], jnp.int32)
    qspec = pl.BlockSpec((1, bq, ATT_WIDTH), lambda b, s, it, jt: (b, it[s], 0))
    kspec = pl.BlockSpec((1, bq, ATT_WIDTH), lambda b, s, it, jt: (b, jt[s], 0))
    vspec = pl.BlockSpec((1, ATT_WIDTH, bq), lambda b, s, it, jt: (b, 0, jt[s]))
    grid_spec = pltpu.PrefetchScalarGridSpec(
        num_scalar_prefetch=2,
        grid=(nb, len(pairs)),
        in_specs=[qspec, kspec, vspec,
                  pl.BlockSpec(lam_vecs.shape, lambda b, s, it, jt: (0, 0)),
                  pl.BlockSpec(subln.shape, lambda b, s, it, jt: (0, 0))],
        out_specs=qspec,
        scratch_shapes=[pltpu.VMEM((N_HEADS_A, 2 * bq, 2 * HD_V), BF16),
                        pltpu.VMEM((N_HEADS_A, 1, 2 * bq), F32),
                        pltpu.VMEM((N_HEADS_A, 1, 2 * bq), F32),
                        pltpu.VMEM((N_HEADS_A, HD_V, 2 * bq), F32),
                        pltpu.VMEM((bq, 2 * bq), F32),
                        pltpu.VMEM((bq, 2 * bq), BF16)])
    return pl.pallas_call(
        functools.partial(_attn_body, bq=bq, lam_init=lam_init),
        grid_spec=grid_spec,
        out_shape=jax.ShapeDtypeStruct((nb, t, ATT_WIDTH), F32),
        compiler_params=_cparams(("arbitrary", "arbitrary")),
        name="attn_prompt",
    )(i_tab, j_tab, q, kb, vt, lam_vecs, subln)


def _attn_dec_body(q_ref, ck_ref, cv_ref, kn_ref, vn_ref, lam_ref, sub_ref, o_ref, *, lam_init):
    tq = q_ref.shape[1]
    n_past = ck_ref.shape[1]
    lam = _lambda_value(lam_ref, lam_init)
    r = lax.broadcasted_iota(jnp.int32, (tq, n_past), 0)
    c = lax.broadcasted_iota(jnp.int32, (tq, n_past), 1)
    dist_c = jnp.abs((r + n_past - c).astype(F32))
    rn = lax.broadcasted_iota(jnp.int32, (tq, tq), 0)
    cn = lax.broadcasted_iota(jnp.int32, (tq, tq), 1)
    dist_n = jnp.abs((rn - cn).astype(F32))
    for h in range(N_HEADS_A):
        sl = slice(h * HD_V, (h + 1) * HD_V)
        q2 = _stack_maps(q_ref[0, :, sl])
        vc = cv_ref[pl.ds(h, n_past, stride=N_HEADS_A), :].astype(BF16)
        kn = kn_ref[0, :, sl].astype(BF16)
        vn = vn_ref[0, :, sl].astype(BF16)
        sn_all = _dot_nt(q2, kn)
        slope = -_alibi_slope(h)
        outs = []
        sc_all = _dot(q2, ck_ref[sl, :].astype(BF16))
        for mp in range(2):
            s_c = sc_all[mp * tq:(mp + 1) * tq] + dist_c * slope
            s_n = sn_all[mp * tq:(mp + 1) * tq] + dist_n * slope
            m = jnp.maximum(jnp.max(s_c, axis=-1, keepdims=True), jnp.max(s_n, axis=-1, keepdims=True))
            p_c = jnp.exp(s_c - m)
            p_n = jnp.exp(s_n - m)
            l = jnp.sum(p_c, axis=-1, keepdims=True) + jnp.sum(p_n, axis=-1, keepdims=True)
            acc = _dot(p_c.astype(BF16), vc) + _dot(p_n.astype(BF16), vn)
            outs.append(acc / l)
        o = outs[0] - lam * outs[1]
        o_ref[0, :, sl] = _sub_norm(o, sub_ref[...], lam_init)


def _attn_sample(q, cache_k, cache_v, layer, k_new, v_new, lam_vecs, subln, lam_init):
    nb, tq, _ = q.shape
    n_past = cache_k.shape[3]
    row = pl.BlockSpec((1, tq, ATT_WIDTH), lambda b: (b, 0, 0))
    kspec = pl.BlockSpec((None, None, ATT_WIDTH, n_past), lambda b: (layer, b, 0, 0))
    vspec = pl.BlockSpec((None, None, n_past * N_HEADS_A, HD_V), lambda b: (layer, b, 0, 0))
    return pl.pallas_call(
        functools.partial(_attn_dec_body, lam_init=lam_init),
        grid=(nb,),
        in_specs=[row, kspec, vspec, row, row,
                  pl.BlockSpec(lam_vecs.shape, lambda b: (0, 0)),
                  pl.BlockSpec(subln.shape, lambda b: (0, 0))],
        out_specs=row,
        out_shape=jax.ShapeDtypeStruct((nb, tq, ATT_WIDTH), F32),
        compiler_params=_cparams(("arbitrary",)),
        name="attn_sample",
    )(q, cache_k, cache_v, k_new, v_new, lam_vecs, subln)


def _prep_body(rw_ref, shift_ref, mu_ref, w0_ref, w2_ref, a0_ref, a2_ref, g2_ref, kk_ref, ka_ref,
               rk_ref, e_ref, r_o, k_o, v_o, kk_o, bh_o, lw_o, g_o, bon_o, carry_ref):
    tb, tt, w = rw_ref.shape
    i = pl.program_id(1)
    rows = tb * tt
    cols = rw_ref[...].reshape(rows, w)
    rolled = pltpu.roll(cols, 1, 0)
    rowi = lax.broadcasted_iota(jnp.int32, (rows, 1), 0)
    if tb == 1:
        @pl.when(i == 0)
        def _():
            carry_ref[...] = shift_ref[0]

        prev = jnp.where(rowi == 0, carry_ref[...], rolled)
        carry_ref[...] = cols[tt - 1:tt, :]
    else:
        sh = jnp.broadcast_to(shift_ref[...], (tb, tt, w)).reshape(rows, w)
        prev = jnp.where((rowi % tt) == 0, sh, rolled)
    xs = cols + (prev - cols) * mu_ref[...]
    o1, o2, o3 = RWKV_WIDTH, 2 * RWKV_WIDTH, 3 * RWKV_WIDTH
    r = xs[:, :o1]
    k = xs[:, o1:o2]
    v = xs[:, o2:o3]
    da = xs[:, o3:o3 + LORA_W + LORA_A]
    gd = xs[:, o3 + LORA_W + LORA_A:]
    e = e_ref[...]
    z = w0_ref[...] + _dot(jnp.tanh(da).astype(BF16), w2_ref[...])
    lw_o[...] = (-math.exp(-0.5) * _sigmoid(z)).reshape(tb, tt, o1)
    a = _sigmoid(a0_ref[...] + _dot(da.astype(BF16), a2_ref[...]))
    g_o[...] = _dot(_sigmoid(gd).astype(BF16), g2_ref[...]).reshape(tb, tt, o1)
    kk = k * kk_ref[...]
    nrm = jnp.sqrt(_seg_sum(kk * kk, e))
    kk = kk / jnp.maximum(nrm, 1e-12)
    kp = k * (1.0 + (a - 1.0) * ka_ref[...])
    bon = _seg_sum(r * kp * rk_ref[...], e) * v
    r_o[...] = r.reshape(tb, tt, o1)
    k_o[...] = kp.reshape(tb, tt, o1)
    v_o[...] = v.reshape(tb, tt, o1)
    kk_o[...] = kk.reshape(tb, tt, o1)
    bh_o[...] = (kk * a).reshape(tb, tt, o1)
    bon_o[...] = bon.reshape(tb, tt, o1)


def _rwkv_prep(rw, shift, mu, w0, w2p, a0, a2p, g2, k_k, k_a, r_k, eseg, tb, tt):
    nb, t, _ = rw.shape
    row = lambda w: pl.BlockSpec((tb, tt, w), lambda b, i: (b, i, 0))
    full = lambda a: pl.BlockSpec(a.shape, lambda b, i: (0,) * a.ndim)
    out = jax.ShapeDtypeStruct((nb, t, RWKV_WIDTH), F32)
    return pl.pallas_call(
        _prep_body,
        grid=(nb // tb, t // tt),
        in_specs=[row(RW_COLS), pl.BlockSpec((tb, 1, RW_COLS), lambda b, i: (b, 0, 0)), full(mu), full(w0),
                  full(w2p), full(a0), full(a2p), full(g2), full(k_k), full(k_a), full(r_k), full(eseg)],
        out_specs=[row(RWKV_WIDTH)] * 8,
        out_shape=[out] * 8,
        scratch_shapes=[pltpu.VMEM((1, RW_COLS), F32)],
        compiler_params=_cparams(("arbitrary", "arbitrary")),
        name="rwkv_prep",
    )(rw, shift, mu, w0, w2p, a0, a2p, g2, k_k, k_a, r_k, eseg)


def _scan_body(*refs, C):
    r_ref, k_ref, v_ref, kk_ref, bh_ref, lw_ref, g_ref, bon_ref, s0_ref, lnw_ref, lnb_ref, e_ref = refs[:12]
    y_ref, sout_ref, st_ref = refs[-3:]
    ci = pl.program_id(1)
    nc = pl.num_programs(1)

    nbk = lw_ref.shape[0]

    @pl.when(ci == 0)
    def _():
        z = jnp.zeros((HEAD_R, HEAD_R), F32)
        for b in range(nbk):
            for q in range(N_HEADS_R // 2):
                top = jnp.concatenate([s0_ref[b, 2 * q], z], axis=1)
                bot = jnp.concatenate([z, s0_ref[b, 2 * q + 1]], axis=1)
                st_ref[b, q] = jnp.concatenate([top, bot], axis=0)

    ti = lax.broadcasted_iota(jnp.int32, (C, C), 0)
    si = lax.broadcasted_iota(jnp.int32, (C, C), 1)
    ltri = (si <= ti).astype(BF16)
    at, rt, bt, kt, bb, kb, g_c, vb = [], [], [], [], [], [], [], []
    for b in range(nbk):
        lw = lw_ref[b]
        hi = lw.astype(BF16)
        r1 = lw - hi.astype(F32)
        mid = r1.astype(BF16)
        lo = (r1 - mid.astype(F32)).astype(BF16)
        cs = _dot(ltri, hi) + _dot(ltri, mid) + _dot(ltri, lo)
        cend = cs[C - 1:C, :]
        kk = kk_ref[b]
        bh = bh_ref[b]
        kp = k_ref[b]
        e_neg = jnp.exp(-cs)
        e_end = jnp.exp(cend - cs)
        at.append((-kk * jnp.exp(cs - lw)).astype(BF16))
        rt.append((r_ref[b] * jnp.exp(cs)).astype(BF16))
        bt.append((bh * e_neg).astype(BF16))
        kt.append((kp * e_neg).astype(BF16))
        bb.append((bh * e_end).astype(BF16))
        kb.append((kp * e_end).astype(BF16))
        g_c.append(jnp.exp(cend))
        vb.append(v_ref[b].astype(BF16))

    C2 = 2 * C
    hm = ((lax.broadcasted_iota(jnp.int32, (C2, LANES), 0) >= C)
          == (lax.broadcasted_iota(jnp.int32, (C2, LANES), 1) >= HEAD_R))
    hm2 = ((lax.broadcasted_iota(jnp.int32, (C2, C2), 0) >= C)
           == (lax.broadcasted_iota(jnp.int32, (C2, C2), 1) >= C))
    hm128 = ((lax.broadcasted_iota(jnp.int32, (LANES, LANES), 0) >= HEAD_R)
             == (lax.broadcasted_iota(jnp.int32, (LANES, LANES), 1) >= HEAD_R))
    tt = lax.broadcasted_iota(jnp.int32, (C, C2), 0)
    ss = lax.broadcasted_iota(jnp.int32, (C, C2), 1) & (C - 1)
    strict = ss < tt
    incl = ss <= tt
    eye = (ss == tt).astype(F32)

    def bd(x, mask):
        xx = jnp.concatenate([x, x], axis=0)
        return jnp.where(mask, xx, jnp.zeros_like(xx)).astype(BF16)

    def mm(x, y):
        return _dot(x.astype(BF16), bd(y, hm2))

    npair = N_HEADS_R // 2
    chains = [(b, q) for b in range(nbk) for q in range(npair)]
    pairs = range(len(chains))
    sls = [slice(q * LANES, (q + 1) * LANES) for (_, q) in chains]
    zero = jnp.zeros((C, C2), F32)
    a_l = [at[b][:, sls[p]] for p, (b, _) in enumerate(chains)]
    r_l = [rt[b][:, sls[p]] for p, (b, _) in enumerate(chains)]
    v_l = [vb[b][:, sls[p]] for p, (b, _) in enumerate(chains)]
    ar_l = [jnp.concatenate([a_l[p], r_l[p]], axis=0) for p in pairs]
    gb_l = [_dot_nt(ar_l[p], bd(bt[chains[p][0]][:, sls[p]], hm)) for p in pairs]
    gk_l = [_dot_nt(ar_l[p], bd(kt[chains[p][0]][:, sls[p]], hm)) for p in pairs]
    n_l = [jnp.where(strict, gb_l[p][:C], zero) for p in pairs]
    aks_l = [jnp.where(strict, gk_l[p][:C], zero).astype(BF16) for p in pairs]
    rbi_l = [jnp.where(incl, gb_l[p][C:], zero).astype(BF16) for p in pairs]
    rki_l = [jnp.where(incl, gk_l[p][C:], zero).astype(BF16) for p in pairs]
    nd_l = [jnp.where((ss >> 3) == (tt >> 3), n_l[p], zero) for p in pairs]
    nd2_l = [mm(nd_l[p], nd_l[p]) for p in pairs]
    nd4_l = [mm(nd2_l[p], nd2_l[p]) for p in pairs]
    t_l = [mm(eye + nd_l[p], eye + nd2_l[p]) for p in pairs]
    t_l = [mm(t_l[p], eye + nd4_l[p]) for p in pairs]
    lvl = 3
    while (1 << lvl) < C:
        tb_ = tt >> lvl
        ml = ((tb_ & 1) == 1) & ((ss >> lvl) == tb_ - 1)
        tn_l = [mm(t_l[p], jnp.where(ml, n_l[p], zero)) for p in pairs]
        t_l = [t_l[p] + mm(tn_l[p], t_l[p]) for p in pairs]
        lvl += 1
    s_l = [st_ref[b, q] for (b, q) in chains]
    sb_l = [s.astype(BF16) for s in s_l]
    vbd_l = [bd(v_l[p], hm) for p in pairs]
    x_l = [_dot_nt(a_l[p], sb_l[p]) + _dot(aks_l[p], vbd_l[p]) for p in pairs]
    u_l = [_dot(t_l[p].astype(BF16), bd(x_l[p], hm)) for p in pairs]
    ys = [_dot_nt(r_l[p], sb_l[p]) + _dot(rbi_l[p], bd(u_l[p], hm)) + _dot(rki_l[p], vbd_l[p]) for p in pairs]
    for p, (b, q) in enumerate(chains):
        uv = jnp.concatenate([u_l[p].astype(BF16), v_l[p]], axis=0)
        bk = jnp.concatenate([bb[b][:, sls[p]], kb[b][:, sls[p]]], axis=0)
        s_add = _dot_tn(uv, bk)
        st_ref[b, q] = s_l[p] * g_c[b][:, sls[p]] + jnp.where(hm128, s_add, jnp.zeros_like(s_add))

    e = e_ref[...]
    ycat = [jnp.concatenate(ys[b * npair:(b + 1) * npair], axis=1) for b in range(nbk)]
    mean = [_seg_sum(y, e) * (1.0 / HEAD_R) for y in ycat]
    d = [ycat[b] - mean[b] for b in range(nbk)]
    var = [_seg_sum(x * x, e) * (1.0 / HEAD_R) for x in d]
    for b in range(nbk):
        yn = d[b] * lax.rsqrt(var[b] + LNX_EPS) * lnw_ref[...] + lnb_ref[...]
        y_ref[b] = (yn + bon_ref[b]) * g_ref[b]

    @pl.when(ci == nc - 1)
    def _():
        for b in range(nbk):
            for q in range(N_HEADS_R // 2):
                s = st_ref[b, q]
                sout_ref[b, 2 * q] = s[:HEAD_R, :HEAD_R]
                sout_ref[b, 2 * q + 1] = s[HEAD_R:, HEAD_R:]


def _rwkv_scan(prep, wkv_init, layer, wkv_prev, lnx_w, lnx_b, eseg, C, nbk):
    nb, t, _ = prep[0].shape
    row = pl.BlockSpec((nbk, C, RWKV_WIDTH), lambda b, c: (b, c, 0))
    sspec = pl.BlockSpec((None, nbk, N_HEADS_R, HEAD_R, HEAD_R), lambda b, c: (layer, b, 0, 0, 0))
    full = lambda a: pl.BlockSpec(a.shape, lambda b, c: (0,) * a.ndim)
    ins = list(prep) + [wkv_init, lnx_w, lnx_b, eseg]
    in_specs = [row] * 8 + [sspec, full(lnx_w), full(lnx_b), full(eseg)]
    aliases = {}
    if wkv_prev is not None:
        aliases[len(ins)] = 1
        ins.append(wkv_prev)
        in_specs.append(pl.BlockSpec(memory_space=pl.ANY))
    return pl.pallas_call(
        functools.partial(_scan_body, C=C),
        grid=(nb // nbk, t // C),
        in_specs=in_specs,
        out_specs=[row, sspec],
        out_shape=[jax.ShapeDtypeStruct((nb, t, RWKV_WIDTH), F32),
                   jax.ShapeDtypeStruct(wkv_init.shape, F32)],
        scratch_shapes=[pltpu.VMEM((nbk, N_HEADS_R // 2, LANES, LANES), F32)],
        input_output_aliases=aliases,
        compiler_params=_cparams(("arbitrary", "arbitrary")),
        name="rwkv_scan",
    )(*ins)


def _out_body(*refs, routed):
    if routed:
        (o_ref, y_ref, x_ref, g1_ref, sc_ref, sh_ref, gn_ref, wa_ref, wr_ref, rh_ref, rl_ref,
         x1_ref, h2_ref, comb_ref) = refs
    else:
        (o_ref, y_ref, x_ref, g1_ref, sc_ref, sh_ref, gn_ref, wa_ref, wr_ref, x1_ref, h2_ref) = refs
    tb, tt, _ = x_ref.shape
    rows = tb * tt
    ob = o_ref[...].reshape(rows, ATT_WIDTH).astype(BF16)
    yb = y_ref[...].reshape(rows, RWKV_WIDTH).astype(BF16)
    mix = _dot(ob, wa_ref[...]) + _dot(yb, wr_ref[...])
    x1 = x_ref[...] + g1_ref[...] * mix.reshape(tb, tt, D_MODEL)
    x1_ref[...] = x1
    ms = jnp.mean(x1 * x1, axis=-1, keepdims=True)
    h2 = x1 * lax.rsqrt(ms + NORM_EPS) * gn_ref[...]
    h2 = h2 * (1.0 + sc_ref[...]) + sh_ref[...]
    h2_ref[...] = h2.astype(BF16)
    if routed:
        hf = h2.reshape(rows, D_MODEL)
        hi, lo = _split2(hf)
        logits = _dot(hi, rh_ref[...]) + _dot(hi, rl_ref[...]) + _dot(lo, rh_ref[...])
        lane = lax.broadcasted_iota(jnp.int32, logits.shape, 1)
        logits = jnp.where(lane < N_EXPERTS, logits, NEG_BIG)
        m1 = jnp.max(logits, axis=-1, keepdims=True)
        i1 = jnp.min(jnp.where(logits == m1, lane, LANES), axis=-1, keepdims=True)
        rest = jnp.where(lane == i1, NEG_BIG, logits)
        m2 = jnp.max(rest, axis=-1, keepdims=True)
        i2 = jnp.min(jnp.where(rest == m2, lane, LANES), axis=-1, keepdims=True)
        ex = jnp.exp(m2 - m1)
        gate1 = 1.0 / (1.0 + ex)
        gate2 = ex / (1.0 + ex)
        comb = jnp.where(lane == i1, gate1, 0.0) + jnp.where(lane == i2, gate2, 0.0)
        comb_ref[...] = comb.reshape(tb, tt, LANES)


def _out_proj(o, y, x, g1, sc2, sh2, gn2, wa, wr, router, tb, tt):
    nb, t, _ = x.shape
    routed = router is not None
    row = lambda w: pl.BlockSpec((tb, tt, w), lambda b, i: (b, i, 0))
    modspec = pl.BlockSpec((tb, 1, D_MODEL), lambda b, i: (b, 0, 0))
    full = lambda a: pl.BlockSpec(a.shape, lambda b, i: (0,) * a.ndim)
    ins = [o, y, x, g1, sc2, sh2, gn2, wa, wr]
    in_specs = [row(ATT_WIDTH), row(RWKV_WIDTH), row(D_MODEL), modspec, modspec, modspec, full(gn2),
                full(wa), full(wr)]
    out_specs = [row(D_MODEL), row(D_MODEL)]
    out_shape = [jax.ShapeDtypeStruct((nb, t, D_MODEL), F32), jax.ShapeDtypeStruct((nb, t, D_MODEL), BF16)]
    if routed:
        ins += list(router)
        in_specs += [full(router[0]), full(router[1])]
        out_specs.append(row(LANES))
        out_shape.append(jax.ShapeDtypeStruct((nb, t, LANES), F32))
    return pl.pallas_call(
        functools.partial(_out_body, routed=routed),
        grid=(nb // tb, t // tt),
        in_specs=in_specs,
        out_specs=out_specs,
        out_shape=out_shape,
        compiler_params=_cparams(("arbitrary", "arbitrary")),
        name="out_proj_routed" if routed else "out_proj",
    )(*ins)


def _glu_body(*refs, routed):
    if routed:
        h_ref, x_ref, g2_ref, comb_ref, wg_ref, wu_ref, wd_ref, o_ref, acc_ref = refs
    else:
        h_ref, x_ref, g2_ref, wg_ref, wu_ref, wd_ref, o_ref, acc_ref = refs
    tb, tt, _ = x_ref.shape
    rows = tb * tt
    e = pl.program_id(2)
    ne = pl.num_programs(2)

    @pl.when(e == 0)
    def _():
        acc_ref[...] = jnp.zeros(acc_ref.shape, F32)

    hb = h_ref[...].reshape(rows, D_MODEL)
    gate = _dot(hb, wg_ref[...])
    up = _dot(hb, wu_ref[...])
    act = gate * _sigmoid(gate) * up
    if routed:
        comb = comb_ref[...].reshape(rows, LANES)
        lane = lax.broadcasted_iota(jnp.int32, comb.shape, 1)
        w = jnp.sum(jnp.where(lane == e, comb, 0.0), axis=-1, keepdims=True)
        act = act * w
    acc_ref[...] += _dot(act.astype(BF16), wd_ref[...])

    @pl.when(e == ne - 1)
    def _():
        o_ref[...] = x_ref[...] + g2_ref[...] * acc_ref[...].reshape(tb, tt, D_MODEL)


def _glu(h2, x1, g2, comb, wg, wu, wd, tb, tt):
    nb, t, _ = x1.shape
    routed = comb is not None
    row = lambda w: pl.BlockSpec((tb, tt, w), lambda b, i, e: (b, i, 0))
    modspec = pl.BlockSpec((tb, 1, D_MODEL), lambda b, i, e: (b, 0, 0))
    if routed:
        ne = N_EXPERTS
        wspec_in = pl.BlockSpec((None, D_MODEL, D_FF_E), lambda b, i, e: (e, 0, 0))
        wspec_out = pl.BlockSpec((None, D_FF_E, D_MODEL), lambda b, i, e: (e, 0, 0))
        ins = [h2, x1, g2, comb, wg, wu, wd]
        in_specs = [row(D_MODEL), row(D_MODEL), modspec, row(LANES), wspec_in, wspec_in, wspec_out]
    else:
        ne = D_FF // D_FF_E
        wspec_in = pl.BlockSpec((D_MODEL, D_FF_E), lambda b, i, e: (0, e))
        wspec_out = pl.BlockSpec((D_FF_E, D_MODEL), lambda b, i, e: (e, 0))
        ins = [h2, x1, g2, wg, wu, wd]
        in_specs = [row(D_MODEL), row(D_MODEL), modspec, wspec_in, wspec_in, wspec_out]
    return pl.pallas_call(
        functools.partial(_glu_body, routed=routed),
        grid=(nb // tb, t // tt, ne),
        in_specs=in_specs,
        out_specs=row(D_MODEL),
        out_shape=jax.ShapeDtypeStruct((nb, t, D_MODEL), F32),
        scratch_shapes=[pltpu.VMEM((tb * tt, D_MODEL), F32)],
        compiler_params=_cparams(("arbitrary", "arbitrary", "arbitrary")),
        name="glu_routed" if routed else "glu_dense",
    )(*ins)


def _prepare_weights(P):
    W = {}
    W['w_mod'] = P['w_mod'].astype(BF16)
    w_in = P['w_in'].astype(BF16)
    W['wq'] = w_in[:, :, :ATT_WIDTH]
    W['wk'] = w_in[:, :, ATT_WIDTH:2 * ATT_WIDTH]
    W['wv'] = w_in[:, :, 2 * ATT_WIDTH:3 * ATT_WIDTH]
    W['wrw'] = w_in[:, :, 3 * ATT_WIDTH:]
    depth = P['w_in'].shape[0]
    zw = jnp.zeros((depth, LORA_A, RWKV_WIDTH), BF16)
    W['w2p'] = jnp.concatenate([P['w2'].astype(BF16), zw], axis=1)
    W['a2p'] = jnp.concatenate([zw, P['a2'].astype(BF16)], axis=1)
    W['g2'] = P['g2'].astype(BF16)
    w_out = P['w_out'].astype(BF16)
    W['wo_a'] = w_out[:, :ATT_WIDTH]
    W['wo_r'] = w_out[:, ATT_WIDTH:]
    W['w_ff_gate'] = P['w_ff_gate'].astype(BF16)
    W['w_ff_up'] = P['w_ff_up'].astype(BF16)
    W['w_ff_down'] = P['w_ff_down'].astype(BF16)
    W['w_moe_gate'] = P['w_moe_gate'].astype(BF16)
    W['w_moe_up'] = P['w_moe_up'].astype(BF16)
    W['w_moe_down'] = P['w_moe_down'].astype(BF16)
    wr = jnp.pad(P['w_router'], ((0, 0), (0, 0), (0, LANES - N_EXPERTS)))
    W['router_hi'] = wr.astype(BF16)
    W['router_lo'] = (wr - W['router_hi'].astype(F32)).astype(BF16)
    seg = jnp.arange(RWKV_WIDTH) // HEAD_R
    W['eseg'] = (seg[:, None] == seg[None, :]).astype(BF16)
    return W


def _run_group(x, c, cache_k, cache_v, wkv_init, shift_init, P, W, tb, tt, scan_chunk, scan_nb, bq):
    nb, t, _ = x.shape
    depth = P['w_in'].shape[0]
    eseg = W['eseg']
    prompt = cache_k is None
    ks, vs, shifts = [], [], []
    kv_stacked = ()
    wkv_stacked = None
    for l in range(depth):
        mod = _mod(c, W['w_mod'][l], P['b_mod'][l])
        sh1, sc1, g1, sh2, sc2, g2 = [mod[:, None, i * D_MODEL:(i + 1) * D_MODEL] for i in range(6)]
        qg = jnp.tile(P['q_gain'][l].reshape(1, 2 * HD_QK), (1, N_HEADS_A))
        kg = jnp.tile(P['k_gain'][l].reshape(1, 2 * HD_QK), (1, N_HEADS_A))
        proj = _in_proj(x, sc1, sh1, P['g_norm1'][l].reshape(1, D_MODEL), W['wq'][l], W['wk'][l],
                        W['wv'][l], W['wrw'][l], qg, kg, eseg, tb, tt, l, depth,
                        kv_stacked if prompt else None)
        q, rw = proj[:2]
        lam_init = 0.8 - 0.6 * math.exp(-0.3 * l)
        lam_vecs = jnp.stack([P['lambda_q1'][l], P['lambda_k1'][l], P['lambda_q2'][l], P['lambda_k2'][l]])
        subln = P['subln'][l].reshape(1, HD_V)
        if prompt:
            kv_stacked = (proj[4], proj[5])
            o = _attn_prompt(q, proj[2], proj[3], lam_vecs, subln, lam_init, bq)
        else:
            k, v = proj[2], proj[3]
            ks.append(k.reshape(nb, t, N_HEADS_A, 2, HD_QK))
            vs.append(v.reshape(nb, t, N_HEADS_A, HD_V))
            o = _attn_sample(q, cache_k, cache_v, l, k, v, lam_vecs, subln, lam_init)
        vec = lambda a: a.reshape(1, RWKV_WIDTH)
        prep = _rwkv_prep(rw, shift_init[l], P['mu'][l].reshape(1, RW_COLS), vec(P['w0'][l]), W['w2p'][l],
                          vec(P['a0'][l]), W['a2p'][l], W['g2'][l], vec(P['k_k'][l]), vec(P['k_a'][l]),
                          vec(P['r_k'][l]), eseg, tb, tt)
        y_r, wkv_stacked = _rwkv_scan(prep, wkv_init, l, wkv_stacked, vec(P['lnx_w'][l]), vec(P['lnx_b'][l]),
                                      eseg, scan_chunk, scan_nb)
        routed = (l % 2 == 1)
        j = l // 2
        router = (W['router_hi'][j], W['router_lo'][j]) if routed else None
        res = _out_proj(o, y_r, x, g1, sc2, sh2, P['g_norm2'][l].reshape(1, D_MODEL), W['wo_a'][l], W['wo_r'][l],
                        router, tb, tt)
        if routed:
            x1, h2, comb = res
            x = _glu(h2, x1, g2, comb, W['w_moe_gate'][j], W['w_moe_up'][j], W['w_moe_down'][j], tb, tt)
        else:
            x1, h2 = res
            x = _glu(h2, x1, g2, None, W['w_ff_gate'][j], W['w_ff_up'][j], W['w_ff_down'][j], tb, tt)
        shifts.append(rw[:, t - 1:, :])
    if prompt:
        kt, vr = kv_stacked
        k_out = jnp.transpose(kt.reshape(depth, nb, N_HEADS_A, 2, HD_QK, t), (0, 1, 5, 2, 3, 4))
        v_out = vr.reshape(depth, nb, t, N_HEADS_A, HD_V)
    else:
        k_out, v_out = jnp.stack(ks), jnp.stack(vs)
    return x, k_out, v_out, wkv_stacked, jnp.stack(shifts)


def kernel(x_prompt, x_sample, cache_k, cache_v, state_wkv, state_shift, c_prompt, c_sample, w_mod, b_mod, g_norm1, g_norm2, w_in, q_gain, k_gain, lambda_q1, lambda_k1, lambda_q2, lambda_k2, subln, mu, w0, w2, a0, a2, g2, k_k, k_a, r_k, lnx_w, lnx_b, w_out, w_ff_gate, w_ff_up, w_ff_down, w_router, w_moe_gate, w_moe_up, w_moe_down):
    P = dict(w_mod=w_mod, b_mod=b_mod, g_norm1=g_norm1, g_norm2=g_norm2, w_in=w_in, q_gain=q_gain,
             k_gain=k_gain, lambda_q1=lambda_q1, lambda_k1=lambda_k1, lambda_q2=lambda_q2,
             lambda_k2=lambda_k2, subln=subln, mu=mu, w0=w0, w2=w2, a0=a0, a2=a2, g2=g2, k_k=k_k,
             k_a=k_a, r_k=r_k, lnx_w=lnx_w, lnx_b=lnx_b, w_out=w_out, w_ff_gate=w_ff_gate,
             w_ff_up=w_ff_up, w_ff_down=w_ff_down, w_router=w_router, w_moe_gate=w_moe_gate,
             w_moe_up=w_moe_up, w_moe_down=w_moe_down)
    W = _prepare_weights(P)
    depth = w_in.shape[0]
    bp, tp, _ = x_prompt.shape
    bs, ts, _ = x_sample.shape
    n_past = cache_k.shape[2]
    wkv0 = jnp.zeros((depth, bp, N_HEADS_R, HEAD_R, HEAD_R), F32)
    shift0 = jnp.zeros((depth, bp, 1, RW_COLS), F32)
    y_p, k_p, v_p, wkv_p, shift_p = _run_group(x_prompt, c_prompt, None, None, wkv0, shift0, P, W,
                                               tb=1, tt=512, scan_chunk=CHUNK, scan_nb=bp, bq=512)
    ck = jnp.transpose(cache_k, (0, 1, 3, 4, 5, 2)).reshape(depth, bs, ATT_WIDTH, n_past)
    cv = cache_v.reshape(depth, bs, n_past * N_HEADS_A, HD_V)
    y_s, k_s, v_s, wkv_s, shift_s = _run_group(x_sample, c_sample, ck, cv, state_wkv, state_shift, P, W,
                                               tb=bs, tt=ts, scan_chunk=ts, scan_nb=4, bq=None)
    return (y_p, y_s, k_p, v_p, wkv_p, shift_p, k_s, v_s, wkv_s, shift_s)
```

```python
import functools
import math

import jax
import jax.numpy as jnp
from jax import lax
from jax.experimental import pallas as pl
from jax.experimental.pallas import tpu as pltpu

F32 = jnp.float32
BF16 = jnp.bfloat16

D_MODEL = 1024
CHUNK = 64
N_HEADS_A = 4
HD_V = 128
HD_QK = 64
ATT_WIDTH = 512
RWKV_WIDTH = 512
HEAD_R = 64
N_HEADS_R = 8
LORA_W = 64
LORA_A = 64
LORA_G = 128
RW_COLS = 3 * RWKV_WIDTH + LORA_W + LORA_A + LORA_G
D_FF = 2816
N_EXPERTS = 8
D_FF_E = 1408
NORM_EPS = 1e-6
LNX_EPS = 64e-5
NEG_BIG = -1e30
LANES = 128
VMEM_LIMIT = 56 * 1024 * 1024
SCAN_GH = 2

def _cparams(sem):
    return pltpu.CompilerParams(dimension_semantics=sem, vmem_limit_bytes=VMEM_LIMIT)


def _dot(a, b):
    return jnp.dot(a, b, preferred_element_type=F32)


def _dot_nt(a, b):
    return lax.dot_general(a, b, (((1,), (1,)), ((), ())), preferred_element_type=F32)


def _dot_tn(a, b):
    return lax.dot_general(a, b, (((0,), (0,)), ((), ())), preferred_element_type=F32)


def _split2(x):
    hi = x.astype(BF16)
    lo = (x - hi.astype(F32)).astype(BF16)
    return hi, lo


def _seg_sum(x, e):
    hi, lo = _split2(x)
    return _dot(hi, e) + _dot(lo, e)


def _sigmoid(x):
    return 1.0 / (1.0 + jnp.exp(-x))


def _mod_body(c_ref, w_ref, b_ref, o_ref):
    c = c_ref[...]
    cs = c * _sigmoid(c)
    o_ref[...] = _dot(cs.astype(BF16), w_ref[...]) + b_ref[...]


def _mod(c, w_mod, b_mod):
    nb = c.shape[0]
    n = w_mod.shape[1]
    tn = 1536
    return pl.pallas_call(
        _mod_body,
        grid=(n // tn,),
        in_specs=[pl.BlockSpec((nb, D_MODEL), lambda j: (0, 0)),
                  pl.BlockSpec((D_MODEL, tn), lambda j: (0, j)),
                  pl.BlockSpec((1, tn), lambda j: (0, j))],
        out_specs=pl.BlockSpec((nb, tn), lambda j: (0, j)),
        out_shape=jax.ShapeDtypeStruct((nb, n), F32),
        compiler_params=_cparams(("arbitrary",)),
        name="mod",
    )(c, w_mod, b_mod.reshape(1, n))


def _in_body(*refs, prompt, n_alias):
    x_ref, sc_ref, sh_ref, gn_ref, wq_ref, wk_ref, wv_ref, wr_ref, qg_ref, kg_ref, e_ref = refs[:11]
    outs = refs[11 + n_alias:]
    tb, tt, _ = x_ref.shape
    x = x_ref[...]
    ms = jnp.mean(x * x, axis=-1, keepdims=True)
    h = x * lax.rsqrt(ms + NORM_EPS) * gn_ref[...]
    h = h * (1.0 + sc_ref[...]) + sh_ref[...]
    hb = h.reshape(tb * tt, D_MODEL).astype(BF16)
    e = e_ref[...]

    def group_norm(c, gain):
        msq = _seg_sum(c * c, e) * (1.0 / HD_QK)
        return c * lax.rsqrt(msq + NORM_EPS) * gain

    q = group_norm(_dot(hb, wq_ref[...]), qg_ref[...])
    k = group_norm(_dot(hb, wk_ref[...]), kg_ref[...])
    v = _dot(hb, wv_ref[...])
    rw = _dot(hb, wr_ref[...]).reshape(tb, tt, RW_COLS)
    if prompt:
        q_ref, rw_ref, kb_ref, vt_ref, kt_ref, vr_ref = outs
        kb_ref[...] = k.reshape(tb, tt, ATT_WIDTH).astype(BF16)
        vt_ref[0] = v.T.astype(BF16)
        kt_ref[0] = k.T
        for hd in range(N_HEADS_A):
            vr_ref[0, pl.ds(hd, tt, stride=N_HEADS_A), :] = v[:, hd * HD_V:(hd + 1) * HD_V]
    else:
        q_ref, rw_ref, k_ref, v_ref = outs
        k_ref[...] = k.reshape(tb, tt, ATT_WIDTH)
        v_ref[...] = v.reshape(tb, tt, ATT_WIDTH)
    q_ref[...] = q.reshape(tb, tt, ATT_WIDTH).astype(BF16)
    rw_ref[...] = rw


def _in_proj(x, sc, sh, gn, wq, wk, wv, wr, qg, kg, eseg, tb, tt, layer, depth, stacked):
    nb, t, _ = x.shape
    prompt = stacked is not None
    row = lambda w: pl.BlockSpec((tb, tt, w), lambda b, i: (b, i, 0))
    modspec = pl.BlockSpec((tb, 1, D_MODEL), lambda b, i: (b, 0, 0))
    full = lambda a: pl.BlockSpec(a.shape, lambda b, i: (0,) * a.ndim)
    ins = [x, sc, sh, gn, wq, wk, wv, wr, qg, kg, eseg]
    in_specs = [row(D_MODEL), modspec, modspec, full(gn), full(wq), full(wk), full(wv), full(wr),
                full(qg), full(kg), full(eseg)]
    out_specs = [row(ATT_WIDTH), row(RW_COLS)]
    out_shape = [jax.ShapeDtypeStruct((nb, t, ATT_WIDTH), BF16), jax.ShapeDtypeStruct((nb, t, RW_COLS), F32)]
    aliases = {}
    if prompt:
        assert tb == 1
        out_specs += [row(ATT_WIDTH), pl.BlockSpec((1, ATT_WIDTH, tt), lambda b, i: (b, 0, i)),
                      pl.BlockSpec((None, 1, ATT_WIDTH, tt), lambda b, i: (layer, b, 0, i)),
                      pl.BlockSpec((None, 1, N_HEADS_A * tt, HD_V), lambda b, i: (layer, b, i, 0))]
        out_shape += [jax.ShapeDtypeStruct((nb, t, ATT_WIDTH), BF16),
                      jax.ShapeDtypeStruct((nb, ATT_WIDTH, t), BF16),
                      jax.ShapeDtypeStruct((depth, nb, ATT_WIDTH, t), F32),
                      jax.ShapeDtypeStruct((depth, nb, N_HEADS_A * t, HD_V), F32)]
        for n, prev in enumerate(stacked):
            aliases[len(ins)] = 4 + n
            ins.append(prev)
            in_specs.append(pl.BlockSpec(memory_space=pl.ANY))
    else:
        out_specs += [row(ATT_WIDTH), row(ATT_WIDTH)]
        out_shape += [jax.ShapeDtypeStruct((nb, t, ATT_WIDTH), F32)] * 2
    return pl.pallas_call(
        functools.partial(_in_body, prompt=prompt, n_alias=len(aliases)),
        grid=(nb // tb, t // tt),
        in_specs=in_specs,
        out_specs=out_specs,
        out_shape=out_shape,
        input_output_aliases=aliases,
        compiler_params=_cparams(("arbitrary", "arbitrary")),
        name="in_proj",
    )(*ins)


def _alibi_slope(h):
    return 2.0 ** (-8.0 * (h + 1) / N_HEADS_A)


def _lambda_value(lam_ref, lam_init):
    lv = lam_ref[...]
    s1 = jnp.sum(lv[0:1] * lv[1:2], axis=-1, keepdims=True)
    s2 = jnp.sum(lv[2:3] * lv[3:4], axis=-1, keepdims=True)
    return jnp.exp(s1) - jnp.exp(s2) + lam_init


def _stack_maps(qh):
    lane = lax.broadcasted_iota(jnp.int32, qh.shape, 1)
    qs = qh * (HD_QK ** -0.5)
    zero = jnp.zeros_like(qs)
    return jnp.concatenate([jnp.where(lane < HD_QK, qs, zero), jnp.where(lane >= HD_QK, qs, zero)],
                           axis=0)


def _sub_norm(o, gain, lam_init):
    ms = jnp.mean(o * o, axis=-1, keepdims=True)
    return o * lax.rsqrt(ms + NORM_EPS) * gain * (1.0 - lam_init)


def _attn_body(it_ref, jt_ref, q_ref, k_ref, vt_ref, lam_ref, sub_ref, o_ref, q2_ref, m_ref, l_ref,
               acc_ref, s_ref, p_ref, *, bq, lam_init):
    step_id = pl.program_id(1)
    i = it_ref[step_id]
    j = jt_ref[step_id]
    bk = bq
    nq2 = 2 * bq
    kc = LANES
    lane = lax.broadcasted_iota(jnp.int32, (bq, LANES), 1)

    @pl.when(j == 0)
    def _():
        m_ref[...] = jnp.full(m_ref.shape, NEG_BIG, F32)
        l_ref[...] = jnp.zeros(l_ref.shape, F32)
        acc_ref[...] = jnp.zeros(acc_ref.shape, F32)
        lane2 = lax.broadcasted_iota(jnp.int32, (nq2, LANES), 1)
        for h in range(N_HEADS_A):
            slope = _alibi_slope(h)
            feat = jnp.where(lane2 == 0, slope * CHUNK, jnp.where(lane2 == 1, slope, 0.0)).astype(BF16)
            q2_ref[h] = jnp.concatenate([_stack_maps(q_ref[0, :, h * HD_V:(h + 1) * HD_V]), feat], axis=1)

    def step(diag):
        krel = lax.broadcasted_iota(jnp.int32, (bk, LANES), 0) + (j - i) * bq
        kfeat = jnp.where(lane == 0, krel >> 6, jnp.where(lane == 1, krel & (CHUNK - 1), 0))
        kfeat = kfeat.astype(F32).astype(BF16)
        subs = range(bk // kc)
        npair = nq2 // (2 * LANES)

        def qk(h, g):
            kh = jnp.concatenate([k_ref[0, :, h * HD_V:(h + 1) * HD_V], kfeat], axis=1)
            st2 = _dot_nt(kh, q2_ref[h, g * 2 * LANES:(g + 1) * 2 * LANES, :])
            if diag:
                c = lax.broadcasted_iota(jnp.int32, (bk, 2 * LANES), 0)
                r = lax.broadcasted_iota(jnp.int32, (bk, 2 * LANES), 1) + (g * 2 * LANES) % bq
                ahead = jnp.maximum(c - r, 0).astype(F32) * (-2.0 * _alibi_slope(h))
                st2 = jnp.where((c >> 6) <= (r >> 6), st2 + ahead, NEG_BIG)
            s_ref[h % 2, 2 * g] = st2[:, :LANES]
            s_ref[h % 2, 2 * g + 1] = st2[:, LANES:]
            return jnp.max(st2, axis=0, keepdims=True)

        def softmax(h, g, m_blk):
            alpha = []
            for n, st in enumerate((2 * g, 2 * g + 1)):
                sl = slice(st * LANES, (st + 1) * LANES)
                m_old = m_ref[h, :, sl]
                m_new = jnp.maximum(m_old, m_blk[:, n * LANES:(n + 1) * LANES])
                a = jnp.exp(m_old - m_new)
                psum = None
                for kb in subs:
                    p = jnp.exp(s_ref[h % 2, st, kb * kc:(kb + 1) * kc, :] - m_new)
                    psum = p if psum is None else psum + p
                    p_ref[st, kb * kc:(kb + 1) * kc, :] = p.astype(BF16)
                l_ref[h, :, sl] = a * l_ref[h, :, sl] + jnp.sum(psum, axis=0, keepdims=True)
                m_ref[h, :, sl] = m_new
                alpha.append(a)
            return jnp.concatenate(alpha, axis=1)

        def pv(h, g, a_row):
            p2 = jnp.concatenate([p_ref[2 * g], p_ref[2 * g + 1]], axis=1)
            sl = slice(g * 2 * LANES, (g + 1) * 2 * LANES)
            acc_ref[h, :, sl] = acc_ref[h, :, sl] * a_row + _dot(vt_ref[0, h * HD_V:(h + 1) * HD_V, :], p2)

        m_blk = {(0, g): qk(0, g) for g in range(npair)}
        for h in range(N_HEADS_A):
            for g in range(npair):
                if h + 1 < N_HEADS_A:
                    m_blk[(h + 1, g)] = qk(h + 1, g)
                a_row = softmax(h, g, m_blk.pop((h, g)))
                pv(h, g, a_row)

    @pl.when(j < i)
    def _():
        step(False)

    @pl.when(j == i)
    def _():
        step(True)
        lam = _lambda_value(lam_ref, lam_init)
        for h in range(N_HEADS_A):
            o1 = acc_ref[h, :, :bq] / l_ref[h, :, :bq]
            o2 = acc_ref[h, :, bq:] / l_ref[h, :, bq:]
            o = (o1 - lam * o2).T
            o_ref[0, :, h * HD_V:(h + 1) * HD_V] = _sub_norm(o, sub_ref[...], lam_init)


def _attn_prompt(q, kb, vt, lam_vecs, subln, lam_init, bq):
    nb, t, _ = q.shape
    nq = t // bq
    pairs = [(i, j) for i in range(nq) for j in range(i + 1)]
    i_tab = jnp.asarray([p[0] for p in pairs], jnp.int32)
    j_tab = jnp.asarray([p[1] for p in pairs], jnp.int32)
    qspec = pl.BlockSpec((1, bq, ATT_WIDTH), lambda b, s, it, jt: (b, it[s], 0))
    kspec = pl.BlockSpec((1, bq, ATT_WIDTH), lambda b, s, it, jt: (b, jt[s], 0))
    vspec = pl.BlockSpec((1, ATT_WIDTH, bq), lambda b, s, it, jt: (b, 0, jt[s]))
    grid_spec = pltpu.PrefetchScalarGridSpec(
        num_scalar_prefetch=2,
        grid=(nb, len(pairs)),
        in_specs=[qspec, kspec, vspec,
                  pl.BlockSpec(lam_vecs.shape, lambda b, s, it, jt: (0, 0)),
                  pl.BlockSpec(subln.shape, lambda b, s, it, jt: (0, 0))],
        out_specs=qspec,
        scratch_shapes=[pltpu.VMEM((N_HEADS_A, 2 * bq, 2 * HD_V), BF16),
                        pltpu.VMEM((N_HEADS_A, 1, 2 * bq), F32),
                        pltpu.VMEM((N_HEADS_A, 1, 2 * bq), F32),
                        pltpu.VMEM((N_HEADS_A, HD_V, 2 * bq), F32),
                        pltpu.VMEM((2, 2 * bq // LANES, bq, LANES), F32),
                        pltpu.VMEM((2 * bq // LANES, bq, LANES), BF16)])
    return pl.pallas_call(
        functools.partial(_attn_body, bq=bq, lam_init=lam_init),
        grid_spec=grid_spec,
        out_shape=jax.ShapeDtypeStruct((nb, t, ATT_WIDTH), F32),
        compiler_params=_cparams(("arbitrary", "arbitrary")),
        name="attn_prompt",
    )(i_tab, j_tab, q, kb, vt, lam_vecs, subln)


def _attn_dec_body(q_ref, ck_ref, cv_ref, kn_ref, vn_ref, lam_ref, sub_ref, o_ref, *, lam_init):
    tq = q_ref.shape[1]
    n_past = ck_ref.shape[1]
    lam = _lambda_value(lam_ref, lam_init)
    r = lax.broadcasted_iota(jnp.int32, (tq, n_past), 0)
    c = lax.broadcasted_iota(jnp.int32, (tq, n_past), 1)
    dist_c = jnp.abs((r + n_past - c).astype(F32))
    rn = lax.broadcasted_iota(jnp.int32, (tq, tq), 0)
    cn = lax.broadcasted_iota(jnp.int32, (tq, tq), 1)
    dist_n = jnp.abs((rn - cn).astype(F32))
    for h in range(N_HEADS_A):
        sl = slice(h * HD_V, (h + 1) * HD_V)
        q2 = _stack_maps(q_ref[0, :, sl])
        vc = cv_ref[pl.ds(h, n_past, stride=N_HEADS_A), :].astype(BF16)
        kn = kn_ref[0, :, sl].astype(BF16)
        vn = vn_ref[0, :, sl].astype(BF16)
        sn_all = _dot_nt(q2, kn)
        slope = -_alibi_slope(h)
        outs = []
        sc_all = _dot(q2, ck_ref[sl, :].astype(BF16))
        for mp in range(2):
            s_c = sc_all[mp * tq:(mp + 1) * tq] + dist_c * slope
            s_n = sn_all[mp * tq:(mp + 1) * tq] + dist_n * slope
            m = jnp.maximum(jnp.max(s_c, axis=-1, keepdims=True), jnp.max(s_n, axis=-1, keepdims=True))
            p_c = jnp.exp(s_c - m)
            p_n = jnp.exp(s_n - m)
            l = jnp.sum(p_c, axis=-1, keepdims=True) + jnp.sum(p_n, axis=-1, keepdims=True)
            acc = _dot(p_c.astype(BF16), vc) + _dot(p_n.astype(BF16), vn)
            outs.append(acc / l)
        o = outs[0] - lam * outs[1]
        o_ref[0, :, sl] = _sub_norm(o, sub_ref[...], lam_init)


def _attn_sample(q, cache_k, cache_v, layer, k_new, v_new, lam_vecs, subln, lam_init):
    nb, tq, _ = q.shape
    n_past = cache_k.shape[3]
    row = pl.BlockSpec((1, tq, ATT_WIDTH), lambda b: (b, 0, 0))
    kspec = pl.BlockSpec((None, None, ATT_WIDTH, n_past), lambda b: (layer, b, 0, 0))
    vspec = pl.BlockSpec((None, None, n_past * N_HEADS_A, HD_V), lambda b: (layer, b, 0, 0))
    return pl.pallas_call(
        functools.partial(_attn_dec_body, lam_init=lam_init),
        grid=(nb,),
        in_specs=[row, kspec, vspec, row, row,
                  pl.BlockSpec(lam_vecs.shape, lambda b: (0, 0)),
                  pl.BlockSpec(subln.shape, lambda b: (0, 0))],
        out_specs=row,
        out_shape=jax.ShapeDtypeStruct((nb, tq, ATT_WIDTH), F32),
        compiler_params=_cparams(("arbitrary",)),
        name="attn_sample",
    )(q, cache_k, cache_v, k_new, v_new, lam_vecs, subln)


def _prep_body(rw_ref, shift_ref, mu_ref, w0_ref, w2_ref, a0_ref, a2_ref, g2_ref, kk_ref, ka_ref,
               rk_ref, e_ref, r_o, k_o, v_o, kk_o, bh_o, lw_o, g_o, bon_o, carry_ref):
    tb, tt, w = rw_ref.shape
    i = pl.program_id(1)
    rows = tb * tt
    cols = rw_ref[...].reshape(rows, w)
    rolled = pltpu.roll(cols, 1, 0)
    rowi = lax.broadcasted_iota(jnp.int32, (rows, 1), 0)
    if tb == 1:
        @pl.when(i == 0)
        def _():
            carry_ref[...] = shift_ref[0]

        prev = jnp.where(rowi == 0, carry_ref[...], rolled)
        carry_ref[...] = cols[tt - 1:tt, :]
    else:
        sh = jnp.broadcast_to(shift_ref[...], (tb, tt, w)).reshape(rows, w)
        prev = jnp.where((rowi % tt) == 0, sh, rolled)
    xs = cols + (prev - cols) * mu_ref[...]
    o1, o2, o3 = RWKV_WIDTH, 2 * RWKV_WIDTH, 3 * RWKV_WIDTH
    r = xs[:, :o1]
    k = xs[:, o1:o2]
    v = xs[:, o2:o3]
    da = xs[:, o3:o3 + LORA_W + LORA_A]
    gd = xs[:, o3 + LORA_W + LORA_A:]
    e = e_ref[...]
    z = w0_ref[...] + _dot(jnp.tanh(da).astype(BF16), w2_ref[...])
    lw_o[...] = (-math.exp(-0.5) * _sigmoid(z)).reshape(tb, tt, o1)
    a = _sigmoid(a0_ref[...] + _dot(da.astype(BF16), a2_ref[...]))
    g_o[...] = _dot(_sigmoid(gd).astype(BF16), g2_ref[...]).reshape(tb, tt, o1)
    kk = k * kk_ref[...]
    nrm = jnp.sqrt(_seg_sum(kk * kk, e))
    kk = kk / jnp.maximum(nrm, 1e-12)
    kp = k * (1.0 + (a - 1.0) * ka_ref[...])
    bon = _seg_sum(r * kp * rk_ref[...], e) * v
    r_o[...] = r.reshape(tb, tt, o1)
    k_o[...] = kp.reshape(tb, tt, o1)
    v_o[...] = v.reshape(tb, tt, o1)
    kk_o[...] = kk.reshape(tb, tt, o1)
    bh_o[...] = (kk * a).reshape(tb, tt, o1)
    bon_o[...] = bon.reshape(tb, tt, o1)


def _rwkv_prep(rw, shift, mu, w0, w2p, a0, a2p, g2, k_k, k_a, r_k, eseg, tb, tt):
    nb, t, _ = rw.shape
    row = lambda w: pl.BlockSpec((tb, tt, w), lambda b, i: (b, i, 0))
    full = lambda a: pl.BlockSpec(a.shape, lambda b, i: (0,) * a.ndim)
    out = jax.ShapeDtypeStruct((nb, t, RWKV_WIDTH), F32)
    return pl.pallas_call(
        _prep_body,
        grid=(nb // tb, t // tt),
        in_specs=[row(RW_COLS), pl.BlockSpec((tb, 1, RW_COLS), lambda b, i: (b, 0, 0)), full(mu), full(w0),
                  full(w2p), full(a0), full(a2p), full(g2), full(k_k), full(k_a), full(r_k), full(eseg)],
        out_specs=[row(RWKV_WIDTH)] * 8,
        out_shape=[out] * 8,
        scratch_shapes=[pltpu.VMEM((1, RW_COLS), F32)],
        compiler_params=_cparams(("arbitrary", "arbitrary")),
        name="rwkv_prep",
    )(rw, shift, mu, w0, w2p, a0, a2p, g2, k_k, k_a, r_k, eseg)


def _scan_body(*refs, C, nsub):
    r_ref, k_ref, v_ref, kk_ref, bh_ref, lw_ref, g_ref, bon_ref, s0_ref, lnw_ref, lnb_ref, e_ref = refs[:12]
    y_ref, sout_ref, st_ref = refs[-3:]
    ci = pl.program_id(1)
    nc = pl.num_programs(1)

    nbk = lw_ref.shape[0]

    @pl.when(ci == 0)
    def _():
        z = jnp.zeros((HEAD_R, HEAD_R), F32)
        for b in range(nbk):
            for q in range(N_HEADS_R // SCAN_GH):
                blocks = [jnp.concatenate([s0_ref[b, SCAN_GH * q + hh] if hc == hh else z
                                           for hc in range(SCAN_GH)], axis=1) for hh in range(SCAN_GH)]
                st_ref[b, q] = jnp.concatenate(blocks, axis=0)

    ti = lax.broadcasted_iota(jnp.int32, (C, C), 0)
    si = lax.broadcasted_iota(jnp.int32, (C, C), 1)
    ltri = (si <= ti).astype(BF16)
    segs = [(b, u) for u in range(nsub) for b in range(nbk)]
    rows_of = lambda u: slice(u * C, (u + 1) * C)
    at, rt, bt, kt, bb, kb, g_c, vb = [], [], [], [], [], [], [], []
    for (b, u) in segs:
        rw = rows_of(u)
        lw = lw_ref[b, rw, :]
        hi = lw.astype(BF16)
        r1 = lw - hi.astype(F32)
        mid = r1.astype(BF16)
        lo = (r1 - mid.astype(F32)).astype(BF16)
        cs = _dot(ltri, hi) + _dot(ltri, mid) + _dot(ltri, lo)
        cend = cs[C - 1:C, :]
        kk = kk_ref[b, rw, :]
        bh = bh_ref[b, rw, :]
        kp = k_ref[b, rw, :]
        e_neg = jnp.exp(-cs)
        e_end = jnp.exp(cend - cs)
        at.append((-kk * jnp.exp(cs - lw)).astype(BF16))
        rt.append((r_ref[b, rw, :] * jnp.exp(cs)).astype(BF16))
        bt.append((bh * e_neg).astype(BF16))
        kt.append((kp * e_neg).astype(BF16))
        bb.append((bh * e_end).astype(BF16))
        kb.append((kp * e_end).astype(BF16))
        g_c.append(jnp.exp(cend))
        vb.append(v_ref[b, rw, :].astype(BF16))

    C2 = SCAN_GH * C
    gw = SCAN_GH * HEAD_R
    log_c = C.bit_length() - 1
    log_h = HEAD_R.bit_length() - 1
    hm = ((lax.broadcasted_iota(jnp.int32, (C2, gw), 0) >> log_c)
          == (lax.broadcasted_iota(jnp.int32, (C2, gw), 1) >> log_h))
    hm2 = ((lax.broadcasted_iota(jnp.int32, (C2, C2), 0) >> log_c)
           == (lax.broadcasted_iota(jnp.int32, (C2, C2), 1) >> log_c))
    hm_state = ((lax.broadcasted_iota(jnp.int32, (gw, gw), 0) >> log_h)
                == (lax.broadcasted_iota(jnp.int32, (gw, gw), 1) >> log_h))
    tt = lax.broadcasted_iota(jnp.int32, (C, C2), 0)
    ss = lax.broadcasted_iota(jnp.int32, (C, C2), 1) & (C - 1)
    strict = ss < tt
    incl = ss <= tt
    eye = (ss == tt).astype(F32)

    def bd(x, mask):
        xx = jnp.concatenate([x] * SCAN_GH, axis=0)
        return jnp.where(mask, xx, jnp.zeros_like(xx)).astype(BF16)

    def mm(x, y):
        return _dot(x.astype(BF16), bd(y, hm2))

    npair = N_HEADS_R // SCAN_GH
    chains = [(sg, q) for sg in range(len(segs)) for q in range(npair)]
    pairs = range(len(chains))
    sls = [slice(q * gw, (q + 1) * gw) for (_, q) in chains]
    zero = jnp.zeros((C, C2), F32)
    a_l = [at[sg][:, sls[p]] for p, (sg, _) in enumerate(chains)]
    r_l = [rt[sg][:, sls[p]] for p, (sg, _) in enumerate(chains)]
    v_l = [vb[sg][:, sls[p]] for p, (sg, _) in enumerate(chains)]
    ar_l = [jnp.concatenate([a_l[p], r_l[p]], axis=0) for p in pairs]
    gb_l = [_dot_nt(ar_l[p], bd(bt[chains[p][0]][:, sls[p]], hm)) for p in pairs]
    gk_l = [_dot_nt(ar_l[p], bd(kt[chains[p][0]][:, sls[p]], hm)) for p in pairs]
    n_l = [jnp.where(strict, gb_l[p][:C], zero) for p in pairs]
    aks_l = [jnp.where(strict, gk_l[p][:C], zero).astype(BF16) for p in pairs]
    rbi_l = [jnp.where(incl, gb_l[p][C:], zero).astype(BF16) for p in pairs]
    rki_l = [jnp.where(incl, gk_l[p][C:], zero).astype(BF16) for p in pairs]
    nd_l = [jnp.where((ss >> 3) == (tt >> 3), n_l[p], zero) for p in pairs]
    nd2_l = [mm(nd_l[p], nd_l[p]) for p in pairs]
    nd4_l = [mm(nd2_l[p], nd2_l[p]) for p in pairs]
    t_l = [mm(eye + nd_l[p], eye + nd2_l[p]) for p in pairs]
    t_l = [mm(t_l[p], eye + nd4_l[p]) for p in pairs]
    lvl = 3
    while (1 << lvl) < C:
        tb_ = tt >> lvl
        ml = ((tb_ & 1) == 1) & ((ss >> lvl) == tb_ - 1)
        tn_l = [mm(t_l[p], jnp.where(ml, n_l[p], zero)) for p in pairs]
        t_l = [t_l[p] + mm(tn_l[p], t_l[p]) for p in pairs]
        lvl += 1
    vbd_l = [bd(v_l[p], hm) for p in pairs]
    t_l = [t.astype(BF16) for t in t_l]
    ys = [None] * len(chains)
    for u in range(nsub):
        cur = [p for p in pairs if segs[chains[p][0]][1] == u]
        s_l = {p: st_ref[segs[chains[p][0]][0], chains[p][1]] for p in cur}
        sb_l = {p: s_l[p].astype(BF16) for p in cur}
        x_l = {p: _dot_nt(a_l[p], sb_l[p]) + _dot(aks_l[p], vbd_l[p]) for p in cur}
        u_l = {p: _dot(t_l[p], bd(x_l[p], hm)) for p in cur}
        for p in cur:
            ys[p] = _dot_nt(r_l[p], sb_l[p]) + _dot(rbi_l[p], bd(u_l[p], hm)) + _dot(rki_l[p], vbd_l[p])
        for p in cur:
            sg, q = chains[p]
            uv = jnp.concatenate([u_l[p].astype(BF16), v_l[p]], axis=0)
            bk = jnp.concatenate([bb[sg][:, sls[p]], kb[sg][:, sls[p]]], axis=0)
            s_add = _dot_tn(uv, bk)
            st_ref[segs[sg][0], q] = (s_l[p] * g_c[sg][:, sls[p]]
                                      + jnp.where(hm_state, s_add, jnp.zeros_like(s_add)))

    e = e_ref[...]
    ycat = [jnp.concatenate(ys[sg * npair:(sg + 1) * npair], axis=1) for sg in range(len(segs))]
    mean = [_seg_sum(y, e) * (1.0 / HEAD_R) for y in ycat]
    d = [ycat[sg] - mean[sg] for sg in range(len(segs))]
    var = [_seg_sum(x * x, e) * (1.0 / HEAD_R) for x in d]
    for sg, (b, u) in enumerate(segs):
        yn = d[sg] * lax.rsqrt(var[sg] + LNX_EPS) * lnw_ref[...] + lnb_ref[...]
        y_ref[b, rows_of(u), :] = (yn + bon_ref[b, rows_of(u), :]) * g_ref[b, rows_of(u), :]

    @pl.when(ci == nc - 1)
    def _():
        for b in range(nbk):
            for q in range(N_HEADS_R // SCAN_GH):
                s = st_ref[b, q]
                for hh in range(SCAN_GH):
                    blk = slice(hh * HEAD_R, (hh + 1) * HEAD_R)
                    sout_ref[b, SCAN_GH * q + hh] = s[blk, blk]


def _rwkv_scan(prep, wkv_init, layer, wkv_prev, lnx_w, lnx_b, eseg, C, nbk, nsub):
    nb, t, _ = prep[0].shape
    row = pl.BlockSpec((nbk, C * nsub, RWKV_WIDTH), lambda b, c: (b, c, 0))
    sspec = pl.BlockSpec((None, nbk, N_HEADS_R, HEAD_R, HEAD_R), lambda b, c: (layer, b, 0, 0, 0))
    full = lambda a: pl.BlockSpec(a.shape, lambda b, c: (0,) * a.ndim)
    ins = list(prep) + [wkv_init, lnx_w, lnx_b, eseg]
    in_specs = [row] * 8 + [sspec, full(lnx_w), full(lnx_b), full(eseg)]
    aliases = {}
    if wkv_prev is not None:
        aliases[len(ins)] = 1
        ins.append(wkv_prev)
        in_specs.append(pl.BlockSpec(memory_space=pl.ANY))
    return pl.pallas_call(
        functools.partial(_scan_body, C=C, nsub=nsub),
        grid=(nb // nbk, t // (C * nsub)),
        in_specs=in_specs,
        out_specs=[row, sspec],
        out_shape=[jax.ShapeDtypeStruct((nb, t, RWKV_WIDTH), F32),
                   jax.ShapeDtypeStruct(wkv_init.shape, F32)],
        scratch_shapes=[pltpu.VMEM((nbk, N_HEADS_R // SCAN_GH, SCAN_GH * HEAD_R, SCAN_GH * HEAD_R), F32)],
        input_output_aliases=aliases,
        compiler_params=_cparams(("arbitrary", "arbitrary")),
        name="rwkv_scan",
    )(*ins)


def _out_body(*refs, routed):
    if routed:
        (o_ref, y_ref, x_ref, g1_ref, sc_ref, sh_ref, gn_ref, wa_ref, wr_ref, rh_ref, rl_ref,
         x1_ref, h2_ref, comb_ref) = refs
    else:
        (o_ref, y_ref, x_ref, g1_ref, sc_ref, sh_ref, gn_ref, wa_ref, wr_ref, x1_ref, h2_ref) = refs
    tb, tt, _ = x_ref.shape
    rows = tb * tt
    ob = o_ref[...].reshape(rows, ATT_WIDTH).astype(BF16)
    yb = y_ref[...].reshape(rows, RWKV_WIDTH).astype(BF16)
    mix = _dot(ob, wa_ref[...]) + _dot(yb, wr_ref[...])
    x1 = x_ref[...] + g1_ref[...] * mix.reshape(tb, tt, D_MODEL)
    x1_ref[...] = x1
    ms = jnp.mean(x1 * x1, axis=-1, keepdims=True)
    h2 = x1 * lax.rsqrt(ms + NORM_EPS) * gn_ref[...]
    h2 = h2 * (1.0 + sc_ref[...]) + sh_ref[...]
    h2_ref[...] = h2.astype(BF16)
    if routed:
        hf = h2.reshape(rows, D_MODEL)
        hi, lo = _split2(hf)
        logits = _dot(hi, rh_ref[...]) + _dot(hi, rl_ref[...]) + _dot(lo, rh_ref[...])
        lane = lax.broadcasted_iota(jnp.int32, logits.shape, 1)
        logits = jnp.where(lane < N_EXPERTS, logits, NEG_BIG)
        m1 = jnp.max(logits, axis=-1, keepdims=True)
        i1 = jnp.min(jnp.where(logits == m1, lane, LANES), axis=-1, keepdims=True)
        rest = jnp.where(lane == i1, NEG_BIG, logits)
        m2 = jnp.max(rest, axis=-1, keepdims=True)
        i2 = jnp.min(jnp.where(rest == m2, lane, LANES), axis=-1, keepdims=True)
        ex = jnp.exp(m2 - m1)
        gate1 = 1.0 / (1.0 + ex)
        gate2 = ex / (1.0 + ex)
        comb = jnp.where(lane == i1, gate1, 0.0) + jnp.where(lane == i2, gate2, 0.0)
        comb_ref[...] = comb.reshape(tb, tt, LANES)


def _out_proj(o, y, x, g1, sc2, sh2, gn2, wa, wr, router, tb, tt):
    nb, t, _ = x.shape
    routed = router is not None
    row = lambda w: pl.BlockSpec((tb, tt, w), lambda b, i: (b, i, 0))
    modspec = pl.BlockSpec((tb, 1, D_MODEL), lambda b, i: (b, 0, 0))
    full = lambda a: pl.BlockSpec(a.shape, lambda b, i: (0,) * a.ndim)
    ins = [o, y, x, g1, sc2, sh2, gn2, wa, wr]
    in_specs = [row(ATT_WIDTH), row(RWKV_WIDTH), row(D_MODEL), modspec, modspec, modspec, full(gn2),
                full(wa), full(wr)]
    out_specs = [row(D_MODEL), row(D_MODEL)]
    out_shape = [jax.ShapeDtypeStruct((nb, t, D_MODEL), F32), jax.ShapeDtypeStruct((nb, t, D_MODEL), BF16)]
    if routed:
        ins += list(router)
        in_specs += [full(router[0]), full(router[1])]
        out_specs.append(row(LANES))
        out_shape.append(jax.ShapeDtypeStruct((nb, t, LANES), F32))
    return pl.pallas_call(
        functools.partial(_out_body, routed=routed),
        grid=(nb // tb, t // tt),
        in_specs=in_specs,
        out_specs=out_specs,
        out_shape=out_shape,
        compiler_params=_cparams(("arbitrary", "arbitrary")),
        name="out_proj_routed" if routed else "out_proj",
    )(*ins)


def _glu_body(*refs, routed):
    if routed:
        h_ref, x_ref, g2_ref, comb_ref, wg_ref, wu_ref, wd_ref, o_ref, acc_ref = refs
    else:
        h_ref, x_ref, g2_ref, wg_ref, wu_ref, wd_ref, o_ref, acc_ref = refs
    tb, tt, _ = x_ref.shape
    rows = tb * tt
    e = pl.program_id(2)
    ne = pl.num_programs(2)

    @pl.when(e == 0)
    def _():
        acc_ref[...] = jnp.zeros(acc_ref.shape, F32)

    hb = h_ref[...].reshape(rows, D_MODEL)
    gate = _dot(hb, wg_ref[...])
    up = _dot(hb, wu_ref[...])
    act = gate * _sigmoid(gate) * up
    if routed:
        comb = comb_ref[...].reshape(rows, LANES)
        lane = lax.broadcasted_iota(jnp.int32, comb.shape, 1)
        w = jnp.sum(jnp.where(lane == e, comb, 0.0), axis=-1, keepdims=True)
        act = act * w
    acc_ref[...] += _dot(act.astype(BF16), wd_ref[...])

    @pl.when(e == ne - 1)
    def _():
        o_ref[...] = x_ref[...] + g2_ref[...] * acc_ref[...].reshape(tb, tt, D_MODEL)


def _glu(h2, x1, g2, comb, wg, wu, wd, tb, tt):
    nb, t, _ = x1.shape
    routed = comb is not None
    row = lambda w: pl.BlockSpec((tb, tt, w), lambda b, i, e: (b, i, 0))
    modspec = pl.BlockSpec((tb, 1, D_MODEL), lambda b, i, e: (b, 0, 0))
    if routed:
        ne = N_EXPERTS
        wspec_in = pl.BlockSpec((None, D_MODEL, D_FF_E), lambda b, i, e: (e, 0, 0))
        wspec_out = pl.BlockSpec((None, D_FF_E, D_MODEL), lambda b, i, e: (e, 0, 0))
        ins = [h2, x1, g2, comb, wg, wu, wd]
        in_specs = [row(D_MODEL), row(D_MODEL), modspec, row(LANES), wspec_in, wspec_in, wspec_out]
    else:
        ne = D_FF // D_FF_E
        wspec_in = pl.BlockSpec((D_MODEL, D_FF_E), lambda b, i, e: (0, e))
        wspec_out = pl.BlockSpec((D_FF_E, D_MODEL), lambda b, i, e: (e, 0))
        ins = [h2, x1, g2, wg, wu, wd]
        in_specs = [row(D_MODEL), row(D_MODEL), modspec, wspec_in, wspec_in, wspec_out]
    return pl.pallas_call(
        functools.partial(_glu_body, routed=routed),
        grid=(nb // tb, t // tt, ne),
        in_specs=in_specs,
        out_specs=row(D_MODEL),
        out_shape=jax.ShapeDtypeStruct((nb, t, D_MODEL), F32),
        scratch_shapes=[pltpu.VMEM((tb * tt, D_MODEL), F32)],
        compiler_params=_cparams(("arbitrary", "arbitrary", "arbitrary")),
        name="glu_routed" if routed else "glu_dense",
    )(*ins)


def _prepare_weights(P):
    W = {}
    W['w_mod'] = P['w_mod'].astype(BF16)
    w_in = P['w_in'].astype(BF16)
    W['wq'] = w_in[:, :, :ATT_WIDTH]
    W['wk'] = w_in[:, :, ATT_WIDTH:2 * ATT_WIDTH]
    W['wv'] = w_in[:, :, 2 * ATT_WIDTH:3 * ATT_WIDTH]
    W['wrw'] = w_in[:, :, 3 * ATT_WIDTH:]
    depth = P['w_in'].shape[0]
    zw = jnp.zeros((depth, LORA_A, RWKV_WIDTH), BF16)
    W['w2p'] = jnp.concatenate([P['w2'].astype(BF16), zw], axis=1)
    W['a2p'] = jnp.concatenate([zw, P['a2'].astype(BF16)], axis=1)
    W['g2'] = P['g2'].astype(BF16)
    w_out = P['w_out'].astype(BF16)
    W['wo_a'] = w_out[:, :ATT_WIDTH]
    W['wo_r'] = w_out[:, ATT_WIDTH:]
    W['w_ff_gate'] = P['w_ff_gate'].astype(BF16)
    W['w_ff_up'] = P['w_ff_up'].astype(BF16)
    W['w_ff_down'] = P['w_ff_down'].astype(BF16)
    W['w_moe_gate'] = P['w_moe_gate'].astype(BF16)
    W['w_moe_up'] = P['w_moe_up'].astype(BF16)
    W['w_moe_down'] = P['w_moe_down'].astype(BF16)
    wr = jnp.pad(P['w_router'], ((0, 0), (0, 0), (0, LANES - N_EXPERTS)))
    W['router_hi'] = wr.astype(BF16)
    W['router_lo'] = (wr - W['router_hi'].astype(F32)).astype(BF16)
    seg = jnp.arange(RWKV_WIDTH) // HEAD_R
    W['eseg'] = (seg[:, None] == seg[None, :]).astype(BF16)
    return W


def _run_group(x, c, cache_k, cache_v, wkv_init, shift_init, P, W, tb, tt, scan_chunk, scan_nb, scan_nsub, bq):
    nb, t, _ = x.shape
    depth = P['w_in'].shape[0]
    eseg = W['eseg']
    prompt = cache_k is None
    ks, vs, shifts = [], [], []
    kv_stacked = ()
    wkv_stacked = None
    for l in range(depth):
        mod = _mod(c, W['w_mod'][l], P['b_mod'][l])
        sh1, sc1, g1, sh2, sc2, g2 = [mod[:, None, i * D_MODEL:(i + 1) * D_MODEL] for i in range(6)]
        qg = jnp.tile(P['q_gain'][l].reshape(1, 2 * HD_QK), (1, N_HEADS_A))
        kg = jnp.tile(P['k_gain'][l].reshape(1, 2 * HD_QK), (1, N_HEADS_A))
        proj = _in_proj(x, sc1, sh1, P['g_norm1'][l].reshape(1, D_MODEL), W['wq'][l], W['wk'][l],
                        W['wv'][l], W['wrw'][l], qg, kg, eseg, tb, tt, l, depth,
                        kv_stacked if prompt else None)
        q, rw = proj[:2]
        lam_init = 0.8 - 0.6 * math.exp(-0.3 * l)
        lam_vecs = jnp.stack([P['lambda_q1'][l], P['lambda_k1'][l], P['lambda_q2'][l], P['lambda_k2'][l]])
        subln = P['subln'][l].reshape(1, HD_V)
        if prompt:
            kv_stacked = (proj[4], proj[5])
            o = _attn_prompt(q, proj[2], proj[3], lam_vecs, subln, lam_init, bq)
        else:
            k, v = proj[2], proj[3]
            ks.append(k.reshape(nb, t, N_HEADS_A, 2, HD_QK))
            vs.append(v.reshape(nb, t, N_HEADS_A, HD_V))
            o = _attn_sample(q, cache_k, cache_v, l, k, v, lam_vecs, subln, lam_init)
        vec = lambda a: a.reshape(1, RWKV_WIDTH)
        prep = _rwkv_prep(rw, shift_init[l], P['mu'][l].reshape(1, RW_COLS), vec(P['w0'][l]), W['w2p'][l],
                          vec(P['a0'][l]), W['a2p'][l], W['g2'][l], vec(P['k_k'][l]), vec(P['k_a'][l]),
                          vec(P['r_k'][l]), eseg, tb, tt)
        y_r, wkv_stacked = _rwkv_scan(prep, wkv_init, l, wkv_stacked, vec(P['lnx_w'][l]), vec(P['lnx_b'][l]),
                                      eseg, scan_chunk, scan_nb, scan_nsub)
        routed = (l % 2 == 1)
        j = l // 2
        router = (W['router_hi'][j], W['router_lo'][j]) if routed else None
        res = _out_proj(o, y_r, x, g1, sc2, sh2, P['g_norm2'][l].reshape(1, D_MODEL), W['wo_a'][l], W['wo_r'][l],
                        router, tb, tt)
        if routed:
            x1, h2, comb = res
            x = _glu(h2, x1, g2, comb, W['w_moe_gate'][j], W['w_moe_up'][j], W['w_moe_down'][j], tb, tt)
        else:
            x1, h2 = res
            x = _glu(h2, x1, g2, None, W['w_ff_gate'][j], W['w_ff_up'][j], W['w_ff_down'][j], tb, tt)
        shifts.append(rw[:, t - 1:, :])
    if prompt:
        kt, vr = kv_stacked
        k_out = jnp.transpose(kt.reshape(depth, nb, N_HEADS_A, 2, HD_QK, t), (0, 1, 5, 2, 3, 4))
        v_out = vr.reshape(depth, nb, t, N_HEADS_A, HD_V)
    else:
        k_out, v_out = jnp.stack(ks), jnp.stack(vs)
    return x, k_out, v_out, wkv_stacked, jnp.stack(shifts)


def kernel(x_prompt, x_sample, cache_k, cache_v, state_wkv, state_shift, c_prompt, c_sample, w_mod, b_mod, g_norm1, g_norm2, w_in, q_gain, k_gain, lambda_q1, lambda_k1, lambda_q2, lambda_k2, subln, mu, w0, w2, a0, a2, g2, k_k, k_a, r_k, lnx_w, lnx_b, w_out, w_ff_gate, w_ff_up, w_ff_down, w_router, w_moe_gate, w_moe_up, w_moe_down):
    P = dict(w_mod=w_mod, b_mod=b_mod, g_norm1=g_norm1, g_norm2=g_norm2, w_in=w_in, q_gain=q_gain,
             k_gain=k_gain, lambda_q1=lambda_q1, lambda_k1=lambda_k1, lambda_q2=lambda_q2,
             lambda_k2=lambda_k2, subln=subln, mu=mu, w0=w0, w2=w2, a0=a0, a2=a2, g2=g2, k_k=k_k,
             k_a=k_a, r_k=r_k, lnx_w=lnx_w, lnx_b=lnx_b, w_out=w_out, w_ff_gate=w_ff_gate,
             w_ff_up=w_ff_up, w_ff_down=w_ff_down, w_router=w_router, w_moe_gate=w_moe_gate,
             w_moe_up=w_moe_up, w_moe_down=w_moe_down)
    W = _prepare_weights(P)
    depth = w_in.shape[0]
    bp, tp, _ = x_prompt.shape
    bs, ts, _ = x_sample.shape
    n_past = cache_k.shape[2]
    wkv0 = jnp.zeros((depth, bp, N_HEADS_R, HEAD_R, HEAD_R), F32)
    shift0 = jnp.zeros((depth, bp, 1, RW_COLS), F32)
    y_p, k_p, v_p, wkv_p, shift_p = _run_group(x_prompt, c_prompt, None, None, wkv0, shift0, P, W,
                                               tb=1, tt=512, scan_chunk=CHUNK, scan_nb=bp, scan_nsub=2, bq=512)
    ck = jnp.transpose(cache_k, (0, 1, 3, 4, 5, 2)).reshape(depth, bs, ATT_WIDTH, n_past)
    cv = cache_v.reshape(depth, bs, n_past * N_HEADS_A, HD_V)
    y_s, k_s, v_s, wkv_s, shift_s = _run_group(x_sample, c_sample, ck, cv, state_wkv, state_shift, P, W,
                                               tb=bs, tt=ts, scan_chunk=ts, scan_nb=4, scan_nsub=1, bq=None)
    return (y_p, y_s, k_p, v_p, wkv_p, shift_p, k_s, v_s, wkv_s, shift_s)
```

```python
import functools
import math

import jax
import jax.numpy as jnp
from jax import lax
from jax.experimental import pallas as pl
from jax.experimental.pallas import tpu as pltpu

F32 = jnp.float32
BF16 = jnp.bfloat16

D_MODEL = 1024
CHUNK = 64
N_HEADS_A = 4
HD_V = 128
HD_QK = 64
ATT_WIDTH = 512
RWKV_WIDTH = 512
HEAD_R = 64
N_HEADS_R = 8
LORA_W = 64
LORA_A = 64
LORA_G = 128
RW_COLS = 3 * RWKV_WIDTH + LORA_W + LORA_A + LORA_G
D_FF = 2816
N_EXPERTS = 8
D_FF_E = 1408
NORM_EPS = 1e-6
LNX_EPS = 64e-5
NEG_BIG = -1e30
LANES = 128
VMEM_LIMIT = 56 * 1024 * 1024
SCAN_GH = 2

def _cparams(sem):
    return pltpu.CompilerParams(dimension_semantics=sem, vmem_limit_bytes=VMEM_LIMIT)


def _dot(a, b):
    return jnp.dot(a, b, preferred_element_type=F32)


def _dot_nt(a, b):
    return lax.dot_general(a, b, (((1,), (1,)), ((), ())), preferred_element_type=F32)


def _dot_tn(a, b):
    return lax.dot_general(a, b, (((0,), (0,)), ((), ())), preferred_element_type=F32)


def _split2(x):
    hi = x.astype(BF16)
    lo = (x - hi.astype(F32)).astype(BF16)
    return hi, lo


def _seg_sum(x, e):
    hi, lo = _split2(x)
    return _dot(hi, e) + _dot(lo, e)


def _sigmoid(x):
    return 1.0 / (1.0 + jnp.exp(-x))


def _mod_body(c_ref, w_ref, b_ref, o_ref):
    c = c_ref[...]
    cs = c * _sigmoid(c)
    o_ref[...] = _dot(cs.astype(BF16), w_ref[...]) + b_ref[...]


def _mod(c, w_mod, b_mod):
    nb = c.shape[0]
    n = w_mod.shape[1]
    tn = 1536
    return pl.pallas_call(
        _mod_body,
        grid=(n // tn,),
        in_specs=[pl.BlockSpec((nb, D_MODEL), lambda j: (0, 0)),
                  pl.BlockSpec((D_MODEL, tn), lambda j: (0, j)),
                  pl.BlockSpec((1, tn), lambda j: (0, j))],
        out_specs=pl.BlockSpec((nb, tn), lambda j: (0, j)),
        out_shape=jax.ShapeDtypeStruct((nb, n), F32),
        compiler_params=_cparams(("arbitrary",)),
        name="mod",
    )(c, w_mod, b_mod.reshape(1, n))


def _in_body(*refs, prompt, n_alias):
    x_ref, sc_ref, sh_ref, gn_ref, wq_ref, wk_ref, wv_ref, wr_ref, qg_ref, kg_ref, e_ref = refs[:11]
    outs = refs[11 + n_alias:]
    tb, tt, _ = x_ref.shape
    x = x_ref[...]
    ms = jnp.mean(x * x, axis=-1, keepdims=True)
    h = x * lax.rsqrt(ms + NORM_EPS) * gn_ref[...]
    h = h * (1.0 + sc_ref[...]) + sh_ref[...]
    hb = h.reshape(tb * tt, D_MODEL).astype(BF16)
    e = e_ref[...]

    def group_norm(c, gain):
        msq = _seg_sum(c * c, e) * (1.0 / HD_QK)
        return c * lax.rsqrt(msq + NORM_EPS) * gain

    q = group_norm(_dot(hb, wq_ref[...]), qg_ref[...])
    k = group_norm(_dot(hb, wk_ref[...]), kg_ref[...])
    v = _dot(hb, wv_ref[...])
    rw = _dot(hb, wr_ref[...]).reshape(tb, tt, RW_COLS)
    if prompt:
        q_ref, rw_ref, kb_ref, vt_ref, kt_ref, vr_ref = outs
        kb_ref[...] = k.reshape(tb, tt, ATT_WIDTH).astype(BF16)
        vt_ref[0] = v.T.astype(BF16)
        kt_ref[0] = k.T
        for hd in range(N_HEADS_A):
            vr_ref[0, pl.ds(hd, tt, stride=N_HEADS_A), :] = v[:, hd * HD_V:(hd + 1) * HD_V]
    else:
        q_ref, rw_ref, k_ref, v_ref = outs
        k_ref[...] = k.reshape(tb, tt, ATT_WIDTH)
        v_ref[...] = v.reshape(tb, tt, ATT_WIDTH)
    q_ref[...] = q.reshape(tb, tt, ATT_WIDTH).astype(BF16)
    rw_ref[...] = rw


def _in_proj(x, sc, sh, gn, wq, wk, wv, wr, qg, kg, eseg, tb, tt, layer, depth, stacked):
    nb, t, _ = x.shape
    prompt = stacked is not None
    row = lambda w: pl.BlockSpec((tb, tt, w), lambda b, i: (b, i, 0))
    modspec = pl.BlockSpec((tb, 1, D_MODEL), lambda b, i: (b, 0, 0))
    full = lambda a: pl.BlockSpec(a.shape, lambda b, i: (0,) * a.ndim)
    ins = [x, sc, sh, gn, wq, wk, wv, wr, qg, kg, eseg]
    in_specs = [row(D_MODEL), modspec, modspec, full(gn), full(wq), full(wk), full(wv), full(wr),
                full(qg), full(kg), full(eseg)]
    out_specs = [row(ATT_WIDTH), row(RW_COLS)]
    out_shape = [jax.ShapeDtypeStruct((nb, t, ATT_WIDTH), BF16), jax.ShapeDtypeStruct((nb, t, RW_COLS), F32)]
    aliases = {}
    if prompt:
        assert tb == 1
        out_specs += [row(ATT_WIDTH), pl.BlockSpec((1, ATT_WIDTH, tt), lambda b, i: (b, 0, i)),
                      pl.BlockSpec((None, 1, ATT_WIDTH, tt), lambda b, i: (layer, b, 0, i)),
                      pl.BlockSpec((None, 1, N_HEADS_A * tt, HD_V), lambda b, i: (layer, b, i, 0))]
        out_shape += [jax.ShapeDtypeStruct((nb, t, ATT_WIDTH), BF16),
                      jax.ShapeDtypeStruct((nb, ATT_WIDTH, t), BF16),
                      jax.ShapeDtypeStruct((depth, nb, ATT_WIDTH, t), F32),
                      jax.ShapeDtypeStruct((depth, nb, N_HEADS_A * t, HD_V), F32)]
        for n, prev in enumerate(stacked):
            aliases[len(ins)] = 4 + n
            ins.append(prev)
            in_specs.append(pl.BlockSpec(memory_space=pl.ANY))
    else:
        out_specs += [row(ATT_WIDTH), row(ATT_WIDTH)]
        out_shape += [jax.ShapeDtypeStruct((nb, t, ATT_WIDTH), F32)] * 2
    return pl.pallas_call(
        functools.partial(_in_body, prompt=prompt, n_alias=len(aliases)),
        grid=(nb // tb, t // tt),
        in_specs=in_specs,
        out_specs=out_specs,
        out_shape=out_shape,
        input_output_aliases=aliases,
        compiler_params=_cparams(("arbitrary", "arbitrary")),
        name="in_proj",
    )(*ins)


def _alibi_slope(h):
    return 2.0 ** (-8.0 * (h + 1) / N_HEADS_A)


def _lambda_value(lam_ref, lam_init):
    lv = lam_ref[...]
    s1 = jnp.sum(lv[0:1] * lv[1:2], axis=-1, keepdims=True)
    s2 = jnp.sum(lv[2:3] * lv[3:4], axis=-1, keepdims=True)
    return jnp.exp(s1) - jnp.exp(s2) + lam_init


def _stack_maps(qh):
    lane = lax.broadcasted_iota(jnp.int32, qh.shape, 1)
    qs = qh * (HD_QK ** -0.5)
    zero = jnp.zeros_like(qs)
    return jnp.concatenate([jnp.where(lane < HD_QK, qs, zero), jnp.where(lane >= HD_QK, qs, zero)],
                           axis=0)


def _sub_norm(o, gain, lam_init):
    ms = jnp.mean(o * o, axis=-1, keepdims=True)
    return o * lax.rsqrt(ms + NORM_EPS) * gain * (1.0 - lam_init)


def _attn_body(it_ref, jt_ref, q_ref, k_ref, vt_ref, lam_ref, sub_ref, o_ref, q2_ref, m_ref, l_ref,
               acc_ref, s_ref, p_ref, *, bq, lam_init):
    step_id = pl.program_id(1)
    i = it_ref[step_id]
    j = jt_ref[step_id]
    bk = bq
    nq2 = 2 * bq
    kc = LANES
    lane = lax.broadcasted_iota(jnp.int32, (bq, LANES), 1)

    @pl.when(j == 0)
    def _():
        m_ref[...] = jnp.full(m_ref.shape, NEG_BIG, F32)
        l_ref[...] = jnp.zeros(l_ref.shape, F32)
        acc_ref[...] = jnp.zeros(acc_ref.shape, F32)
        lane2 = lax.broadcasted_iota(jnp.int32, (nq2, LANES), 1)
        for h in range(N_HEADS_A):
            slope = _alibi_slope(h)
            feat = jnp.where(lane2 == 0, slope * CHUNK, jnp.where(lane2 == 1, slope, 0.0)).astype(BF16)
            q2_ref[h] = jnp.concatenate([_stack_maps(q_ref[0, :, h * HD_V:(h + 1) * HD_V]), feat], axis=1)

    def step(diag):
        krel = lax.broadcasted_iota(jnp.int32, (bk, LANES), 0) + (j - i) * bq
        kfeat = jnp.where(lane == 0, krel >> 6, jnp.where(lane == 1, krel & (CHUNK - 1), 0))
        kfeat = kfeat.astype(F32).astype(BF16)
        subs = range(bk // kc)
        npair = nq2 // (2 * LANES)

        def qk(h, g):
            kh = jnp.concatenate([k_ref[0, :, h * HD_V:(h + 1) * HD_V], kfeat], axis=1)
            st2 = _dot_nt(kh, q2_ref[h, g * 2 * LANES:(g + 1) * 2 * LANES, :])
            if diag:
                c = lax.broadcasted_iota(jnp.int32, (bk, 2 * LANES), 0)
                r = lax.broadcasted_iota(jnp.int32, (bk, 2 * LANES), 1) + (g * 2 * LANES) % bq
                ahead = jnp.maximum(c - r, 0).astype(F32) * (-2.0 * _alibi_slope(h))
                st2 = jnp.where((c >> 6) <= (r >> 6), st2 + ahead, NEG_BIG)
            s_ref[h % 2, 2 * g] = st2[:, :LANES]
            s_ref[h % 2, 2 * g + 1] = st2[:, LANES:]
            return jnp.max(st2, axis=0, keepdims=True)

        def softmax(h, g, m_blk):
            alpha = []
            for n, st in enumerate((2 * g, 2 * g + 1)):
                sl = slice(st * LANES, (st + 1) * LANES)
                m_old = m_ref[h, :, sl]
                m_new = jnp.maximum(m_old, m_blk[:, n * LANES:(n + 1) * LANES])
                a = jnp.exp(m_old - m_new)
                psum = None
                for kb in subs:
                    p = jnp.exp(s_ref[h % 2, st, kb * kc:(kb + 1) * kc, :] - m_new)
                    psum = p if psum is None else psum + p
                    p_ref[st, kb * kc:(kb + 1) * kc, :] = p.astype(BF16)
                l_ref[h, :, sl] = a * l_ref[h, :, sl] + jnp.sum(psum, axis=0, keepdims=True)
                m_ref[h, :, sl] = m_new
                alpha.append(a)
            return jnp.concatenate(alpha, axis=1)

        def pv(h, g, a_row):
            p2 = jnp.concatenate([p_ref[2 * g], p_ref[2 * g + 1]], axis=1)
            sl = slice(g * 2 * LANES, (g + 1) * 2 * LANES)
            acc_ref[h, :, sl] = acc_ref[h, :, sl] * a_row + _dot(vt_ref[0, h * HD_V:(h + 1) * HD_V, :], p2)

        m_blk = {(0, g): qk(0, g) for g in range(npair)}
        for h in range(N_HEADS_A):
            for g in range(npair):
                if h + 1 < N_HEADS_A:
                    m_blk[(h + 1, g)] = qk(h + 1, g)
                a_row = softmax(h, g, m_blk.pop((h, g)))
                pv(h, g, a_row)

    @pl.when(j < i)
    def _():
        step(False)

    @pl.when(j == i)
    def _():
        step(True)
        lam = _lambda_value(lam_ref, lam_init)
        for h in range(N_HEADS_A):
            o1 = acc_ref[h, :, :bq] / l_ref[h, :, :bq]
            o2 = acc_ref[h, :, bq:] / l_ref[h, :, bq:]
            o = (o1 - lam * o2).T
            o_ref[0, :, h * HD_V:(h + 1) * HD_V] = _sub_norm(o, sub_ref[...], lam_init)


def _attn_prompt(q, kb, vt, lam_vecs, subln, lam_init, bq):
    nb, t, _ = q.shape
    nq = t // bq
    pairs = [(i, j) for i in range(nq) for j in range(i + 1)]
    i_tab = jnp.asarray([p[0] for p in pairs], jnp.int32)
    j_tab = jnp.asarray([p[1] for p in pairs], jnp.int32)
    qspec = pl.BlockSpec((1, bq, ATT_WIDTH), lambda b, s, it, jt: (b, it[s], 0))
    kspec = pl.BlockSpec((1, bq, ATT_WIDTH), lambda b, s, it, jt: (b, jt[s], 0))
    vspec = pl.BlockSpec((1, ATT_WIDTH, bq), lambda b, s, it, jt: (b, 0, jt[s]))
    grid_spec = pltpu.PrefetchScalarGridSpec(
        num_scalar_prefetch=2,
        grid=(nb, len(pairs)),
        in_specs=[qspec, kspec, vspec,
                  pl.BlockSpec(lam_vecs.shape, lambda b, s, it, jt: (0, 0)),
                  pl.BlockSpec(subln.shape, lambda b, s, it, jt: (0, 0))],
        out_specs=qspec,
        scratch_shapes=[pltpu.VMEM((N_HEADS_A, 2 * bq, 2 * HD_V), BF16),
                        pltpu.VMEM((N_HEADS_A, 1, 2 * bq), F32),
                        pltpu.VMEM((N_HEADS_A, 1, 2 * bq), F32),
                        pltpu.VMEM((N_HEADS_A, HD_V, 2 * bq), F32),
                        pltpu.VMEM((2, 2 * bq // LANES, bq, LANES), F32),
                        pltpu.VMEM((2 * bq // LANES, bq, LANES), BF16)])
    return pl.pallas_call(
        functools.partial(_attn_body, bq=bq, lam_init=lam_init),
        grid_spec=grid_spec,
        out_shape=jax.ShapeDtypeStruct((nb, t, ATT_WIDTH), F32),
        compiler_params=_cparams(("arbitrary", "arbitrary")),
        name="attn_prompt",
    )(i_tab, j_tab, q, kb, vt, lam_vecs, subln)


def _attn_dec_body(q_ref, ck_ref, cv_ref, kn_ref, vn_ref, lam_ref, sub_ref, o_ref, *, lam_init):
    tq = q_ref.shape[1]
    n_past = ck_ref.shape[1]
    lam = _lambda_value(lam_ref, lam_init)
    r = lax.broadcasted_iota(jnp.int32, (tq, n_past), 0)
    c = lax.broadcasted_iota(jnp.int32, (tq, n_past), 1)
    dist_c = jnp.abs((r + n_past - c).astype(F32))
    rn = lax.broadcasted_iota(jnp.int32, (tq, tq), 0)
    cn = lax.broadcasted_iota(jnp.int32, (tq, tq), 1)
    dist_n = jnp.abs((rn - cn).astype(F32))
    for h in range(N_HEADS_A):
        sl = slice(h * HD_V, (h + 1) * HD_V)
        q2 = _stack_maps(q_ref[0, :, sl])
        vc = cv_ref[pl.ds(h, n_past, stride=N_HEADS_A), :].astype(BF16)
        kn = kn_ref[0, :, sl].astype(BF16)
        vn = vn_ref[0, :, sl].astype(BF16)
        sn_all = _dot_nt(q2, kn)
        slope = -_alibi_slope(h)
        outs = []
        sc_all = _dot(q2, ck_ref[sl, :].astype(BF16))
        for mp in range(2):
            s_c = sc_all[mp * tq:(mp + 1) * tq] + dist_c * slope
            s_n = sn_all[mp * tq:(mp + 1) * tq] + dist_n * slope
            m = jnp.maximum(jnp.max(s_c, axis=-1, keepdims=True), jnp.max(s_n, axis=-1, keepdims=True))
            p_c = jnp.exp(s_c - m)
            p_n = jnp.exp(s_n - m)
            l = jnp.sum(p_c, axis=-1, keepdims=True) + jnp.sum(p_n, axis=-1, keepdims=True)
            acc = _dot(p_c.astype(BF16), vc) + _dot(p_n.astype(BF16), vn)
            outs.append(acc / l)
        o = outs[0] - lam * outs[1]
        o_ref[0, :, sl] = _sub_norm(o, sub_ref[...], lam_init)


def _attn_sample(q, cache_k, cache_v, layer, k_new, v_new, lam_vecs, subln, lam_init):
    nb, tq, _ = q.shape
    n_past = cache_k.shape[3]
    row = pl.BlockSpec((1, tq, ATT_WIDTH), lambda b: (b, 0, 0))
    kspec = pl.BlockSpec((None, None, ATT_WIDTH, n_past), lambda b: (layer, b, 0, 0))
    vspec = pl.BlockSpec((None, None, n_past * N_HEADS_A, HD_V), lambda b: (layer, b, 0, 0))
    return pl.pallas_call(
        functools.partial(_attn_dec_body, lam_init=lam_init),
        grid=(nb,),
        in_specs=[row, kspec, vspec, row, row,
                  pl.BlockSpec(lam_vecs.shape, lambda b: (0, 0)),
                  pl.BlockSpec(subln.shape, lambda b: (0, 0))],
        out_specs=row,
        out_shape=jax.ShapeDtypeStruct((nb, tq, ATT_WIDTH), F32),
        compiler_params=_cparams(("arbitrary",)),
        name="attn_sample",
    )(q, cache_k, cache_v, k_new, v_new, lam_vecs, subln)


def _prep_body(rw_ref, shift_ref, mu_ref, w0_ref, w2_ref, a0_ref, a2_ref, g2_ref, kk_ref, ka_ref,
               rk_ref, e_ref, r_o, k_o, v_o, kk_o, bh_o, lw_o, g_o, bon_o, carry_ref):
    tb, tt, w = rw_ref.shape
    i = pl.program_id(1)
    rows = tb * tt
    cols = rw_ref[...].reshape(rows, w)
    rolled = pltpu.roll(cols, 1, 0)
    rowi = lax.broadcasted_iota(jnp.int32, (rows, 1), 0)
    if tb == 1:
        @pl.when(i == 0)
        def _():
            carry_ref[...] = shift_ref[0]

        prev = jnp.where(rowi == 0, carry_ref[...], rolled)
        carry_ref[...] = cols[tt - 1:tt, :]
    else:
        sh = jnp.broadcast_to(shift_ref[...], (tb, tt, w)).reshape(rows, w)
        prev = jnp.where((rowi % tt) == 0, sh, rolled)
    xs = cols + (prev - cols) * mu_ref[...]
    o1, o2, o3 = RWKV_WIDTH, 2 * RWKV_WIDTH, 3 * RWKV_WIDTH
    r = xs[:, :o1]
    k = xs[:, o1:o2]
    v = xs[:, o2:o3]
    da = xs[:, o3:o3 + LORA_W + LORA_A]
    gd = xs[:, o3 + LORA_W + LORA_A:]
    e = e_ref[...]
    z = w0_ref[...] + _dot(jnp.tanh(da).astype(BF16), w2_ref[...])
    lw_o[...] = (-math.exp(-0.5) * _sigmoid(z)).reshape(tb, tt, o1)
    a = _sigmoid(a0_ref[...] + _dot(da.astype(BF16), a2_ref[...]))
    g_o[...] = _dot(_sigmoid(gd).astype(BF16), g2_ref[...]).reshape(tb, tt, o1)
    kk = k * kk_ref[...]
    nrm = jnp.sqrt(_seg_sum(kk * kk, e))
    kk = kk / jnp.maximum(nrm, 1e-12)
    kp = k * (1.0 + (a - 1.0) * ka_ref[...])
    bon = _seg_sum(r * kp * rk_ref[...], e) * v
    r_o[...] = r.reshape(tb, tt, o1)
    k_o[...] = kp.reshape(tb, tt, o1)
    v_o[...] = v.reshape(tb, tt, o1)
    kk_o[...] = kk.reshape(tb, tt, o1)
    bh_o[...] = (kk * a).reshape(tb, tt, o1)
    bon_o[...] = bon.reshape(tb, tt, o1)


def _rwkv_prep(rw, shift, mu, w0, w2p, a0, a2p, g2, k_k, k_a, r_k, eseg, tb, tt):
    nb, t, _ = rw.shape
    row = lambda w: pl.BlockSpec((tb, tt, w), lambda b, i: (b, i, 0))
    full = lambda a: pl.BlockSpec(a.shape, lambda b, i: (0,) * a.ndim)
    out = jax.ShapeDtypeStruct((nb, t, RWKV_WIDTH), F32)
    return pl.pallas_call(
        _prep_body,
        grid=(nb // tb, t // tt),
        in_specs=[row(RW_COLS), pl.BlockSpec((tb, 1, RW_COLS), lambda b, i: (b, 0, 0)), full(mu), full(w0),
                  full(w2p), full(a0), full(a2p), full(g2), full(k_k), full(k_a), full(r_k), full(eseg)],
        out_specs=[row(RWKV_WIDTH)] * 8,
        out_shape=[out] * 8,
        scratch_shapes=[pltpu.VMEM((1, RW_COLS), F32)],
        compiler_params=_cparams(("arbitrary", "arbitrary")),
        name="rwkv_prep",
    )(rw, shift, mu, w0, w2p, a0, a2p, g2, k_k, k_a, r_k, eseg)


def _scan_body(*refs, C, nsub):
    r_ref, k_ref, v_ref, kk_ref, bh_ref, lw_ref, g_ref, bon_ref, s0_ref, lnw_ref, lnb_ref, e_ref = refs[:12]
    y_ref, sout_ref, st_ref = refs[-3:]
    ci = pl.program_id(1)
    nc = pl.num_programs(1)

    nbk = lw_ref.shape[0]

    @pl.when(ci == 0)
    def _():
        z = jnp.zeros((HEAD_R, HEAD_R), F32)
        for b in range(nbk):
            for q in range(N_HEADS_R // SCAN_GH):
                blocks = [jnp.concatenate([s0_ref[b, SCAN_GH * q + hh] if hc == hh else z
                                           for hc in range(SCAN_GH)], axis=1) for hh in range(SCAN_GH)]
                st_ref[b, q] = jnp.concatenate(blocks, axis=0)

    ti = lax.broadcasted_iota(jnp.int32, (C, C), 0)
    si = lax.broadcasted_iota(jnp.int32, (C, C), 1)
    ltri = (si <= ti).astype(BF16)
    segs = [(b, u) for u in range(nsub) for b in range(nbk)]
    rows_of = lambda u: slice(u * C, (u + 1) * C)
    at, rt, bt, kt, bb, kb, g_c, vb = [], [], [], [], [], [], [], []
    for (b, u) in segs:
        rw = rows_of(u)
        lw = lw_ref[b, rw, :]
        hi = lw.astype(BF16)
        r1 = lw - hi.astype(F32)
        mid = r1.astype(BF16)
        lo = (r1 - mid.astype(F32)).astype(BF16)
        cs = _dot(ltri, hi) + _dot(ltri, mid) + _dot(ltri, lo)
        cend = cs[C - 1:C, :]
        kk = kk_ref[b, rw, :]
        bh = bh_ref[b, rw, :]
        kp = k_ref[b, rw, :]
        e_neg = jnp.exp(-cs)
        e_end = jnp.exp(cend - cs)
        at.append((-kk * jnp.exp(cs - lw)).astype(BF16))
        rt.append((r_ref[b, rw, :] * jnp.exp(cs)).astype(BF16))
        bt.append((bh * e_neg).astype(BF16))
        kt.append((kp * e_neg).astype(BF16))
        bb.append((bh * e_end).astype(BF16))
        kb.append((kp * e_end).astype(BF16))
        g_c.append(jnp.exp(cend))
        vb.append(v_ref[b, rw, :].astype(BF16))

    C2 = SCAN_GH * C
    gw = SCAN_GH * HEAD_R
    log_c = C.bit_length() - 1
    log_h = HEAD_R.bit_length() - 1
    hm = ((lax.broadcasted_iota(jnp.int32, (C2, gw), 0) >> log_c)
          == (lax.broadcasted_iota(jnp.int32, (C2, gw), 1) >> log_h))
    hm2 = ((lax.broadcasted_iota(jnp.int32, (C2, C2), 0) >> log_c)
           == (lax.broadcasted_iota(jnp.int32, (C2, C2), 1) >> log_c))
    hm_state = ((lax.broadcasted_iota(jnp.int32, (gw, gw), 0) >> log_h)
                == (lax.broadcasted_iota(jnp.int32, (gw, gw), 1) >> log_h))
    tt = lax.broadcasted_iota(jnp.int32, (C, C2), 0)
    ss = lax.broadcasted_iota(jnp.int32, (C, C2), 1) & (C - 1)
    strict = ss < tt
    incl = ss <= tt
    eye = (ss == tt).astype(F32)

    def bd(x, mask):
        xx = jnp.concatenate([x] * SCAN_GH, axis=0)
        return jnp.where(mask, xx, jnp.zeros_like(xx)).astype(BF16)

    def mm(x, y):
        return _dot(x.astype(BF16), bd(y, hm2))

    npair = N_HEADS_R // SCAN_GH
    chains = [(sg, q) for sg in range(len(segs)) for q in range(npair)]
    pairs = range(len(chains))
    sls = [slice(q * gw, (q + 1) * gw) for (_, q) in chains]
    zero = jnp.zeros((C, C2), F32)
    a_l = [at[sg][:, sls[p]] for p, (sg, _) in enumerate(chains)]
    r_l = [rt[sg][:, sls[p]] for p, (sg, _) in enumerate(chains)]
    v_l = [vb[sg][:, sls[p]] for p, (sg, _) in enumerate(chains)]
    ar_l = [jnp.concatenate([a_l[p], r_l[p]], axis=0) for p in pairs]
    gb_l = [_dot_nt(ar_l[p], bd(bt[chains[p][0]][:, sls[p]], hm)) for p in pairs]
    gk_l = [_dot_nt(ar_l[p], bd(kt[chains[p][0]][:, sls[p]], hm)) for p in pairs]
    n_l = [jnp.where(strict, gb_l[p][:C], zero) for p in pairs]
    aks_l = [jnp.where(strict, gk_l[p][:C], zero).astype(BF16) for p in pairs]
    rbi_l = [jnp.where(incl, gb_l[p][C:], zero).astype(BF16) for p in pairs]
    rki_l = [jnp.where(incl, gk_l[p][C:], zero).astype(BF16) for p in pairs]
    nd_l = [jnp.where((ss >> 3) == (tt >> 3), n_l[p], zero) for p in pairs]
    nd2_l = [mm(nd_l[p], nd_l[p]) for p in pairs]
    nd4_l = [mm(nd2_l[p], nd2_l[p]) for p in pairs]
    t_l = [mm(eye + nd_l[p], eye + nd2_l[p]) for p in pairs]
    t_l = [mm(t_l[p], eye + nd4_l[p]) for p in pairs]
    lvl = 3
    while (1 << lvl) < C:
        tb_ = tt >> lvl
        ml = ((tb_ & 1) == 1) & ((ss >> lvl) == tb_ - 1)
        tn_l = [mm(t_l[p], jnp.where(ml, n_l[p], zero)) for p in pairs]
        t_l = [t_l[p] + mm(tn_l[p], t_l[p]) for p in pairs]
        lvl += 1
    vbd_l = [bd(v_l[p], hm) for p in pairs]
    t_l = [t.astype(BF16) for t in t_l]
    ys = [None] * len(chains)
    for u in range(nsub):
        cur = [p for p in pairs if segs[chains[p][0]][1] == u]
        s_l = {p: st_ref[segs[chains[p][0]][0], chains[p][1]] for p in cur}
        sb_l = {p: s_l[p].astype(BF16) for p in cur}
        x_l = {p: _dot_nt(a_l[p], sb_l[p]) + _dot(aks_l[p], vbd_l[p]) for p in cur}
        u_l = {p: _dot(t_l[p], bd(x_l[p], hm)) for p in cur}
        for p in cur:
            ys[p] = _dot_nt(r_l[p], sb_l[p]) + _dot(rbi_l[p], bd(u_l[p], hm)) + _dot(rki_l[p], vbd_l[p])
        for p in cur:
            sg, q = chains[p]
            uv = jnp.concatenate([u_l[p].astype(BF16), v_l[p]], axis=0)
            bk = jnp.concatenate([bb[sg][:, sls[p]], kb[sg][:, sls[p]]], axis=0)
            s_add = _dot_tn(uv, bk)
            st_ref[segs[sg][0], q] = (s_l[p] * g_c[sg][:, sls[p]]
                                      + jnp.where(hm_state, s_add, jnp.zeros_like(s_add)))

    e = e_ref[...]
    ycat = [jnp.concatenate(ys[sg * npair:(sg + 1) * npair], axis=1) for sg in range(len(segs))]
    mean = [_seg_sum(y, e) * (1.0 / HEAD_R) for y in ycat]
    d = [ycat[sg] - mean[sg] for sg in range(len(segs))]
    var = [_seg_sum(x * x, e) * (1.0 / HEAD_R) for x in d]
    for sg, (b, u) in enumerate(segs):
        yn = d[sg] * lax.rsqrt(var[sg] + LNX_EPS) * lnw_ref[...] + lnb_ref[...]
        y_ref[b, rows_of(u), :] = (yn + bon_ref[b, rows_of(u), :]) * g_ref[b, rows_of(u), :]

    @pl.when(ci == nc - 1)
    def _():
        for b in range(nbk):
            for q in range(N_HEADS_R // SCAN_GH):
                s = st_ref[b, q]
                for hh in range(SCAN_GH):
                    blk = slice(hh * HEAD_R, (hh + 1) * HEAD_R)
                    sout_ref[b, SCAN_GH * q + hh] = s[blk, blk]


def _rwkv_scan(prep, wkv_init, layer, wkv_prev, lnx_w, lnx_b, eseg, C, nbk, nsub):
    nb, t, _ = prep[0].shape
    row = pl.BlockSpec((nbk, C * nsub, RWKV_WIDTH), lambda b, c: (b, c, 0))
    sspec = pl.BlockSpec((None, nbk, N_HEADS_R, HEAD_R, HEAD_R), lambda b, c: (layer, b, 0, 0, 0))
    full = lambda a: pl.BlockSpec(a.shape, lambda b, c: (0,) * a.ndim)
    ins = list(prep) + [wkv_init, lnx_w, lnx_b, eseg]
    in_specs = [row] * 8 + [sspec, full(lnx_w), full(lnx_b), full(eseg)]
    aliases = {}
    if wkv_prev is not None:
        aliases[len(ins)] = 1
        ins.append(wkv_prev)
        in_specs.append(pl.BlockSpec(memory_space=pl.ANY))
    return pl.pallas_call(
        functools.partial(_scan_body, C=C, nsub=nsub),
        grid=(nb // nbk, t // (C * nsub)),
        in_specs=in_specs,
        out_specs=[row, sspec],
        out_shape=[jax.ShapeDtypeStruct((nb, t, RWKV_WIDTH), F32),
                   jax.ShapeDtypeStruct(wkv_init.shape, F32)],
        scratch_shapes=[pltpu.VMEM((nbk, N_HEADS_R // SCAN_GH, SCAN_GH * HEAD_R, SCAN_GH * HEAD_R), F32)],
        input_output_aliases=aliases,
        compiler_params=_cparams(("arbitrary", "arbitrary")),
        name="rwkv_scan",
    )(*ins)


def _out_body(*refs, routed):
    if routed:
        (o_ref, y_ref, x_ref, g1_ref, sc_ref, sh_ref, gn_ref, wa_ref, wr_ref, rh_ref, rl_ref,
         x1_ref, h2_ref, comb_ref) = refs
    else:
        (o_ref, y_ref, x_ref, g1_ref, sc_ref, sh_ref, gn_ref, wa_ref, wr_ref, x1_ref, h2_ref) = refs
    tb, tt, _ = x_ref.shape
    rows = tb * tt
    ob = o_ref[...].reshape(rows, ATT_WIDTH).astype(BF16)
    yb = y_ref[...].reshape(rows, RWKV_WIDTH).astype(BF16)
    mix = _dot(ob, wa_ref[...]) + _dot(yb, wr_ref[...])
    x1 = x_ref[...] + g1_ref[...] * mix.reshape(tb, tt, D_MODEL)
    x1_ref[...] = x1
    ms = jnp.mean(x1 * x1, axis=-1, keepdims=True)
    h2 = x1 * lax.rsqrt(ms + NORM_EPS) * gn_ref[...]
    h2 = h2 * (1.0 + sc_ref[...]) + sh_ref[...]
    h2_ref[...] = h2.astype(BF16)
    if routed:
        hf = h2.reshape(rows, D_MODEL)
        hi, lo = _split2(hf)
        logits = _dot(hi, rh_ref[...]) + _dot(hi, rl_ref[...]) + _dot(lo, rh_ref[...])
        lane = lax.broadcasted_iota(jnp.int32, logits.shape, 1)
        logits = jnp.where(lane < N_EXPERTS, logits, NEG_BIG)
        m1 = jnp.max(logits, axis=-1, keepdims=True)
        i1 = jnp.min(jnp.where(logits == m1, lane, LANES), axis=-1, keepdims=True)
        rest = jnp.where(lane == i1, NEG_BIG, logits)
        m2 = jnp.max(rest, axis=-1, keepdims=True)
        i2 = jnp.min(jnp.where(rest == m2, lane, LANES), axis=-1, keepdims=True)
        ex = jnp.exp(m2 - m1)
        gate1 = 1.0 / (1.0 + ex)
        gate2 = ex / (1.0 + ex)
        comb = jnp.where(lane == i1, gate1, 0.0) + jnp.where(lane == i2, gate2, 0.0)
        comb_ref[...] = comb.reshape(tb, tt, LANES)


def _out_proj(o, y, x, g1, sc2, sh2, gn2, wa, wr, router, tb, tt):
    nb, t, _ = x.shape
    routed = router is not None
    row = lambda w: pl.BlockSpec((tb, tt, w), lambda b, i: (b, i, 0))
    modspec = pl.BlockSpec((tb, 1, D_MODEL), lambda b, i: (b, 0, 0))
    full = lambda a: pl.BlockSpec(a.shape, lambda b, i: (0,) * a.ndim)
    ins = [o, y, x, g1, sc2, sh2, gn2, wa, wr]
    in_specs = [row(ATT_WIDTH), row(RWKV_WIDTH), row(D_MODEL), modspec, modspec, modspec, full(gn2),
                full(wa), full(wr)]
    out_specs = [row(D_MODEL), row(D_MODEL)]
    out_shape = [jax.ShapeDtypeStruct((nb, t, D_MODEL), F32), jax.ShapeDtypeStruct((nb, t, D_MODEL), BF16)]
    if routed:
        ins += list(router)
        in_specs += [full(router[0]), full(router[1])]
        out_specs.append(row(LANES))
        out_shape.append(jax.ShapeDtypeStruct((nb, t, LANES), F32))
    return pl.pallas_call(
        functools.partial(_out_body, routed=routed),
        grid=(nb // tb, t // tt),
        in_specs=in_specs,
        out_specs=out_specs,
        out_shape=out_shape,
        compiler_params=_cparams(("arbitrary", "arbitrary")),
        name="out_proj_routed" if routed else "out_proj",
    )(*ins)


def _glu_body(h_ref, x_ref, g2_ref, wg_ref, wu_ref, wd_ref, o_ref, acc_ref):
    tb, tt, _ = x_ref.shape
    rows = tb * tt
    e = pl.program_id(2)
    ne = pl.num_programs(2)

    @pl.when(e == 0)
    def _():
        acc_ref[...] = jnp.zeros(acc_ref.shape, F32)

    hb = h_ref[...].reshape(rows, D_MODEL)
    gate = _dot(hb, wg_ref[...])
    up = _dot(hb, wu_ref[...])
    act = gate * _sigmoid(gate) * up
    acc_ref[...] += _dot(act.astype(BF16), wd_ref[...])

    @pl.when(e == ne - 1)
    def _():
        o_ref[...] = x_ref[...] + g2_ref[...] * acc_ref[...].reshape(tb, tt, D_MODEL)


def _glu(h2, x1, g2, wg, wu, wd, tb, tt):
    nb, t, _ = x1.shape
    row = lambda w: pl.BlockSpec((tb, tt, w), lambda b, i, e: (b, i, 0))
    modspec = pl.BlockSpec((tb, 1, D_MODEL), lambda b, i, e: (b, 0, 0))
    wspec_in = pl.BlockSpec((D_MODEL, D_FF_E), lambda b, i, e: (0, e))
    wspec_out = pl.BlockSpec((D_FF_E, D_MODEL), lambda b, i, e: (e, 0))
    return pl.pallas_call(
        _glu_body,
        grid=(nb // tb, t // tt, D_FF // D_FF_E),
        in_specs=[row(D_MODEL), row(D_MODEL), modspec, wspec_in, wspec_in, wspec_out],
        out_specs=row(D_MODEL),
        out_shape=jax.ShapeDtypeStruct((nb, t, D_MODEL), F32),
        scratch_shapes=[pltpu.VMEM((tb * tt, D_MODEL), F32)],
        compiler_params=_cparams(("arbitrary", "arbitrary", "arbitrary")),
        name="glu_dense",
    )(h2, x1, g2, wg, wu, wd)


def _moe_body(cnt_ref, h_ref, x_ref, g2_ref, comb_ref, wg_ref, wu_ref, wd_ref,
              o_ref, acc_ref, rank_col_ref, rank_row_ref, comb_t_ref, ltri_ref, utri_ref, *, cap):
    tb, tt, _ = x_ref.shape
    rows = tb * tt
    e = pl.program_id(2)
    ne = pl.num_programs(2)
    tile = pl.program_id(0) * pl.num_programs(1) + pl.program_id(1)
    count = cnt_ref[tile * N_EXPERTS + e]
    cap_p = -(-cap // LANES) * LANES

    @pl.when((tile == 0) & (e == 0))
    def _():
        ri = lax.broadcasted_iota(jnp.int32, (rows, rows), 0)
        ci = lax.broadcasted_iota(jnp.int32, (rows, rows), 1)
        ltri_ref[...] = (ci < ri).astype(BF16)
        utri_ref[...] = (ri < ci).astype(BF16)

    @pl.when(e == 0)
    def _():
        acc_ref[...] = jnp.zeros(acc_ref.shape, F32)
        comb = comb_ref[...].reshape(rows, LANES)
        comb_t = comb.T
        comb_t_ref[...] = comb_t
        sel = (comb > 0.0).astype(BF16)
        sel_t = (comb_t > 0.0).astype(BF16)
        rank_col_ref[...] = _dot(ltri_ref[...], sel)
        rank_row_ref[...] = _dot(sel_t, utri_ref[...])

    @pl.when(count > 0)
    def _():
        comb = comb_ref[...].reshape(rows, LANES)
        lane = lax.broadcasted_iota(jnp.int32, (rows, LANES), 1)
        pick = lane == e
        w_col = jnp.sum(jnp.where(pick, comb, 0.0), axis=-1, keepdims=True)
        r_col = jnp.sum(jnp.where(pick, rank_col_ref[...], 0.0), axis=-1, keepdims=True)
        r_col = jnp.where(w_col > 0.0, r_col, -1.0)
        w_row = comb_t_ref[pl.ds(e, 1), :]
        r_row = jnp.where(w_row > 0.0, rank_row_ref[pl.ds(e, 1), :], -1.0)
        w_rep = jnp.broadcast_to(w_col, (rows, LANES))
        w_hi, w_lo = _split2(w_rep)
        hb = h_ref[...].reshape(rows, D_MODEL)
        for c in range(-(-rows // cap)):
            @pl.when(count > c * cap)
            def _():
                slot_r = (lax.broadcasted_iota(jnp.int32, (cap, rows), 0) + c * cap).astype(F32)
                p_mat = (r_row == slot_r).astype(BF16)
                xc = _dot(p_mat, hb).astype(BF16)
                wc = (_dot(p_mat, w_hi) + _dot(p_mat, w_lo))[:, :1]
                gate = _dot(xc, wg_ref[...])
                up = _dot(xc, wu_ref[...])
                act = gate * _sigmoid(gate) * up * wc
                yc = _dot(act.astype(BF16), wd_ref[...])
                if cap_p > cap:
                    yc = jnp.concatenate([yc, jnp.zeros((cap_p - cap, D_MODEL), F32)], axis=0)
                y_hi, y_lo = _split2(yc)
                slot_c = (lax.broadcasted_iota(jnp.int32, (rows, cap_p), 1) + c * cap).astype(F32)
                pt_mat = (r_col == slot_c).astype(BF16)
                acc_ref[...] += _dot(pt_mat, y_hi) + _dot(pt_mat, y_lo)

    @pl.when(e == ne - 1)
    def _():
        o_ref[...] = x_ref[...] + g2_ref[...] * acc_ref[...].reshape(tb, tt, D_MODEL)


def _moe(h2, x1, g2, comb, wg, wu, wd, tb, tt):
    nb, t, _ = x1.shape
    rows = tb * tt
    cap = -(-(rows * 5 // 16) // 16) * 16
    n_tiles = (nb // tb) * (t // tt)
    cnt = (comb.reshape(n_tiles, rows, LANES)[:, :, :N_EXPERTS] > 0.0).sum(axis=1).astype(jnp.int32)
    row = lambda w: pl.BlockSpec((tb, tt, w), lambda b, i, e, c: (b, i, 0))
    modspec = pl.BlockSpec((tb, 1, D_MODEL), lambda b, i, e, c: (b, 0, 0))
    wspec_in = pl.BlockSpec((None, D_MODEL, D_FF_E), lambda b, i, e, c: (e, 0, 0))
    wspec_out = pl.BlockSpec((None, D_FF_E, D_MODEL), lambda b, i, e, c: (e, 0, 0))
    grid_spec = pltpu.PrefetchScalarGridSpec(
        num_scalar_prefetch=1,
        grid=(nb // tb, t // tt, N_EXPERTS),
        in_specs=[row(D_MODEL), row(D_MODEL), modspec, row(LANES), wspec_in, wspec_in, wspec_out],
        out_specs=row(D_MODEL),
        scratch_shapes=[pltpu.VMEM((rows, D_MODEL), F32),
                        pltpu.VMEM((rows, LANES), F32),
                        pltpu.VMEM((LANES, rows), F32),
                        pltpu.VMEM((LANES, rows), F32),
                        pltpu.VMEM((rows, rows), BF16),
                        pltpu.VMEM((rows, rows), BF16)])
    return pl.pallas_call(
        functools.partial(_moe_body, cap=cap),
        grid_spec=grid_spec,
        out_shape=jax.ShapeDtypeStruct((nb, t, D_MODEL), F32),
        compiler_params=_cparams(("arbitrary", "arbitrary", "arbitrary")),
        name="moe",
    )(cnt.reshape(-1), h2, x1, g2, comb, wg, wu, wd)


def _prepare_weights(P):
    W = {}
    W['w_mod'] = P['w_mod'].astype(BF16)
    w_in = P['w_in'].astype(BF16)
    W['wq'] = w_in[:, :, :ATT_WIDTH]
    W['wk'] = w_in[:, :, ATT_WIDTH:2 * ATT_WIDTH]
    W['wv'] = w_in[:, :, 2 * ATT_WIDTH:3 * ATT_WIDTH]
    W['wrw'] = w_in[:, :, 3 * ATT_WIDTH:]
    depth = P['w_in'].shape[0]
    zw = jnp.zeros((depth, LORA_A, RWKV_WIDTH), BF16)
    W['w2p'] = jnp.concatenate([P['w2'].astype(BF16), zw], axis=1)
    W['a2p'] = jnp.concatenate([zw, P['a2'].astype(BF16)], axis=1)
    W['g2'] = P['g2'].astype(BF16)
    w_out = P['w_out'].astype(BF16)
    W['wo_a'] = w_out[:, :ATT_WIDTH]
    W['wo_r'] = w_out[:, ATT_WIDTH:]
    W['w_ff_gate'] = P['w_ff_gate'].astype(BF16)
    W['w_ff_up'] = P['w_ff_up'].astype(BF16)
    W['w_ff_down'] = P['w_ff_down'].astype(BF16)
    W['w_moe_gate'] = P['w_moe_gate'].astype(BF16)
    W['w_moe_up'] = P['w_moe_up'].astype(BF16)
    W['w_moe_down'] = P['w_moe_down'].astype(BF16)
    wr = jnp.pad(P['w_router'], ((0, 0), (0, 0), (0, LANES - N_EXPERTS)))
    W['router_hi'] = wr.astype(BF16)
    W['router_lo'] = (wr - W['router_hi'].astype(F32)).astype(BF16)
    seg = jnp.arange(RWKV_WIDTH) // HEAD_R
    W['eseg'] = (seg[:, None] == seg[None, :]).astype(BF16)
    return W


def _run_group(x, c, cache_k, cache_v, wkv_init, shift_init, P, W, tb, tt, scan_chunk, scan_nb, scan_nsub, bq):
    nb, t, _ = x.shape
    depth = P['w_in'].shape[0]
    eseg = W['eseg']
    prompt = cache_k is None
    ks, vs, shifts = [], [], []
    kv_stacked = ()
    wkv_stacked = None
    for l in range(depth):
        mod = _mod(c, W['w_mod'][l], P['b_mod'][l])
        sh1, sc1, g1, sh2, sc2, g2 = [mod[:, None, i * D_MODEL:(i + 1) * D_MODEL] for i in range(6)]
        qg = jnp.tile(P['q_gain'][l].reshape(1, 2 * HD_QK), (1, N_HEADS_A))
        kg = jnp.tile(P['k_gain'][l].reshape(1, 2 * HD_QK), (1, N_HEADS_A))
        proj = _in_proj(x, sc1, sh1, P['g_norm1'][l].reshape(1, D_MODEL), W['wq'][l], W['wk'][l],
                        W['wv'][l], W['wrw'][l], qg, kg, eseg, tb, tt, l, depth,
                        kv_stacked if prompt else None)
        q, rw = proj[:2]
        lam_init = 0.8 - 0.6 * math.exp(-0.3 * l)
        lam_vecs = jnp.stack([P['lambda_q1'][l], P['lambda_k1'][l], P['lambda_q2'][l], P['lambda_k2'][l]])
        subln = P['subln'][l].reshape(1, HD_V)
        if prompt:
            kv_stacked = (proj[4], proj[5])
            o = _attn_prompt(q, proj[2], proj[3], lam_vecs, subln, lam_init, bq)
        else:
            k, v = proj[2], proj[3]
            ks.append(k.reshape(nb, t, N_HEADS_A, 2, HD_QK))
            vs.append(v.reshape(nb, t, N_HEADS_A, HD_V))
            o = _attn_sample(q, cache_k, cache_v, l, k, v, lam_vecs, subln, lam_init)
        vec = lambda a: a.reshape(1, RWKV_WIDTH)
        prep = _rwkv_prep(rw, shift_init[l], P['mu'][l].reshape(1, RW_COLS), vec(P['w0'][l]), W['w2p'][l],
                          vec(P['a0'][l]), W['a2p'][l], W['g2'][l], vec(P['k_k'][l]), vec(P['k_a'][l]),
                          vec(P['r_k'][l]), eseg, tb, tt)
        y_r, wkv_stacked = _rwkv_scan(prep, wkv_init, l, wkv_stacked, vec(P['lnx_w'][l]), vec(P['lnx_b'][l]),
                                      eseg, scan_chunk, scan_nb, scan_nsub)
        routed = (l % 2 == 1)
        j = l // 2
        router = (W['router_hi'][j], W['router_lo'][j]) if routed else None
        res = _out_proj(o, y_r, x, g1, sc2, sh2, P['g_norm2'][l].reshape(1, D_MODEL), W['wo_a'][l], W['wo_r'][l],
                        router, tb, tt)
        if routed:
            x1, h2, comb = res
            x = _moe(h2, x1, g2, comb, W['w_moe_gate'][j], W['w_moe_up'][j], W['w_moe_down'][j], tb, tt)
        else:
            x1, h2 = res
            x = _glu(h2, x1, g2, W['w_ff_gate'][j], W['w_ff_up'][j], W['w_ff_down'][j], tb, tt)
        shifts.append(rw[:, t - 1:, :])
    if prompt:
        kt, vr = kv_stacked
        k_out = jnp.transpose(kt.reshape(depth, nb, N_HEADS_A, 2, HD_QK, t), (0, 1, 5, 2, 3, 4))
        v_out = vr.reshape(depth, nb, t, N_HEADS_A, HD_V)
    else:
        k_out, v_out = jnp.stack(ks), jnp.stack(vs)
    return x, k_out, v_out, wkv_stacked, jnp.stack(shifts)


def kernel(x_prompt, x_sample, cache_k, cache_v, state_wkv, state_shift, c_prompt, c_sample, w_mod, b_mod, g_norm1, g_norm2, w_in, q_gain, k_gain, lambda_q1, lambda_k1, lambda_q2, lambda_k2, subln, mu, w0, w2, a0, a2, g2, k_k, k_a, r_k, lnx_w, lnx_b, w_out, w_ff_gate, w_ff_up, w_ff_down, w_router, w_moe_gate, w_moe_up, w_moe_down):
    P = dict(w_mod=w_mod, b_mod=b_mod, g_norm1=g_norm1, g_norm2=g_norm2, w_in=w_in, q_gain=q_gain,
             k_gain=k_gain, lambda_q1=lambda_q1, lambda_k1=lambda_k1, lambda_q2=lambda_q2,
             lambda_k2=lambda_k2, subln=subln, mu=mu, w0=w0, w2=w2, a0=a0, a2=a2, g2=g2, k_k=k_k,
             k_a=k_a, r_k=r_k, lnx_w=lnx_w, lnx_b=lnx_b, w_out=w_out, w_ff_gate=w_ff_gate,
             w_ff_up=w_ff_up, w_ff_down=w_ff_down, w_router=w_router, w_moe_gate=w_moe_gate,
             w_moe_up=w_moe_up, w_moe_down=w_moe_down)
    W = _prepare_weights(P)
    depth = w_in.shape[0]
    bp, tp, _ = x_prompt.shape
    bs, ts, _ = x_sample.shape
    n_past = cache_k.shape[2]
    wkv0 = jnp.zeros((depth, bp, N_HEADS_R, HEAD_R, HEAD_R), F32)
    shift0 = jnp.zeros((depth, bp, 1, RW_COLS), F32)
    y_p, k_p, v_p, wkv_p, shift_p = _run_group(x_prompt, c_prompt, None, None, wkv0, shift0, P, W,
                                               tb=1, tt=512, scan_chunk=CHUNK, scan_nb=bp, scan_nsub=2, bq=512)
    ck = jnp.transpose(cache_k, (0, 1, 3, 4, 5, 2)).reshape(depth, bs, ATT_WIDTH, n_past)
    cv = cache_v.reshape(depth, bs, n_past * N_HEADS_A, HD_V)
    y_s, k_s, v_s, wkv_s, shift_s = _run_group(x_sample, c_sample, ck, cv, state_wkv, state_shift, P, W,
                                               tb=bs, tt=ts, scan_chunk=ts, scan_nb=4, scan_nsub=1, bq=None)
    return (y_p, y_s, k_p, v_p, wkv_p, shift_p, k_s, v_s, wkv_s, shift_s)
```

```python
import functools
import math

import jax
import jax.numpy as jnp
from jax import lax
from jax.experimental import pallas as pl
from jax.experimental.pallas import tpu as pltpu

F32 = jnp.float32
BF16 = jnp.bfloat16

D_MODEL = 1024
CHUNK = 64
N_HEADS_A = 4
HD_V = 128
HD_QK = 64
ATT_WIDTH = 512
RWKV_WIDTH = 512
HEAD_R = 64
N_HEADS_R = 8
LORA_W = 64
LORA_A = 64
LORA_G = 128
RW_COLS = 3 * RWKV_WIDTH + LORA_W + LORA_A + LORA_G
D_FF = 2816
N_EXPERTS = 8
D_FF_E = 1408
NORM_EPS = 1e-6
LNX_EPS = 64e-5
NEG_BIG = -1e30
LANES = 128
VMEM_LIMIT = 56 * 1024 * 1024
SCAN_GH = 2

def _cparams(sem):
    return pltpu.CompilerParams(dimension_semantics=sem, vmem_limit_bytes=VMEM_LIMIT)


def _dot(a, b):
    return jnp.dot(a, b, preferred_element_type=F32)


def _dot_nt(a, b):
    return lax.dot_general(a, b, (((1,), (1,)), ((), ())), preferred_element_type=F32)


def _dot_tn(a, b):
    return lax.dot_general(a, b, (((0,), (0,)), ((), ())), preferred_element_type=F32)


def _split2(x):
    hi = x.astype(BF16)
    lo = (x - hi.astype(F32)).astype(BF16)
    return hi, lo


def _seg_sum(x, e):
    hi, lo = _split2(x)
    return _dot(hi, e) + _dot(lo, e)


def _sigmoid(x):
    return 1.0 / (1.0 + jnp.exp(-x))


def _mod_body(c_ref, w_ref, b_ref, o_ref):
    c = c_ref[...]
    cs = c * _sigmoid(c)
    o_ref[...] = _dot(cs.astype(BF16), w_ref[...]) + b_ref[...]


def _mod(c, w_mod, b_mod):
    nb = c.shape[0]
    n = w_mod.shape[1]
    tn = 1536
    return pl.pallas_call(
        _mod_body,
        grid=(n // tn,),
        in_specs=[pl.BlockSpec((nb, D_MODEL), lambda j: (0, 0)),
                  pl.BlockSpec((D_MODEL, tn), lambda j: (0, j)),
                  pl.BlockSpec((1, tn), lambda j: (0, j))],
        out_specs=pl.BlockSpec((nb, tn), lambda j: (0, j)),
        out_shape=jax.ShapeDtypeStruct((nb, n), F32),
        compiler_params=_cparams(("arbitrary",)),
        name="mod",
    )(c, w_mod, b_mod.reshape(1, n))


def _in_body(*refs, prompt, n_alias):
    x_ref, sc_ref, sh_ref, gn_ref, wq_ref, wk_ref, wv_ref, wr_ref, qg_ref, kg_ref, e_ref = refs[:11]
    outs = refs[11 + n_alias:]
    tb, tt, _ = x_ref.shape
    x = x_ref[...]
    ms = jnp.mean(x * x, axis=-1, keepdims=True)
    h = x * lax.rsqrt(ms + NORM_EPS) * gn_ref[...]
    h = h * (1.0 + sc_ref[...]) + sh_ref[...]
    hb = h.reshape(tb * tt, D_MODEL).astype(BF16)
    e = e_ref[...]

    def group_norm(c, gain):
        msq = _seg_sum(c * c, e) * (1.0 / HD_QK)
        return c * lax.rsqrt(msq + NORM_EPS) * gain

    q = group_norm(_dot(hb, wq_ref[...]), qg_ref[...])
    k = group_norm(_dot(hb, wk_ref[...]), kg_ref[...])
    v = _dot(hb, wv_ref[...])
    rw = _dot(hb, wr_ref[...]).reshape(tb, tt, RW_COLS)
    if prompt:
        q_ref, rw_ref, kb_ref, vt_ref, kt_ref, vr_ref = outs
        kb_ref[...] = k.reshape(tb, tt, ATT_WIDTH).astype(BF16)
        vt_ref[0] = v.T.astype(BF16)
        kt_ref[0] = k.T
        for hd in range(N_HEADS_A):
            vr_ref[0, pl.ds(hd, tt, stride=N_HEADS_A), :] = v[:, hd * HD_V:(hd + 1) * HD_V]
    else:
        q_ref, rw_ref, k_ref, v_ref = outs
        k_ref[...] = k.reshape(tb, tt, ATT_WIDTH)
        v_ref[...] = v.reshape(tb, tt, ATT_WIDTH)
    q_ref[...] = q.reshape(tb, tt, ATT_WIDTH).astype(BF16)
    rw_ref[...] = rw


def _in_proj(x, sc, sh, gn, wq, wk, wv, wr, qg, kg, eseg, tb, tt, layer, depth, stacked):
    nb, t, _ = x.shape
    prompt = stacked is not None
    row = lambda w: pl.BlockSpec((tb, tt, w), lambda b, i: (b, i, 0))
    modspec = pl.BlockSpec((tb, 1, D_MODEL), lambda b, i: (b, 0, 0))
    full = lambda a: pl.BlockSpec(a.shape, lambda b, i: (0,) * a.ndim)
    ins = [x, sc, sh, gn, wq, wk, wv, wr, qg, kg, eseg]
    in_specs = [row(D_MODEL), modspec, modspec, full(gn), full(wq), full(wk), full(wv), full(wr),
                full(qg), full(kg), full(eseg)]
    out_specs = [row(ATT_WIDTH), row(RW_COLS)]
    out_shape = [jax.ShapeDtypeStruct((nb, t, ATT_WIDTH), BF16), jax.ShapeDtypeStruct((nb, t, RW_COLS), F32)]
    aliases = {}
    if prompt:
        assert tb == 1
        out_specs += [row(ATT_WIDTH), pl.BlockSpec((1, ATT_WIDTH, tt), lambda b, i: (b, 0, i)),
                      pl.BlockSpec((None, 1, ATT_WIDTH, tt), lambda b, i: (layer, b, 0, i)),
                      pl.BlockSpec((None, 1, N_HEADS_A * tt, HD_V), lambda b, i: (layer, b, i, 0))]
        out_shape += [jax.ShapeDtypeStruct((nb, t, ATT_WIDTH), BF16),
                      jax.ShapeDtypeStruct((nb, ATT_WIDTH, t), BF16),
                      jax.ShapeDtypeStruct((depth, nb, ATT_WIDTH, t), F32),
                      jax.ShapeDtypeStruct((depth, nb, N_HEADS_A * t, HD_V), F32)]
        for n, prev in enumerate(stacked):
            aliases[len(ins)] = 4 + n
            ins.append(prev)
            in_specs.append(pl.BlockSpec(memory_space=pl.ANY))
    else:
        out_specs += [row(ATT_WIDTH), row(ATT_WIDTH)]
        out_shape += [jax.ShapeDtypeStruct((nb, t, ATT_WIDTH), F32)] * 2
    return pl.pallas_call(
        functools.partial(_in_body, prompt=prompt, n_alias=len(aliases)),
        grid=(nb // tb, t // tt),
        in_specs=in_specs,
        out_specs=out_specs,
        out_shape=out_shape,
        input_output_aliases=aliases,
        compiler_params=_cparams(("arbitrary", "arbitrary")),
        name="in_proj",
    )(*ins)


def _alibi_slope(h):
    return 2.0 ** (-8.0 * (h + 1) / N_HEADS_A)


def _lambda_value(lam_ref, lam_init):
    lv = lam_ref[...]
    s1 = jnp.sum(lv[0:1] * lv[1:2], axis=-1, keepdims=True)
    s2 = jnp.sum(lv[2:3] * lv[3:4], axis=-1, keepdims=True)
    return jnp.exp(s1) - jnp.exp(s2) + lam_init


def _stack_maps(qh):
    lane = lax.broadcasted_iota(jnp.int32, qh.shape, 1)
    qs = qh * (HD_QK ** -0.5)
    zero = jnp.zeros_like(qs)
    return jnp.concatenate([jnp.where(lane < HD_QK, qs, zero), jnp.where(lane >= HD_QK, qs, zero)],
                           axis=0)


def _sub_norm(o, gain, lam_init):
    ms = jnp.mean(o * o, axis=-1, keepdims=True)
    return o * lax.rsqrt(ms + NORM_EPS) * gain * (1.0 - lam_init)


def _attn_body(it_ref, jt_ref, q_ref, k_ref, vt_ref, lam_ref, sub_ref, o_ref, q2_ref, m_ref, l_ref,
               acc_ref, s_ref, p_ref, *, bq, lam_init):
    step_id = pl.program_id(1)
    i = it_ref[step_id]
    j = jt_ref[step_id]
    bk = bq
    nq2 = 2 * bq
    kc = LANES
    lane = lax.broadcasted_iota(jnp.int32, (bq, LANES), 1)

    @pl.when(j == 0)
    def _():
        m_ref[...] = jnp.full(m_ref.shape, NEG_BIG, F32)
        l_ref[...] = jnp.zeros(l_ref.shape, F32)
        acc_ref[...] = jnp.zeros(acc_ref.shape, F32)
        lane2 = lax.broadcasted_iota(jnp.int32, (nq2, LANES), 1)
        for h in range(N_HEADS_A):
            slope = _alibi_slope(h)
            feat = jnp.where(lane2 == 0, slope * CHUNK, jnp.where(lane2 == 1, slope, 0.0)).astype(BF16)
            q2_ref[h] = jnp.concatenate([_stack_maps(q_ref[0, :, h * HD_V:(h + 1) * HD_V]), feat], axis=1)

    def step(diag):
        krel = lax.broadcasted_iota(jnp.int32, (bk, LANES), 0) + (j - i) * bq
        kfeat = jnp.where(lane == 0, krel >> 6, jnp.where(lane == 1, krel & (CHUNK - 1), 0))
        kfeat = kfeat.astype(F32).astype(BF16)
        subs = range(bk // kc)
        npair = nq2 // (2 * LANES)

        def qk(h, g):
            kh = jnp.concatenate([k_ref[0, :, h * HD_V:(h + 1) * HD_V], kfeat], axis=1)
            st2 = _dot_nt(kh, q2_ref[h, g * 2 * LANES:(g + 1) * 2 * LANES, :])
            if diag:
                c = lax.broadcasted_iota(jnp.int32, (bk, 2 * LANES), 0)
                r = lax.broadcasted_iota(jnp.int32, (bk, 2 * LANES), 1) + (g * 2 * LANES) % bq
                ahead = jnp.maximum(c - r, 0).astype(F32) * (-2.0 * _alibi_slope(h))
                st2 = jnp.where((c >> 6) <= (r >> 6), st2 + ahead, NEG_BIG)
            s_ref[h % 2, 2 * g] = st2[:, :LANES]
            s_ref[h % 2, 2 * g + 1] = st2[:, LANES:]
            return jnp.max(st2, axis=0, keepdims=True)

        def softmax(h, g, m_blk):
            alpha = []
            for n, st in enumerate((2 * g, 2 * g + 1)):
                sl = slice(st * LANES, (st + 1) * LANES)
                m_old = m_ref[h, :, sl]
                m_new = jnp.maximum(m_old, m_blk[:, n * LANES:(n + 1) * LANES])
                a = jnp.exp(m_old - m_new)
                psum = None
                for kb in subs:
                    p = jnp.exp(s_ref[h % 2, st, kb * kc:(kb + 1) * kc, :] - m_new)
                    psum = p if psum is None else psum + p
                    p_ref[st, kb * kc:(kb + 1) * kc, :] = p.astype(BF16)
                l_ref[h, :, sl] = a * l_ref[h, :, sl] + jnp.sum(psum, axis=0, keepdims=True)
                m_ref[h, :, sl] = m_new
                alpha.append(a)
            return jnp.concatenate(alpha, axis=1)

        def pv(h, g, a_row):
            p2 = jnp.concatenate([p_ref[2 * g], p_ref[2 * g + 1]], axis=1)
            sl = slice(g * 2 * LANES, (g + 1) * 2 * LANES)
            acc_ref[h, :, sl] = acc_ref[h, :, sl] * a_row + _dot(vt_ref[0, h * HD_V:(h + 1) * HD_V, :], p2)

        m_blk = {(0, g): qk(0, g) for g in range(npair)}
        for h in range(N_HEADS_A):
            for g in range(npair):
                if h + 1 < N_HEADS_A:
                    m_blk[(h + 1, g)] = qk(h + 1, g)
                a_row = softmax(h, g, m_blk.pop((h, g)))
                pv(h, g, a_row)

    @pl.when(j < i)
    def _():
        step(False)

    @pl.when(j == i)
    def _():
        step(True)
        lam = _lambda_value(lam_ref, lam_init)
        for h in range(N_HEADS_A):
            o1 = acc_ref[h, :, :bq] / l_ref[h, :, :bq]
            o2 = acc_ref[h, :, bq:] / l_ref[h, :, bq:]
            o = (o1 - lam * o2).T
            o_ref[0, :, h * HD_V:(h + 1) * HD_V] = _sub_norm(o, sub_ref[...], lam_init)


def _attn_prompt(q, kb, vt, lam_vecs, subln, lam_init, bq):
    nb, t, _ = q.shape
    nq = t // bq
    pairs = [(i, j) for i in range(nq) for j in range(i + 1)]
    i_tab = jnp.asarray([p[0] for p in pairs], jnp.int32)
    j_tab = jnp.asarray([p[1] for p in pairs], jnp.int32)
    qspec = pl.BlockSpec((1, bq, ATT_WIDTH), lambda b, s, it, jt: (b, it[s], 0))
    kspec = pl.BlockSpec((1, bq, ATT_WIDTH), lambda b, s, it, jt: (b, jt[s], 0))
    vspec = pl.BlockSpec((1, ATT_WIDTH, bq), lambda b, s, it, jt: (b, 0, jt[s]))
    grid_spec = pltpu.PrefetchScalarGridSpec(
        num_scalar_prefetch=2,
        grid=(nb, len(pairs)),
        in_specs=[qspec, kspec, vspec,
                  pl.BlockSpec(lam_vecs.shape, lambda b, s, it, jt: (0, 0)),
                  pl.BlockSpec(subln.shape, lambda b, s, it, jt: (0, 0))],
        out_specs=qspec,
        scratch_shapes=[pltpu.VMEM((N_HEADS_A, 2 * bq, 2 * HD_V), BF16),
                        pltpu.VMEM((N_HEADS_A, 1, 2 * bq), F32),
                        pltpu.VMEM((N_HEADS_A, 1, 2 * bq), F32),
                        pltpu.VMEM((N_HEADS_A, HD_V, 2 * bq), F32),
                        pltpu.VMEM((2, 2 * bq // LANES, bq, LANES), F32),
                        pltpu.VMEM((2 * bq // LANES, bq, LANES), BF16)])
    return pl.pallas_call(
        functools.partial(_attn_body, bq=bq, lam_init=lam_init),
        grid_spec=grid_spec,
        out_shape=jax.ShapeDtypeStruct((nb, t, ATT_WIDTH), F32),
        compiler_params=_cparams(("arbitrary", "arbitrary")),
        name="attn_prompt",
    )(i_tab, j_tab, q, kb, vt, lam_vecs, subln)


def _attn_dec_body(q_ref, ck_ref, cv_ref, kn_ref, vn_ref, lam_ref, sub_ref, o_ref, *, lam_init):
    tq = q_ref.shape[1]
    n_past = ck_ref.shape[1]
    lam = _lambda_value(lam_ref, lam_init)
    r = lax.broadcasted_iota(jnp.int32, (tq, n_past), 0)
    c = lax.broadcasted_iota(jnp.int32, (tq, n_past), 1)
    dist_c = jnp.abs((r + n_past - c).astype(F32))
    rn = lax.broadcasted_iota(jnp.int32, (tq, tq), 0)
    cn = lax.broadcasted_iota(jnp.int32, (tq, tq), 1)
    dist_n = jnp.abs((rn - cn).astype(F32))
    for h in range(N_HEADS_A):
        sl = slice(h * HD_V, (h + 1) * HD_V)
        q2 = _stack_maps(q_ref[0, :, sl])
        vc = cv_ref[pl.ds(h, n_past, stride=N_HEADS_A), :].astype(BF16)
        kn = kn_ref[0, :, sl].astype(BF16)
        vn = vn_ref[0, :, sl].astype(BF16)
        sn_all = _dot_nt(q2, kn)
        slope = -_alibi_slope(h)
        outs = []
        sc_all = _dot(q2, ck_ref[sl, :].astype(BF16))
        for mp in range(2):
            s_c = sc_all[mp * tq:(mp + 1) * tq] + dist_c * slope
            s_n = sn_all[mp * tq:(mp + 1) * tq] + dist_n * slope
            m = jnp.maximum(jnp.max(s_c, axis=-1, keepdims=True), jnp.max(s_n, axis=-1, keepdims=True))
            p_c = jnp.exp(s_c - m)
            p_n = jnp.exp(s_n - m)
            l = jnp.sum(p_c, axis=-1, keepdims=True) + jnp.sum(p_n, axis=-1, keepdims=True)
            acc = _dot(p_c.astype(BF16), vc) + _dot(p_n.astype(BF16), vn)
            outs.append(acc / l)
        o = outs[0] - lam * outs[1]
        o_ref[0, :, sl] = _sub_norm(o, sub_ref[...], lam_init)


def _attn_sample(q, cache_k, cache_v, layer, k_new, v_new, lam_vecs, subln, lam_init):
    nb, tq, _ = q.shape
    n_past = cache_k.shape[3]
    row = pl.BlockSpec((1, tq, ATT_WIDTH), lambda b: (b, 0, 0))
    kspec = pl.BlockSpec((None, None, ATT_WIDTH, n_past), lambda b: (layer, b, 0, 0))
    vspec = pl.BlockSpec((None, None, n_past * N_HEADS_A, HD_V), lambda b: (layer, b, 0, 0))
    return pl.pallas_call(
        functools.partial(_attn_dec_body, lam_init=lam_init),
        grid=(nb,),
        in_specs=[row, kspec, vspec, row, row,
                  pl.BlockSpec(lam_vecs.shape, lambda b: (0, 0)),
                  pl.BlockSpec(subln.shape, lambda b: (0, 0))],
        out_specs=row,
        out_shape=jax.ShapeDtypeStruct((nb, tq, ATT_WIDTH), F32),
        compiler_params=_cparams(("arbitrary",)),
        name="attn_sample",
    )(q, cache_k, cache_v, k_new, v_new, lam_vecs, subln)


def _prep_body(rw_ref, shift_ref, mu_ref, w0_ref, w2_ref, a0_ref, a2_ref, g2_ref, kk_ref, ka_ref,
               rk_ref, e_ref, r_o, k_o, v_o, kk_o, bh_o, lw_o, g_o, bon_o, carry_ref):
    tb, tt, w = rw_ref.shape
    i = pl.program_id(1)
    rows = tb * tt
    cols = rw_ref[...].reshape(rows, w)
    rolled = pltpu.roll(cols, 1, 0)
    rowi = lax.broadcasted_iota(jnp.int32, (rows, 1), 0)
    if tb == 1:
        @pl.when(i == 0)
        def _():
            carry_ref[...] = shift_ref[0]

        prev = jnp.where(rowi == 0, carry_ref[...], rolled)
        carry_ref[...] = cols[tt - 1:tt, :]
    else:
        sh = jnp.broadcast_to(shift_ref[...], (tb, tt, w)).reshape(rows, w)
        prev = jnp.where((rowi % tt) == 0, sh, rolled)
    xs = cols + (prev - cols) * mu_ref[...]
    o1, o2, o3 = RWKV_WIDTH, 2 * RWKV_WIDTH, 3 * RWKV_WIDTH
    r = xs[:, :o1]
    k = xs[:, o1:o2]
    v = xs[:, o2:o3]
    da = xs[:, o3:o3 + LORA_W + LORA_A]
    gd = xs[:, o3 + LORA_W + LORA_A:]
    e = e_ref[...]
    z = w0_ref[...] + _dot(jnp.tanh(da).astype(BF16), w2_ref[...])
    lw_o[...] = (-math.exp(-0.5) * _sigmoid(z)).reshape(tb, tt, o1)
    a = _sigmoid(a0_ref[...] + _dot(da.astype(BF16), a2_ref[...]))
    g_o[...] = _dot(_sigmoid(gd).astype(BF16), g2_ref[...]).reshape(tb, tt, o1)
    kk = k * kk_ref[...]
    nrm = jnp.sqrt(_seg_sum(kk * kk, e))
    kk = kk / jnp.maximum(nrm, 1e-12)
    kp = k * (1.0 + (a - 1.0) * ka_ref[...])
    bon = _seg_sum(r * kp * rk_ref[...], e) * v
    r_o[...] = r.reshape(tb, tt, o1)
    k_o[...] = kp.reshape(tb, tt, o1)
    v_o[...] = v.reshape(tb, tt, o1)
    kk_o[...] = kk.reshape(tb, tt, o1)
    bh_o[...] = (kk * a).reshape(tb, tt, o1)
    bon_o[...] = bon.reshape(tb, tt, o1)


def _rwkv_prep(rw, shift, mu, w0, w2p, a0, a2p, g2, k_k, k_a, r_k, eseg, tb, tt):
    nb, t, _ = rw.shape
    row = lambda w: pl.BlockSpec((tb, tt, w), lambda b, i: (b, i, 0))
    full = lambda a: pl.BlockSpec(a.shape, lambda b, i: (0,) * a.ndim)
    out = jax.ShapeDtypeStruct((nb, t, RWKV_WIDTH), F32)
    return pl.pallas_call(
        _prep_body,
        grid=(nb // tb, t // tt),
        in_specs=[row(RW_COLS), pl.BlockSpec((tb, 1, RW_COLS), lambda b, i: (b, 0, 0)), full(mu), full(w0),
                  full(w2p), full(a0), full(a2p), full(g2), full(k_k), full(k_a), full(r_k), full(eseg)],
        out_specs=[row(RWKV_WIDTH)] * 8,
        out_shape=[out] * 8,
        scratch_shapes=[pltpu.VMEM((1, RW_COLS), F32)],
        compiler_params=_cparams(("arbitrary", "arbitrary")),
        name="rwkv_prep",
    )(rw, shift, mu, w0, w2p, a0, a2p, g2, k_k, k_a, r_k, eseg)


def _scan_body(*refs, C, nsub):
    r_ref, k_ref, v_ref, kk_ref, bh_ref, lw_ref, g_ref, bon_ref, s0_ref, lnw_ref, lnb_ref, e_ref = refs[:12]
    y_ref, sout_ref, st_ref = refs[-3:]
    ci = pl.program_id(1)
    nc = pl.num_programs(1)

    nbk = lw_ref.shape[0]

    @pl.when(ci == 0)
    def _():
        z = jnp.zeros((HEAD_R, HEAD_R), F32)
        for b in range(nbk):
            for q in range(N_HEADS_R // SCAN_GH):
                blocks = [jnp.concatenate([s0_ref[b, SCAN_GH * q + hh] if hc == hh else z
                                           for hc in range(SCAN_GH)], axis=1) for hh in range(SCAN_GH)]
                st_ref[b, q] = jnp.concatenate(blocks, axis=0)

    ti = lax.broadcasted_iota(jnp.int32, (C, C), 0)
    si = lax.broadcasted_iota(jnp.int32, (C, C), 1)
    ltri = (si <= ti).astype(BF16)
    segs = [(b, u) for u in range(nsub) for b in range(nbk)]
    rows_of = lambda u: slice(u * C, (u + 1) * C)
    at, rt, bt, kt, bb, kb, g_c, vb = [], [], [], [], [], [], [], []
    for (b, u) in segs:
        rw = rows_of(u)
        lw = lw_ref[b, rw, :]
        hi = lw.astype(BF16)
        r1 = lw - hi.astype(F32)
        mid = r1.astype(BF16)
        lo = (r1 - mid.astype(F32)).astype(BF16)
        cs = _dot(ltri, hi) + _dot(ltri, mid) + _dot(ltri, lo)
        cend = cs[C - 1:C, :]
        kk = kk_ref[b, rw, :]
        bh = bh_ref[b, rw, :]
        kp = k_ref[b, rw, :]
        e_neg = jnp.exp(-cs)
        e_end = jnp.exp(cend - cs)
        at.append((-kk * jnp.exp(cs - lw)).astype(BF16))
        rt.append((r_ref[b, rw, :] * jnp.exp(cs)).astype(BF16))
        bt.append((bh * e_neg).astype(BF16))
        kt.append((kp * e_neg).astype(BF16))
        bb.append((bh * e_end).astype(BF16))
        kb.append((kp * e_end).astype(BF16))
        g_c.append(jnp.exp(cend))
        vb.append(v_ref[b, rw, :].astype(BF16))

    C2 = SCAN_GH * C
    gw = SCAN_GH * HEAD_R
    log_c = C.bit_length() - 1
    log_h = HEAD_R.bit_length() - 1
    hm = ((lax.broadcasted_iota(jnp.int32, (C2, gw), 0) >> log_c)
          == (lax.broadcasted_iota(jnp.int32, (C2, gw), 1) >> log_h))
    hm2 = ((lax.broadcasted_iota(jnp.int32, (C2, C2), 0) >> log_c)
           == (lax.broadcasted_iota(jnp.int32, (C2, C2), 1) >> log_c))
    hm_state = ((lax.broadcasted_iota(jnp.int32, (gw, gw), 0) >> log_h)
                == (lax.broadcasted_iota(jnp.int32, (gw, gw), 1) >> log_h))
    tt = lax.broadcasted_iota(jnp.int32, (C, C2), 0)
    ss = lax.broadcasted_iota(jnp.int32, (C, C2), 1) & (C - 1)
    strict = ss < tt
    incl = ss <= tt
    eye = (ss == tt).astype(F32)

    def bd(x, mask):
        xx = jnp.concatenate([x] * SCAN_GH, axis=0)
        return jnp.where(mask, xx, jnp.zeros_like(xx)).astype(BF16)

    def mm(x, y):
        return _dot(x.astype(BF16), bd(y, hm2))

    npair = N_HEADS_R // SCAN_GH
    chains = [(sg, q) for sg in range(len(segs)) for q in range(npair)]
    pairs = range(len(chains))
    sls = [slice(q * gw, (q + 1) * gw) for (_, q) in chains]
    zero = jnp.zeros((C, C2), F32)
    a_l = [at[sg][:, sls[p]] for p, (sg, _) in enumerate(chains)]
    r_l = [rt[sg][:, sls[p]] for p, (sg, _) in enumerate(chains)]
    v_l = [vb[sg][:, sls[p]] for p, (sg, _) in enumerate(chains)]
    ar_l = [jnp.concatenate([a_l[p], r_l[p]], axis=0) for p in pairs]
    gb_l = [_dot_nt(ar_l[p], bd(bt[chains[p][0]][:, sls[p]], hm)) for p in pairs]
    gk_l = [_dot_nt(ar_l[p], bd(kt[chains[p][0]][:, sls[p]], hm)) for p in pairs]
    n_l = [jnp.where(strict, gb_l[p][:C], zero) for p in pairs]
    aks_l = [jnp.where(strict, gk_l[p][:C], zero).astype(BF16) for p in pairs]
    rbi_l = [jnp.where(incl, gb_l[p][C:], zero).astype(BF16) for p in pairs]
    rki_l = [jnp.where(incl, gk_l[p][C:], zero).astype(BF16) for p in pairs]
    nd_l = [jnp.where((ss >> 3) == (tt >> 3), n_l[p], zero) for p in pairs]
    nd2_l = [mm(nd_l[p], nd_l[p]) for p in pairs]
    nd4_l = [mm(nd2_l[p], nd2_l[p]) for p in pairs]
    t_l = [mm(eye + nd_l[p], eye + nd2_l[p]) for p in pairs]
    t_l = [mm(t_l[p], eye + nd4_l[p]) for p in pairs]
    lvl = 3
    while (1 << lvl) < C:
        tb_ = tt >> lvl
        ml = ((tb_ & 1) == 1) & ((ss >> lvl) == tb_ - 1)
        tn_l = [mm(t_l[p], jnp.where(ml, n_l[p], zero)) for p in pairs]
        t_l = [t_l[p] + mm(tn_l[p], t_l[p]) for p in pairs]
        lvl += 1
    vbd_l = [bd(v_l[p], hm) for p in pairs]
    t_l = [t.astype(BF16) for t in t_l]
    ys = [None] * len(chains)
    for u in range(nsub):
        cur = [p for p in pairs if segs[chains[p][0]][1] == u]
        s_l = {p: st_ref[segs[chains[p][0]][0], chains[p][1]] for p in cur}
        sb_l = {p: s_l[p].astype(BF16) for p in cur}
        x_l = {p: _dot_nt(a_l[p], sb_l[p]) + _dot(aks_l[p], vbd_l[p]) for p in cur}
        u_l = {p: _dot(t_l[p], bd(x_l[p], hm)) for p in cur}
        for p in cur:
            ys[p] = _dot_nt(r_l[p], sb_l[p]) + _dot(rbi_l[p], bd(u_l[p], hm)) + _dot(rki_l[p], vbd_l[p])
        for p in cur:
            sg, q = chains[p]
            uv = jnp.concatenate([u_l[p].astype(BF16), v_l[p]], axis=0)
            bk = jnp.concatenate([bb[sg][:, sls[p]], kb[sg][:, sls[p]]], axis=0)
            s_add = _dot_tn(uv, bk)
            st_ref[segs[sg][0], q] = (s_l[p] * g_c[sg][:, sls[p]]
                                      + jnp.where(hm_state, s_add, jnp.zeros_like(s_add)))

    e = e_ref[...]
    ycat = [jnp.concatenate(ys[sg * npair:(sg + 1) * npair], axis=1) for sg in range(len(segs))]
    mean = [_seg_sum(y, e) * (1.0 / HEAD_R) for y in ycat]
    d = [ycat[sg] - mean[sg] for sg in range(len(segs))]
    var = [_seg_sum(x * x, e) * (1.0 / HEAD_R) for x in d]
    for sg, (b, u) in enumerate(segs):
        yn = d[sg] * lax.rsqrt(var[sg] + LNX_EPS) * lnw_ref[...] + lnb_ref[...]
        y_ref[b, rows_of(u), :] = (yn + bon_ref[b, rows_of(u), :]) * g_ref[b, rows_of(u), :]

    @pl.when(ci == nc - 1)
    def _():
        for b in range(nbk):
            for q in range(N_HEADS_R // SCAN_GH):
                s = st_ref[b, q]
                for hh in range(SCAN_GH):
                    blk = slice(hh * HEAD_R, (hh + 1) * HEAD_R)
                    sout_ref[b, SCAN_GH * q + hh] = s[blk, blk]


def _rwkv_scan(prep, wkv_init, layer, wkv_prev, lnx_w, lnx_b, eseg, C, nbk, nsub):
    nb, t, _ = prep[0].shape
    row = pl.BlockSpec((nbk, C * nsub, RWKV_WIDTH), lambda b, c: (b, c, 0))
    sspec = pl.BlockSpec((None, nbk, N_HEADS_R, HEAD_R, HEAD_R), lambda b, c: (layer, b, 0, 0, 0))
    full = lambda a: pl.BlockSpec(a.shape, lambda b, c: (0,) * a.ndim)
    ins = list(prep) + [wkv_init, lnx_w, lnx_b, eseg]
    in_specs = [row] * 8 + [sspec, full(lnx_w), full(lnx_b), full(eseg)]
    aliases = {}
    if wkv_prev is not None:
        aliases[len(ins)] = 1
        ins.append(wkv_prev)
        in_specs.append(pl.BlockSpec(memory_space=pl.ANY))
    return pl.pallas_call(
        functools.partial(_scan_body, C=C, nsub=nsub),
        grid=(nb // nbk, t // (C * nsub)),
        in_specs=in_specs,
        out_specs=[row, sspec],
        out_shape=[jax.ShapeDtypeStruct((nb, t, RWKV_WIDTH), F32),
                   jax.ShapeDtypeStruct(wkv_init.shape, F32)],
        scratch_shapes=[pltpu.VMEM((nbk, N_HEADS_R // SCAN_GH, SCAN_GH * HEAD_R, SCAN_GH * HEAD_R), F32)],
        input_output_aliases=aliases,
        compiler_params=_cparams(("arbitrary", "arbitrary")),
        name="rwkv_scan",
    )(*ins)


def _out_body(*refs, routed):
    if routed:
        (o_ref, y_ref, x_ref, g1_ref, sc_ref, sh_ref, gn_ref, wa_ref, wr_ref, rh_ref, rl_ref,
         x1_ref, h2_ref, comb_ref) = refs
    else:
        (o_ref, y_ref, x_ref, g1_ref, sc_ref, sh_ref, gn_ref, wa_ref, wr_ref, x1_ref, h2_ref) = refs
    tb, tt, _ = x_ref.shape
    rows = tb * tt
    ob = o_ref[...].reshape(rows, ATT_WIDTH).astype(BF16)
    yb = y_ref[...].reshape(rows, RWKV_WIDTH).astype(BF16)
    mix = _dot(ob, wa_ref[...]) + _dot(yb, wr_ref[...])
    x1 = x_ref[...] + g1_ref[...] * mix.reshape(tb, tt, D_MODEL)
    x1_ref[...] = x1
    ms = jnp.mean(x1 * x1, axis=-1, keepdims=True)
    h2 = x1 * lax.rsqrt(ms + NORM_EPS) * gn_ref[...]
    h2 = h2 * (1.0 + sc_ref[...]) + sh_ref[...]
    h2_ref[...] = h2.astype(BF16)
    if routed:
        hf = h2.reshape(rows, D_MODEL)
        hi, lo = _split2(hf)
        logits = _dot(hi, rh_ref[...]) + _dot(hi, rl_ref[...]) + _dot(lo, rh_ref[...])
        lane = lax.broadcasted_iota(jnp.int32, logits.shape, 1)
        logits = jnp.where(lane < N_EXPERTS, logits, NEG_BIG)
        m1 = jnp.max(logits, axis=-1, keepdims=True)
        i1 = jnp.min(jnp.where(logits == m1, lane, LANES), axis=-1, keepdims=True)
        rest = jnp.where(lane == i1, NEG_BIG, logits)
        m2 = jnp.max(rest, axis=-1, keepdims=True)
        i2 = jnp.min(jnp.where(rest == m2, lane, LANES), axis=-1, keepdims=True)
        ex = jnp.exp(m2 - m1)
        gate1 = 1.0 / (1.0 + ex)
        gate2 = ex / (1.0 + ex)
        comb = jnp.where(lane == i1, gate1, 0.0) + jnp.where(lane == i2, gate2, 0.0)
        comb_ref[...] = comb.reshape(tb, tt, LANES)


def _out_proj(o, y, x, g1, sc2, sh2, gn2, wa, wr, router, tb, tt):
    nb, t, _ = x.shape
    routed = router is not None
    row = lambda w: pl.BlockSpec((tb, tt, w), lambda b, i: (b, i, 0))
    modspec = pl.BlockSpec((tb, 1, D_MODEL), lambda b, i: (b, 0, 0))
    full = lambda a: pl.BlockSpec(a.shape, lambda b, i: (0,) * a.ndim)
    ins = [o, y, x, g1, sc2, sh2, gn2, wa, wr]
    in_specs = [row(ATT_WIDTH), row(RWKV_WIDTH), row(D_MODEL), modspec, modspec, modspec, full(gn2),
                full(wa), full(wr)]
    out_specs = [row(D_MODEL), row(D_MODEL)]
    out_shape = [jax.ShapeDtypeStruct((nb, t, D_MODEL), F32), jax.ShapeDtypeStruct((nb, t, D_MODEL), BF16)]
    if routed:
        ins += list(router)
        in_specs += [full(router[0]), full(router[1])]
        out_specs.append(row(LANES))
        out_shape.append(jax.ShapeDtypeStruct((nb, t, LANES), F32))
    return pl.pallas_call(
        functools.partial(_out_body, routed=routed),
        grid=(nb // tb, t // tt),
        in_specs=in_specs,
        out_specs=out_specs,
        out_shape=out_shape,
        compiler_params=_cparams(("arbitrary", "arbitrary")),
        name="out_proj_routed" if routed else "out_proj",
    )(*ins)


def _glu_body(h_ref, x_ref, g2_ref, wg_ref, wu_ref, wd_ref, o_ref, acc_ref):
    tb, tt, _ = x_ref.shape
    rows = tb * tt
    e = pl.program_id(2)
    ne = pl.num_programs(2)

    @pl.when(e == 0)
    def _():
        acc_ref[...] = jnp.zeros(acc_ref.shape, F32)

    hb = h_ref[...].reshape(rows, D_MODEL)
    gate = _dot(hb, wg_ref[...])
    up = _dot(hb, wu_ref[...])
    act = gate * _sigmoid(gate) * up
    acc_ref[...] += _dot(act.astype(BF16), wd_ref[...])

    @pl.when(e == ne - 1)
    def _():
        o_ref[...] = x_ref[...] + g2_ref[...] * acc_ref[...].reshape(tb, tt, D_MODEL)


def _glu(h2, x1, g2, wg, wu, wd, tb, tt):
    nb, t, _ = x1.shape
    row = lambda w: pl.BlockSpec((tb, tt, w), lambda b, i, e: (b, i, 0))
    modspec = pl.BlockSpec((tb, 1, D_MODEL), lambda b, i, e: (b, 0, 0))
    wspec_in = pl.BlockSpec((D_MODEL, D_FF_E), lambda b, i, e: (0, e))
    wspec_out = pl.BlockSpec((D_FF_E, D_MODEL), lambda b, i, e: (e, 0))
    return pl.pallas_call(
        _glu_body,
        grid=(nb // tb, t // tt, D_FF // D_FF_E),
        in_specs=[row(D_MODEL), row(D_MODEL), modspec, wspec_in, wspec_in, wspec_out],
        out_specs=row(D_MODEL),
        out_shape=jax.ShapeDtypeStruct((nb, t, D_MODEL), F32),
        scratch_shapes=[pltpu.VMEM((tb * tt, D_MODEL), F32)],
        compiler_params=_cparams(("arbitrary", "arbitrary", "arbitrary")),
        name="glu_dense",
    )(h2, x1, g2, wg, wu, wd)


def _moe_body(cnt_ref, h_ref, x_ref, g2_ref, comb_ref, wg_ref, wu_ref, wd_ref,
              o_ref, acc_ref, rank_col_ref, rank_row_ref, comb_t_ref, ltri_ref, *, cap, rs):
    tb, tt, _ = x_ref.shape
    rows = tb * tt
    nsub = rows // rs
    e = pl.program_id(2)
    ne = pl.num_programs(2)
    tile = pl.program_id(0) * pl.num_programs(1) + pl.program_id(1)
    cap_p = -(-cap // LANES) * LANES
    hb = h_ref[...].reshape(rows, D_MODEL)
    comb = comb_ref[...].reshape(rows, LANES)

    @pl.when((tile == 0) & (e == 0))
    def _():
        ri = lax.broadcasted_iota(jnp.int32, (rs, rs), 0)
        ci = lax.broadcasted_iota(jnp.int32, (rs, rs), 1)
        ltri_ref[...] = (ci < ri).astype(BF16)

    @pl.when(e == 0)
    def _():
        acc_ref[...] = jnp.zeros(acc_ref.shape, F32)
        for s in range(nsub):
            sub = slice(s * rs, (s + 1) * rs)
            rank = _dot(ltri_ref[...], (comb[sub] > 0.0).astype(BF16))
            rank_col_ref[sub, :] = rank
            rank_row_ref[:, sub] = rank.T
            comb_t_ref[:, sub] = comb[sub].T

    for s in range(nsub):
        sub = slice(s * rs, (s + 1) * rs)
        count = cnt_ref[(tile * nsub + s) * N_EXPERTS + e]

        @pl.when(count > 0)
        def _():
            lane = lax.broadcasted_iota(jnp.int32, (rs, LANES), 1)
            pick = lane == e
            w_col = jnp.sum(jnp.where(pick, comb[sub], 0.0), axis=-1, keepdims=True)
            r_col = jnp.sum(jnp.where(pick, rank_col_ref[sub, :], 0.0), axis=-1, keepdims=True)
            r_col = jnp.where(w_col > 0.0, r_col, -1.0)
            w_row = comb_t_ref[pl.ds(e, 1), sub]
            r_row = jnp.where(w_row > 0.0, rank_row_ref[pl.ds(e, 1), sub], -1.0)
            w_rep = jnp.broadcast_to(w_col, (rs, LANES))
            w_hi, w_lo = _split2(w_rep)
            for c in range(-(-rs // cap)):
                @pl.when(count > c * cap)
                def _():
                    slot_r = (lax.broadcasted_iota(jnp.int32, (cap, rs), 0) + c * cap).astype(F32)
                    p_mat = (r_row == slot_r).astype(BF16)
                    xc = _dot(p_mat, hb[sub]).astype(BF16)
                    wc = (_dot(p_mat, w_hi) + _dot(p_mat, w_lo))[:, :1]
                    gate = _dot(xc, wg_ref[...])
                    up = _dot(xc, wu_ref[...])
                    act = gate * _sigmoid(gate) * up * wc
                    yc = _dot(act.astype(BF16), wd_ref[...])
                    yc = yc.astype(BF16)
                    if cap_p > cap:
                        yc = jnp.concatenate([yc, jnp.zeros((cap_p - cap, D_MODEL), BF16)], axis=0)
                    slot_c = (lax.broadcasted_iota(jnp.int32, (rs, cap_p), 1) + c * cap).astype(F32)
                    pt_mat = (r_col == slot_c).astype(BF16)
                    acc_ref[sub, :] += _dot(pt_mat, yc)

    @pl.when(e == ne - 1)
    def _():
        o_ref[...] = x_ref[...] + g2_ref[...] * acc_ref[...].reshape(tb, tt, D_MODEL)


def _moe(h2, x1, g2, comb, wg, wu, wd, tb, tt, rs):
    nb, t, _ = x1.shape
    rows = tb * tt
    cap = -(-(rs * 5 // 16) // 16) * 16
    n_sub = (nb // tb) * (t // tt) * (rows // rs)
    cnt = (comb.reshape(n_sub, rs, LANES)[:, :, :N_EXPERTS] > 0.0).sum(axis=1).astype(jnp.int32)
    row = lambda w: pl.BlockSpec((tb, tt, w), lambda b, i, e, c: (b, i, 0))
    modspec = pl.BlockSpec((tb, 1, D_MODEL), lambda b, i, e, c: (b, 0, 0))
    wspec_in = pl.BlockSpec((None, D_MODEL, D_FF_E), lambda b, i, e, c: (e, 0, 0))
    wspec_out = pl.BlockSpec((None, D_FF_E, D_MODEL), lambda b, i, e, c: (e, 0, 0))
    grid_spec = pltpu.PrefetchScalarGridSpec(
        num_scalar_prefetch=1,
        grid=(nb // tb, t // tt, N_EXPERTS),
        in_specs=[row(D_MODEL), row(D_MODEL), modspec, row(LANES), wspec_in, wspec_in, wspec_out],
        out_specs=row(D_MODEL),
        scratch_shapes=[pltpu.VMEM((rows, D_MODEL), F32),
                        pltpu.VMEM((rows, LANES), F32),
                        pltpu.VMEM((LANES, rows), F32),
                        pltpu.VMEM((LANES, rows), F32),
                        pltpu.VMEM((rs, rs), BF16)])
    return pl.pallas_call(
        functools.partial(_moe_body, cap=cap, rs=rs),
        grid_spec=grid_spec,
        out_shape=jax.ShapeDtypeStruct((nb, t, D_MODEL), F32),
        compiler_params=_cparams(("arbitrary", "arbitrary", "arbitrary")),
        name="moe",
    )(cnt.reshape(-1), h2, x1, g2, comb, wg, wu, wd)


def _prepare_weights(P):
    W = {}
    W['w_mod'] = P['w_mod'].astype(BF16)
    w_in = P['w_in'].astype(BF16)
    W['wq'] = w_in[:, :, :ATT_WIDTH]
    W['wk'] = w_in[:, :, ATT_WIDTH:2 * ATT_WIDTH]
    W['wv'] = w_in[:, :, 2 * ATT_WIDTH:3 * ATT_WIDTH]
    W['wrw'] = w_in[:, :, 3 * ATT_WIDTH:]
    depth = P['w_in'].shape[0]
    zw = jnp.zeros((depth, LORA_A, RWKV_WIDTH), BF16)
    W['w2p'] = jnp.concatenate([P['w2'].astype(BF16), zw], axis=1)
    W['a2p'] = jnp.concatenate([zw, P['a2'].astype(BF16)], axis=1)
    W['g2'] = P['g2'].astype(BF16)
    w_out = P['w_out'].astype(BF16)
    W['wo_a'] = w_out[:, :ATT_WIDTH]
    W['wo_r'] = w_out[:, ATT_WIDTH:]
    W['w_ff_gate'] = P['w_ff_gate'].astype(BF16)
    W['w_ff_up'] = P['w_ff_up'].astype(BF16)
    W['w_ff_down'] = P['w_ff_down'].astype(BF16)
    W['w_moe_gate'] = P['w_moe_gate'].astype(BF16)
    W['w_moe_up'] = P['w_moe_up'].astype(BF16)
    W['w_moe_down'] = P['w_moe_down'].astype(BF16)
    wr = jnp.pad(P['w_router'], ((0, 0), (0, 0), (0, LANES - N_EXPERTS)))
    W['router_hi'] = wr.astype(BF16)
    W['router_lo'] = (wr - W['router_hi'].astype(F32)).astype(BF16)
    seg = jnp.arange(RWKV_WIDTH) // HEAD_R
    W['eseg'] = (seg[:, None] == seg[None, :]).astype(BF16)
    return W


def _run_group(x, c, cache_k, cache_v, wkv_init, shift_init, P, W, tb, tt, scan_chunk, scan_nb, scan_nsub, bq):
    nb, t, _ = x.shape
    depth = P['w_in'].shape[0]
    eseg = W['eseg']
    prompt = cache_k is None
    ks, vs, shifts = [], [], []
    kv_stacked = ()
    wkv_stacked = None
    for l in range(depth):
        mod = _mod(c, W['w_mod'][l], P['b_mod'][l])
        sh1, sc1, g1, sh2, sc2, g2 = [mod[:, None, i * D_MODEL:(i + 1) * D_MODEL] for i in range(6)]
        qg = jnp.tile(P['q_gain'][l].reshape(1, 2 * HD_QK), (1, N_HEADS_A))
        kg = jnp.tile(P['k_gain'][l].reshape(1, 2 * HD_QK), (1, N_HEADS_A))
        proj = _in_proj(x, sc1, sh1, P['g_norm1'][l].reshape(1, D_MODEL), W['wq'][l], W['wk'][l],
                        W['wv'][l], W['wrw'][l], qg, kg, eseg, tb, tt, l, depth,
                        kv_stacked if prompt else None)
        q, rw = proj[:2]
        lam_init = 0.8 - 0.6 * math.exp(-0.3 * l)
        lam_vecs = jnp.stack([P['lambda_q1'][l], P['lambda_k1'][l], P['lambda_q2'][l], P['lambda_k2'][l]])
        subln = P['subln'][l].reshape(1, HD_V)
        if prompt:
            kv_stacked = (proj[4], proj[5])
            o = _attn_prompt(q, proj[2], proj[3], lam_vecs, subln, lam_init, bq)
        else:
            k, v = proj[2], proj[3]
            ks.append(k.reshape(nb, t, N_HEADS_A, 2, HD_QK))
            vs.append(v.reshape(nb, t, N_HEADS_A, HD_V))
            o = _attn_sample(q, cache_k, cache_v, l, k, v, lam_vecs, subln, lam_init)
        vec = lambda a: a.reshape(1, RWKV_WIDTH)
        prep = _rwkv_prep(rw, shift_init[l], P['mu'][l].reshape(1, RW_COLS), vec(P['w0'][l]), W['w2p'][l],
                          vec(P['a0'][l]), W['a2p'][l], W['g2'][l], vec(P['k_k'][l]), vec(P['k_a'][l]),
                          vec(P['r_k'][l]), eseg, tb, tt)
        y_r, wkv_stacked = _rwkv_scan(prep, wkv_init, l, wkv_stacked, vec(P['lnx_w'][l]), vec(P['lnx_b'][l]),
                                      eseg, scan_chunk, scan_nb, scan_nsub)
        routed = (l % 2 == 1)
        j = l // 2
        router = (W['router_hi'][j], W['router_lo'][j]) if routed else None
        res = _out_proj(o, y_r, x, g1, sc2, sh2, P['g_norm2'][l].reshape(1, D_MODEL), W['wo_a'][l], W['wo_r'][l],
                        router, tb, tt)
        if routed:
            x1, h2, comb = res
            x = _moe(h2, x1, g2, comb, W['w_moe_gate'][j], W['w_moe_up'][j], W['w_moe_down'][j], tb,
                     tt if tb > 1 else 2 * tt, tb * tt)
        else:
            x1, h2 = res
            x = _glu(h2, x1, g2, W['w_ff_gate'][j], W['w_ff_up'][j], W['w_ff_down'][j], tb, tt)
        shifts.append(rw[:, t - 1:, :])
    if prompt:
        kt, vr = kv_stacked
        k_out = jnp.transpose(kt.reshape(depth, nb, N_HEADS_A, 2, HD_QK, t), (0, 1, 5, 2, 3, 4))
        v_out = vr.reshape(depth, nb, t, N_HEADS_A, HD_V)
    else:
        k_out, v_out = jnp.stack(ks), jnp.stack(vs)
    return x, k_out, v_out, wkv_stacked, jnp.stack(shifts)


def kernel(x_prompt, x_sample, cache_k, cache_v, state_wkv, state_shift, c_prompt, c_sample, w_mod, b_mod, g_norm1, g_norm2, w_in, q_gain, k_gain, lambda_q1, lambda_k1, lambda_q2, lambda_k2, subln, mu, w0, w2, a0, a2, g2, k_k, k_a, r_k, lnx_w, lnx_b, w_out, w_ff_gate, w_ff_up, w_ff_down, w_router, w_moe_gate, w_moe_up, w_moe_down):
    P = dict(w_mod=w_mod, b_mod=b_mod, g_norm1=g_norm1, g_norm2=g_norm2, w_in=w_in, q_gain=q_gain,
             k_gain=k_gain, lambda_q1=lambda_q1, lambda_k1=lambda_k1, lambda_q2=lambda_q2,
             lambda_k2=lambda_k2, subln=subln, mu=mu, w0=w0, w2=w2, a0=a0, a2=a2, g2=g2, k_k=k_k,
             k_a=k_a, r_k=r_k, lnx_w=lnx_w, lnx_b=lnx_b, w_out=w_out, w_ff_gate=w_ff_gate,
             w_ff_up=w_ff_up, w_ff_down=w_ff_down, w_router=w_router, w_moe_gate=w_moe_gate,
             w_moe_up=w_moe_up, w_moe_down=w_moe_down)
    W = _prepare_weights(P)
    depth = w_in.shape[0]
    bp, tp, _ = x_prompt.shape
    bs, ts, _ = x_sample.shape
    n_past = cache_k.shape[2]
    wkv0 = jnp.zeros((depth, bp, N_HEADS_R, HEAD_R, HEAD_R), F32)
    shift0 = jnp.zeros((depth, bp, 1, RW_COLS), F32)
    y_p, k_p, v_p, wkv_p, shift_p = _run_group(x_prompt, c_prompt, None, None, wkv0, shift0, P, W,
                                               tb=1, tt=512, scan_chunk=CHUNK, scan_nb=bp, scan_nsub=2, bq=512)
    ck = jnp.transpose(cache_k, (0, 1, 3, 4, 5, 2)).reshape(depth, bs, ATT_WIDTH, n_past)
    cv = cache_v.reshape(depth, bs, n_past * N_HEADS_A, HD_V)
    y_s, k_s, v_s, wkv_s, shift_s = _run_group(x_sample, c_sample, ck, cv, state_wkv, state_shift, P, W,
                                               tb=bs, tt=ts, scan_chunk=ts, scan_nb=4, scan_nsub=1, bq=None)
    return (y_p, y_s, k_p, v_p, wkv_p, shift_p, k_s, v_s, wkv_s, shift_s)
```

```python
import functools
import math

import jax
import jax.numpy as jnp
from jax import lax
from jax.experimental import pallas as pl
from jax.experimental.pallas import tpu as pltpu

F32 = jnp.float32
BF16 = jnp.bfloat16

D_MODEL = 1024
CHUNK = 64
N_HEADS_A = 4
HD_V = 128
HD_QK = 64
ATT_WIDTH = 512
RWKV_WIDTH = 512
HEAD_R = 64
N_HEADS_R = 8
LORA_W = 64
LORA_A = 64
LORA_G = 128
RW_COLS = 3 * RWKV_WIDTH + LORA_W + LORA_A + LORA_G
D_FF = 2816
N_EXPERTS = 8
D_FF_E = 1408
NORM_EPS = 1e-6
LNX_EPS = 64e-5
NEG_BIG = -1e30
LANES = 128
VMEM_LIMIT = 56 * 1024 * 1024
SCAN_GH = 2

def _cparams(sem):
    return pltpu.CompilerParams(dimension_semantics=sem, vmem_limit_bytes=VMEM_LIMIT)


def _dot(a, b):
    return jnp.dot(a, b, preferred_element_type=F32)


def _dot_nt(a, b):
    return lax.dot_general(a, b, (((1,), (1,)), ((), ())), preferred_element_type=F32)


def _dot_tn(a, b):
    return lax.dot_general(a, b, (((0,), (0,)), ((), ())), preferred_element_type=F32)


def _split2(x):
    hi = x.astype(BF16)
    lo = (x - hi.astype(F32)).astype(BF16)
    return hi, lo


def _seg_sum(x, e, split=True):
    if not split:
        return _dot(x.astype(BF16), e)
    hi, lo = _split2(x)
    return _dot(hi, e) + _dot(lo, e)


def _sigmoid(x):
    return 1.0 / (1.0 + jnp.exp(-x))


def _mod_body(c_ref, w_ref, b_ref, o_ref):
    c = c_ref[...]
    cs = c * _sigmoid(c)
    o_ref[...] = _dot(cs.astype(BF16), w_ref[...]) + b_ref[...]


def _mod(c, w_mod, b_mod):
    nb = c.shape[0]
    n = w_mod.shape[1]
    tn = 1536
    return pl.pallas_call(
        _mod_body,
        grid=(n // tn,),
        in_specs=[pl.BlockSpec((nb, D_MODEL), lambda j: (0, 0)),
                  pl.BlockSpec((D_MODEL, tn), lambda j: (0, j)),
                  pl.BlockSpec((1, tn), lambda j: (0, j))],
        out_specs=pl.BlockSpec((nb, tn), lambda j: (0, j)),
        out_shape=jax.ShapeDtypeStruct((nb, n), F32),
        compiler_params=_cparams(("arbitrary",)),
        name="mod",
    )(c, w_mod, b_mod.reshape(1, n))


def _in_body(*refs, prompt, n_alias):
    x_ref, sc_ref, sh_ref, gn_ref, wq_ref, wk_ref, wv_ref, wr_ref, qg_ref, kg_ref, e_ref = refs[:11]
    outs = refs[11 + n_alias:]
    tb, tt, _ = x_ref.shape
    x = x_ref[...]
    ms = jnp.mean(x * x, axis=-1, keepdims=True)
    h = x * lax.rsqrt(ms + NORM_EPS) * gn_ref[...]
    h = h * (1.0 + sc_ref[...]) + sh_ref[...]
    hb = h.reshape(tb * tt, D_MODEL).astype(BF16)
    e = e_ref[...]

    def group_norm(c, gain):
        msq = _seg_sum(c * c, e, split=False) * (1.0 / HD_QK)
        return c * lax.rsqrt(msq + NORM_EPS) * gain

    q = group_norm(_dot(hb, wq_ref[...]), qg_ref[...])
    k = group_norm(_dot(hb, wk_ref[...]), kg_ref[...])
    v = _dot(hb, wv_ref[...])
    rw = _dot(hb, wr_ref[...]).reshape(tb, tt, RW_COLS)
    if prompt:
        q_ref, rw_ref, kb_ref, vt_ref, kt_ref, vr_ref = outs
        kb_ref[...] = k.reshape(tb, tt, ATT_WIDTH).astype(BF16)
        vt_ref[0] = v.T.astype(BF16)
        kt_ref[0] = k.T
        for hd in range(N_HEADS_A):
            vr_ref[0, pl.ds(hd, tt, stride=N_HEADS_A), :] = v[:, hd * HD_V:(hd + 1) * HD_V]
    else:
        q_ref, rw_ref, k_ref, v_ref = outs
        k_ref[...] = k.reshape(tb, tt, ATT_WIDTH)
        v_ref[...] = v.reshape(tb, tt, ATT_WIDTH)
    q_ref[...] = q.reshape(tb, tt, ATT_WIDTH).astype(BF16)
    rw_ref[...] = rw


def _in_proj(x, sc, sh, gn, wq, wk, wv, wr, qg, kg, eseg, tb, tt, layer, depth, stacked):
    nb, t, _ = x.shape
    prompt = stacked is not None
    row = lambda w: pl.BlockSpec((tb, tt, w), lambda b, i: (b, i, 0))
    modspec = pl.BlockSpec((tb, 1, D_MODEL), lambda b, i: (b, 0, 0))
    full = lambda a: pl.BlockSpec(a.shape, lambda b, i: (0,) * a.ndim)
    ins = [x, sc, sh, gn, wq, wk, wv, wr, qg, kg, eseg]
    in_specs = [row(D_MODEL), modspec, modspec, full(gn), full(wq), full(wk), full(wv), full(wr),
                full(qg), full(kg), full(eseg)]
    out_specs = [row(ATT_WIDTH), row(RW_COLS)]
    out_shape = [jax.ShapeDtypeStruct((nb, t, ATT_WIDTH), BF16), jax.ShapeDtypeStruct((nb, t, RW_COLS), F32)]
    aliases = {}
    if prompt:
        assert tb == 1
        out_specs += [row(ATT_WIDTH), pl.BlockSpec((1, ATT_WIDTH, tt), lambda b, i: (b, 0, i)),
                      pl.BlockSpec((None, 1, ATT_WIDTH, tt), lambda b, i: (layer, b, 0, i)),
                      pl.BlockSpec((None, 1, N_HEADS_A * tt, HD_V), lambda b, i: (layer, b, i, 0))]
        out_shape += [jax.ShapeDtypeStruct((nb, t, ATT_WIDTH), BF16),
                      jax.ShapeDtypeStruct((nb, ATT_WIDTH, t), BF16),
                      jax.ShapeDtypeStruct((depth, nb, ATT_WIDTH, t), F32),
                      jax.ShapeDtypeStruct((depth, nb, N_HEADS_A * t, HD_V), F32)]
        for n, prev in enumerate(stacked):
            aliases[len(ins)] = 4 + n
            ins.append(prev)
            in_specs.append(pl.BlockSpec(memory_space=pl.ANY))
    else:
        out_specs += [row(ATT_WIDTH), row(ATT_WIDTH)]
        out_shape += [jax.ShapeDtypeStruct((nb, t, ATT_WIDTH), F32)] * 2
    return pl.pallas_call(
        functools.partial(_in_body, prompt=prompt, n_alias=len(aliases)),
        grid=(nb // tb, t // tt),
        in_specs=in_specs,
        out_specs=out_specs,
        out_shape=out_shape,
        input_output_aliases=aliases,
        compiler_params=_cparams(("arbitrary", "arbitrary")),
        name="in_proj",
    )(*ins)


def _alibi_slope(h):
    return 2.0 ** (-8.0 * (h + 1) / N_HEADS_A)


def _lambda_value(lam_ref, lam_init):
    lv = lam_ref[...]
    s1 = jnp.sum(lv[0:1] * lv[1:2], axis=-1, keepdims=True)
    s2 = jnp.sum(lv[2:3] * lv[3:4], axis=-1, keepdims=True)
    return jnp.exp(s1) - jnp.exp(s2) + lam_init


def _stack_maps(qh):
    lane = lax.broadcasted_iota(jnp.int32, qh.shape, 1)
    qs = qh * (HD_QK ** -0.5)
    zero = jnp.zeros_like(qs)
    return jnp.concatenate([jnp.where(lane < HD_QK, qs, zero), jnp.where(lane >= HD_QK, qs, zero)],
                           axis=0)


def _sub_norm(o, gain, lam_init):
    ms = jnp.mean(o * o, axis=-1, keepdims=True)
    return o * lax.rsqrt(ms + NORM_EPS) * gain * (1.0 - lam_init)


def _attn_body(it_ref, jt_ref, q_ref, k_ref, vt_ref, lam_ref, sub_ref, o_ref, q2_ref, m_ref, l_ref,
               acc_ref, s_ref, p_ref, *, bq, lam_init):
    step_id = pl.program_id(1)
    i = it_ref[step_id]
    j = jt_ref[step_id]
    bk = bq
    nq2 = 2 * bq
    kc = 32
    lane = lax.broadcasted_iota(jnp.int32, (bq, LANES), 1)

    @pl.when(j == 0)
    def _():
        m_ref[...] = jnp.full(m_ref.shape, NEG_BIG, F32)
        l_ref[...] = jnp.zeros(l_ref.shape, F32)
        acc_ref[...] = jnp.zeros(acc_ref.shape, F32)
        lane2 = lax.broadcasted_iota(jnp.int32, (nq2, LANES), 1)
        for h in range(N_HEADS_A):
            slope = _alibi_slope(h)
            feat = jnp.where(lane2 == 0, slope * CHUNK, jnp.where(lane2 == 1, slope, 0.0)).astype(BF16)
            q2_ref[h] = jnp.concatenate([_stack_maps(q_ref[0, :, h * HD_V:(h + 1) * HD_V]), feat], axis=1)

    def step(diag):
        krel = lax.broadcasted_iota(jnp.int32, (bk, LANES), 0) + (j - i) * bq
        kfeat = jnp.where(lane == 0, krel >> 6, jnp.where(lane == 1, krel & (CHUNK - 1), 0))
        kfeat = kfeat.astype(F32).astype(BF16)
        subs = range(bk // kc)
        npair = nq2 // (2 * LANES)

        def qk(h, g):
            kh = jnp.concatenate([k_ref[0, :, h * HD_V:(h + 1) * HD_V], kfeat], axis=1)
            st2 = _dot_nt(kh, q2_ref[h, g * 2 * LANES:(g + 1) * 2 * LANES, :])
            if diag:
                c = lax.broadcasted_iota(jnp.int32, (bk, 2 * LANES), 0)
                r = lax.broadcasted_iota(jnp.int32, (bk, 2 * LANES), 1) + (g * 2 * LANES) % bq
                ahead = jnp.maximum(c - r, 0).astype(F32) * (-2.0 * _alibi_slope(h))
                st2 = jnp.where((c >> 6) <= (r >> 6), st2 + ahead, NEG_BIG)
            s_ref[h % 2, 2 * g] = st2[:, :LANES]
            s_ref[h % 2, 2 * g + 1] = st2[:, LANES:]
            return jnp.max(st2, axis=0, keepdims=True)

        def softmax(h, g, m_blk):
            alpha = []
            for n, st in enumerate((2 * g, 2 * g + 1)):
                sl = slice(st * LANES, (st + 1) * LANES)
                m_old = m_ref[h, :, sl]
                m_new = jnp.maximum(m_old, m_blk[:, n * LANES:(n + 1) * LANES])
                a = jnp.exp(m_old - m_new)
                psum = None
                for kb in subs:
                    p = jnp.exp(s_ref[h % 2, st, kb * kc:(kb + 1) * kc, :] - m_new)
                    psum = p if psum is None else psum + p
                    p_ref[st, kb * kc:(kb + 1) * kc, :] = p.astype(BF16)
                l_ref[h, :, sl] = a * l_ref[h, :, sl] + jnp.sum(psum, axis=0, keepdims=True)
                m_ref[h, :, sl] = m_new
                alpha.append(a)
            return jnp.concatenate(alpha, axis=1)

        def pv(h, g, a_row):
            p2 = jnp.concatenate([p_ref[2 * g], p_ref[2 * g + 1]], axis=1)
            sl = slice(g * 2 * LANES, (g + 1) * 2 * LANES)
            acc_ref[h, :, sl] = acc_ref[h, :, sl] * a_row + _dot(vt_ref[0, h * HD_V:(h + 1) * HD_V, :], p2)

        m_blk = {(0, g): qk(0, g) for g in range(npair)}
        for h in range(N_HEADS_A):
            for g in range(npair):
                if h + 1 < N_HEADS_A:
                    m_blk[(h + 1, g)] = qk(h + 1, g)
                a_row = softmax(h, g, m_blk.pop((h, g)))
                pv(h, g, a_row)

    @pl.when(j < i)
    def _():
        step(False)

    @pl.when(j == i)
    def _():
        step(True)
        lam = _lambda_value(lam_ref, lam_init)
        for h in range(N_HEADS_A):
            o1 = acc_ref[h, :, :bq] / l_ref[h, :, :bq]
            o2 = acc_ref[h, :, bq:] / l_ref[h, :, bq:]
            o = (o1 - lam * o2).T
            o_ref[0, :, h * HD_V:(h + 1) * HD_V] = _sub_norm(o, sub_ref[...], lam_init)


def _attn_prompt(q, kb, vt, lam_vecs, subln, lam_init, bq):
    nb, t, _ = q.shape
    nq = t // bq
    pairs = [(i, j) for i in range(nq) for j in range(i + 1)]
    i_tab = jnp.asarray([p[0] for p in pairs], jnp.int32)
    j_tab = jnp.asarray([p[1] for p in pairs], jnp.int32)
    qspec = pl.BlockSpec((1, bq, ATT_WIDTH), lambda b, s, it, jt: (b, it[s], 0))
    kspec = pl.BlockSpec((1, bq, ATT_WIDTH), lambda b, s, it, jt: (b, jt[s], 0))
    vspec = pl.BlockSpec((1, ATT_WIDTH, bq), lambda b, s, it, jt: (b, 0, jt[s]))
    grid_spec = pltpu.PrefetchScalarGridSpec(
        num_scalar_prefetch=2,
        grid=(nb, len(pairs)),
        in_specs=[qspec, kspec, vspec,
                  pl.BlockSpec(lam_vecs.shape, lambda b, s, it, jt: (0, 0)),
                  pl.BlockSpec(subln.shape, lambda b, s, it, jt: (0, 0))],
        out_specs=qspec,
        scratch_shapes=[pltpu.VMEM((N_HEADS_A, 2 * bq, 2 * HD_V), BF16),
                        pltpu.VMEM((N_HEADS_A, 1, 2 * bq), F32),
                        pltpu.VMEM((N_HEADS_A, 1, 2 * bq), F32),
                        pltpu.VMEM((N_HEADS_A, HD_V, 2 * bq), F32),
                        pltpu.VMEM((2, 2 * bq // LANES, bq, LANES), F32),
                        pltpu.VMEM((2 * bq // LANES, bq, LANES), BF16)])
    return pl.pallas_call(
        functools.partial(_attn_body, bq=bq, lam_init=lam_init),
        grid_spec=grid_spec,
        out_shape=jax.ShapeDtypeStruct((nb, t, ATT_WIDTH), F32),
        compiler_params=_cparams(("arbitrary", "arbitrary")),
        name="attn_prompt",
    )(i_tab, j_tab, q, kb, vt, lam_vecs, subln)


def _attn_dec_body(q_ref, ck_ref, cv_ref, kn_ref, vn_ref, lam_ref, sub_ref, o_ref, *, lam_init):
    tq = q_ref.shape[1]
    n_past = ck_ref.shape[1]
    lam = _lambda_value(lam_ref, lam_init)
    r = lax.broadcasted_iota(jnp.int32, (tq, n_past), 0)
    c = lax.broadcasted_iota(jnp.int32, (tq, n_past), 1)
    dist_c = jnp.abs((r + n_past - c).astype(F32))
    rn = lax.broadcasted_iota(jnp.int32, (tq, tq), 0)
    cn = lax.broadcasted_iota(jnp.int32, (tq, tq), 1)
    dist_n = jnp.abs((rn - cn).astype(F32))
    for h in range(N_HEADS_A):
        sl = slice(h * HD_V, (h + 1) * HD_V)
        q2 = _stack_maps(q_ref[0, :, sl])
        vc = cv_ref[pl.ds(h, n_past, stride=N_HEADS_A), :].astype(BF16)
        kn = kn_ref[0, :, sl].astype(BF16)
        vn = vn_ref[0, :, sl].astype(BF16)
        sn_all = _dot_nt(q2, kn)
        slope = -_alibi_slope(h)
        outs = []
        sc_all = _dot(q2, ck_ref[sl, :].astype(BF16))
        for mp in range(2):
            s_c = sc_all[mp * tq:(mp + 1) * tq] + dist_c * slope
            s_n = sn_all[mp * tq:(mp + 1) * tq] + dist_n * slope
            m = jnp.maximum(jnp.max(s_c, axis=-1, keepdims=True), jnp.max(s_n, axis=-1, keepdims=True))
            p_c = jnp.exp(s_c - m)
            p_n = jnp.exp(s_n - m)
            l = jnp.sum(p_c, axis=-1, keepdims=True) + jnp.sum(p_n, axis=-1, keepdims=True)
            acc = _dot(p_c.astype(BF16), vc) + _dot(p_n.astype(BF16), vn)
            outs.append(acc / l)
        o = outs[0] - lam * outs[1]
        o_ref[0, :, sl] = _sub_norm(o, sub_ref[...], lam_init)


def _attn_sample(q, cache_k, cache_v, layer, k_new, v_new, lam_vecs, subln, lam_init):
    nb, tq, _ = q.shape
    n_past = cache_k.shape[3]
    row = pl.BlockSpec((1, tq, ATT_WIDTH), lambda b: (b, 0, 0))
    kspec = pl.BlockSpec((None, None, ATT_WIDTH, n_past), lambda b: (layer, b, 0, 0))
    vspec = pl.BlockSpec((None, None, n_past * N_HEADS_A, HD_V), lambda b: (layer, b, 0, 0))
    return pl.pallas_call(
        functools.partial(_attn_dec_body, lam_init=lam_init),
        grid=(nb,),
        in_specs=[row, kspec, vspec, row, row,
                  pl.BlockSpec(lam_vecs.shape, lambda b: (0, 0)),
                  pl.BlockSpec(subln.shape, lambda b: (0, 0))],
        out_specs=row,
        out_shape=jax.ShapeDtypeStruct((nb, tq, ATT_WIDTH), F32),
        compiler_params=_cparams(("arbitrary",)),
        name="attn_sample",
    )(q, cache_k, cache_v, k_new, v_new, lam_vecs, subln)


def _prep_body(rw_ref, shift_ref, mu_ref, w0_ref, w2_ref, a0_ref, a2_ref, g2_ref, kk_ref, ka_ref,
               rk_ref, e_ref, r_o, k_o, v_o, kk_o, bh_o, lw_o, g_o, bon_o, carry_ref):
    tb, tt, w = rw_ref.shape
    i = pl.program_id(1)
    rows = tb * tt
    cols = rw_ref[...].reshape(rows, w)
    rolled = pltpu.roll(cols, 1, 0)
    rowi = lax.broadcasted_iota(jnp.int32, (rows, 1), 0)
    if tb == 1:
        @pl.when(i == 0)
        def _():
            carry_ref[...] = shift_ref[0]

        prev = jnp.where(rowi == 0, carry_ref[...], rolled)
        carry_ref[...] = cols[tt - 1:tt, :]
    else:
        sh = jnp.broadcast_to(shift_ref[...], (tb, tt, w)).reshape(rows, w)
        prev = jnp.where((rowi % tt) == 0, sh, rolled)
    xs = cols + (prev - cols) * mu_ref[...]
    o1, o2, o3 = RWKV_WIDTH, 2 * RWKV_WIDTH, 3 * RWKV_WIDTH
    r = xs[:, :o1]
    k = xs[:, o1:o2]
    v = xs[:, o2:o3]
    da = xs[:, o3:o3 + LORA_W + LORA_A]
    gd = xs[:, o3 + LORA_W + LORA_A:]
    e = e_ref[...]
    z = w0_ref[...] + _dot(jnp.tanh(da).astype(BF16), w2_ref[...])
    lw_o[...] = (-math.exp(-0.5) * _sigmoid(z)).reshape(tb, tt, o1)
    a = _sigmoid(a0_ref[...] + _dot(da.astype(BF16), a2_ref[...]))
    g_o[...] = _dot(_sigmoid(gd).astype(BF16), g2_ref[...]).reshape(tb, tt, o1)
    kk = k * kk_ref[...]
    nrm = jnp.sqrt(_seg_sum(kk * kk, e))
    kk = kk / jnp.maximum(nrm, 1e-12)
    kp = k * (1.0 + (a - 1.0) * ka_ref[...])
    bon = _seg_sum(r * kp * rk_ref[...], e, split=False) * v
    r_o[...] = r.reshape(tb, tt, o1)
    k_o[...] = kp.reshape(tb, tt, o1)
    v_o[...] = v.reshape(tb, tt, o1)
    kk_o[...] = kk.reshape(tb, tt, o1)
    bh_o[...] = (kk * a).reshape(tb, tt, o1)
    bon_o[...] = bon.reshape(tb, tt, o1)


def _rwkv_prep(rw, shift, mu, w0, w2p, a0, a2p, g2, k_k, k_a, r_k, eseg, tb, tt):
    nb, t, _ = rw.shape
    row = lambda w: pl.BlockSpec((tb, tt, w), lambda b, i: (b, i, 0))
    full = lambda a: pl.BlockSpec(a.shape, lambda b, i: (0,) * a.ndim)
    out = jax.ShapeDtypeStruct((nb, t, RWKV_WIDTH), F32)
    return pl.pallas_call(
        _prep_body,
        grid=(nb // tb, t // tt),
        in_specs=[row(RW_COLS), pl.BlockSpec((tb, 1, RW_COLS), lambda b, i: (b, 0, 0)), full(mu), full(w0),
                  full(w2p), full(a0), full(a2p), full(g2), full(k_k), full(k_a), full(r_k), full(eseg)],
        out_specs=[row(RWKV_WIDTH)] * 8,
        out_shape=[out] * 8,
        scratch_shapes=[pltpu.VMEM((1, RW_COLS), F32)],
        compiler_params=_cparams(("arbitrary", "arbitrary")),
        name="rwkv_prep",
    )(rw, shift, mu, w0, w2p, a0, a2p, g2, k_k, k_a, r_k, eseg)


def _scan_body(*refs, C, nsub):
    r_ref, k_ref, v_ref, kk_ref, bh_ref, lw_ref, g_ref, bon_ref, s0_ref, lnw_ref, lnb_ref, e_ref = refs[:12]
    y_ref, sout_ref, st_ref = refs[-3:]
    ci = pl.program_id(1)
    nc = pl.num_programs(1)

    nbk = lw_ref.shape[0]

    @pl.when(ci == 0)
    def _():
        z = jnp.zeros((HEAD_R, HEAD_R), F32)
        for b in range(nbk):
            for q in range(N_HEADS_R // SCAN_GH):
                blocks = [jnp.concatenate([s0_ref[b, SCAN_GH * q + hh] if hc == hh else z
                                           for hc in range(SCAN_GH)], axis=1) for hh in range(SCAN_GH)]
                st_ref[b, q] = jnp.concatenate(blocks, axis=0)

    ti = lax.broadcasted_iota(jnp.int32, (C, C), 0)
    si = lax.broadcasted_iota(jnp.int32, (C, C), 1)
    ltri = (si <= ti).astype(BF16)
    segs = [(b, u) for u in range(nsub) for b in range(nbk)]
    rows_of = lambda u: slice(u * C, (u + 1) * C)
    at, rt, bt, kt, bb, kb, g_c, vb = [], [], [], [], [], [], [], []
    for (b, u) in segs:
        rw = rows_of(u)
        lw = lw_ref[b, rw, :]
        hi = lw.astype(BF16)
        r1 = lw - hi.astype(F32)
        mid = r1.astype(BF16)
        lo = (r1 - mid.astype(F32)).astype(BF16)
        cs = _dot(ltri, hi) + _dot(ltri, mid) + _dot(ltri, lo)
        cend = cs[C - 1:C, :]
        kk = kk_ref[b, rw, :]
        bh = bh_ref[b, rw, :]
        kp = k_ref[b, rw, :]
        e_neg = jnp.exp(-cs)
        e_end = jnp.exp(cend - cs)
        at.append((-kk * jnp.exp(cs - lw)).astype(BF16))
        rt.append((r_ref[b, rw, :] * jnp.exp(cs)).astype(BF16))
        bt.append((bh * e_neg).astype(BF16))
        kt.append((kp * e_neg).astype(BF16))
        bb.append((bh * e_end).astype(BF16))
        kb.append((kp * e_end).astype(BF16))
        g_c.append(jnp.exp(cend))
        vb.append(v_ref[b, rw, :].astype(BF16))

    C2 = SCAN_GH * C
    gw = SCAN_GH * HEAD_R
    log_c = C.bit_length() - 1
    log_h = HEAD_R.bit_length() - 1
    hm = ((lax.broadcasted_iota(jnp.int32, (C2, gw), 0) >> log_c)
          == (lax.broadcasted_iota(jnp.int32, (C2, gw), 1) >> log_h))
    hm2 = ((lax.broadcasted_iota(jnp.int32, (C2, C2), 0) >> log_c)
           == (lax.broadcasted_iota(jnp.int32, (C2, C2), 1) >> log_c))
    hm_state = ((lax.broadcasted_iota(jnp.int32, (gw, gw), 0) >> log_h)
                == (lax.broadcasted_iota(jnp.int32, (gw, gw), 1) >> log_h))
    tt = lax.broadcasted_iota(jnp.int32, (C, C2), 0)
    ss = lax.broadcasted_iota(jnp.int32, (C, C2), 1) & (C - 1)
    strict = ss < tt
    incl = ss <= tt
    eye = (ss == tt).astype(F32)

    def bd(x, mask):
        xx = jnp.concatenate([x] * SCAN_GH, axis=0)
        return jnp.where(mask, xx, jnp.zeros_like(xx)).astype(BF16)

    def mm(x, y):
        return _dot(x.astype(BF16), bd(y, hm2))

    npair = N_HEADS_R // SCAN_GH
    chains = [(sg, q) for sg in range(len(segs)) for q in range(npair)]
    pairs = range(len(chains))
    sls = [slice(q * gw, (q + 1) * gw) for (_, q) in chains]
    zero = jnp.zeros((C, C2), F32)
    a_l = [at[sg][:, sls[p]] for p, (sg, _) in enumerate(chains)]
    r_l = [rt[sg][:, sls[p]] for p, (sg, _) in enumerate(chains)]
    v_l = [vb[sg][:, sls[p]] for p, (sg, _) in enumerate(chains)]
    ar_l = [jnp.concatenate([a_l[p], r_l[p]], axis=0) for p in pairs]
    gb_l = [_dot_nt(ar_l[p], bd(bt[chains[p][0]][:, sls[p]], hm)) for p in pairs]
    gk_l = [_dot_nt(ar_l[p], bd(kt[chains[p][0]][:, sls[p]], hm)) for p in pairs]
    n_l = [jnp.where(strict, gb_l[p][:C], zero) for p in pairs]
    aks_l = [jnp.where(strict, gk_l[p][:C], zero).astype(BF16) for p in pairs]
    rbi_l = [jnp.where(incl, gb_l[p][C:], zero).astype(BF16) for p in pairs]
    rki_l = [jnp.where(incl, gk_l[p][C:], zero).astype(BF16) for p in pairs]
    nd_l = [jnp.where((ss >> 3) == (tt >> 3), n_l[p], zero) for p in pairs]
    nd2_l = [mm(nd_l[p], nd_l[p]) for p in pairs]
    nd4_l = [mm(nd2_l[p], nd2_l[p]) for p in pairs]
    t_l = [mm(eye + nd_l[p], eye + nd2_l[p]) for p in pairs]
    t_l = [mm(t_l[p], eye + nd4_l[p]) for p in pairs]
    lvl = 3
    while (1 << lvl) < C:
        tb_ = tt >> lvl
        ml = ((tb_ & 1) == 1) & ((ss >> lvl) == tb_ - 1)
        tn_l = [mm(t_l[p], jnp.where(ml, n_l[p], zero)) for p in pairs]
        t_l = [t_l[p] + mm(tn_l[p], t_l[p]) for p in pairs]
        lvl += 1
    vbd_l = [bd(v_l[p], hm) for p in pairs]
    t_l = [t.astype(BF16) for t in t_l]
    ys = [None] * len(chains)
    for u in range(nsub):
        cur = [p for p in pairs if segs[chains[p][0]][1] == u]
        s_l = {p: st_ref[segs[chains[p][0]][0], chains[p][1]] for p in cur}
        sb_l = {p: s_l[p].astype(BF16) for p in cur}
        x_l = {p: _dot_nt(a_l[p], sb_l[p]) + _dot(aks_l[p], vbd_l[p]) for p in cur}
        u_l = {p: _dot(t_l[p], bd(x_l[p], hm)) for p in cur}
        for p in cur:
            ys[p] = _dot_nt(r_l[p], sb_l[p]) + _dot(rbi_l[p], bd(u_l[p], hm)) + _dot(rki_l[p], vbd_l[p])
        for p in cur:
            sg, q = chains[p]
            uv = jnp.concatenate([u_l[p].astype(BF16), v_l[p]], axis=0)
            bk = jnp.concatenate([bb[sg][:, sls[p]], kb[sg][:, sls[p]]], axis=0)
            s_add = _dot_tn(uv, bk)
            st_ref[segs[sg][0], q] = (s_l[p] * g_c[sg][:, sls[p]]
                                      + jnp.where(hm_state, s_add, jnp.zeros_like(s_add)))

    e = e_ref[...]
    ycat = [jnp.concatenate(ys[sg * npair:(sg + 1) * npair], axis=1) for sg in range(len(segs))]
    mean = [_seg_sum(y, e, split=False) * (1.0 / HEAD_R) for y in ycat]
    d = [ycat[sg] - mean[sg] for sg in range(len(segs))]
    var = [_seg_sum(x * x, e, split=False) * (1.0 / HEAD_R) for x in d]
    for sg, (b, u) in enumerate(segs):
        yn = d[sg] * lax.rsqrt(var[sg] + LNX_EPS) * lnw_ref[...] + lnb_ref[...]
        y_ref[b, rows_of(u), :] = (yn + bon_ref[b, rows_of(u), :]) * g_ref[b, rows_of(u), :]

    @pl.when(ci == nc - 1)
    def _():
        for b in range(nbk):
            for q in range(N_HEADS_R // SCAN_GH):
                s = st_ref[b, q]
                for hh in range(SCAN_GH):
                    blk = slice(hh * HEAD_R, (hh + 1) * HEAD_R)
                    sout_ref[b, SCAN_GH * q + hh] = s[blk, blk]


def _rwkv_scan(prep, wkv_init, layer, wkv_prev, lnx_w, lnx_b, eseg, C, nbk, nsub):
    nb, t, _ = prep[0].shape
    row = pl.BlockSpec((nbk, C * nsub, RWKV_WIDTH), lambda b, c: (b, c, 0))
    sspec = pl.BlockSpec((None, nbk, N_HEADS_R, HEAD_R, HEAD_R), lambda b, c: (layer, b, 0, 0, 0))
    full = lambda a: pl.BlockSpec(a.shape, lambda b, c: (0,) * a.ndim)
    ins = list(prep) + [wkv_init, lnx_w, lnx_b, eseg]
    in_specs = [row] * 8 + [sspec, full(lnx_w), full(lnx_b), full(eseg)]
    aliases = {}
    if wkv_prev is not None:
        aliases[len(ins)] = 1
        ins.append(wkv_prev)
        in_specs.append(pl.BlockSpec(memory_space=pl.ANY))
    return pl.pallas_call(
        functools.partial(_scan_body, C=C, nsub=nsub),
        grid=(nb // nbk, t // (C * nsub)),
        in_specs=in_specs,
        out_specs=[row, sspec],
        out_shape=[jax.ShapeDtypeStruct((nb, t, RWKV_WIDTH), F32),
                   jax.ShapeDtypeStruct(wkv_init.shape, F32)],
        scratch_shapes=[pltpu.VMEM((nbk, N_HEADS_R // SCAN_GH, SCAN_GH * HEAD_R, SCAN_GH * HEAD_R), F32)],
        input_output_aliases=aliases,
        compiler_params=_cparams(("arbitrary", "arbitrary")),
        name="rwkv_scan",
    )(*ins)


def _out_body(*refs, routed):
    if routed:
        (o_ref, y_ref, x_ref, g1_ref, sc_ref, sh_ref, gn_ref, wa_ref, wr_ref, rh_ref, rl_ref,
         x1_ref, h2_ref, comb_ref) = refs
    else:
        (o_ref, y_ref, x_ref, g1_ref, sc_ref, sh_ref, gn_ref, wa_ref, wr_ref, x1_ref, h2_ref) = refs
    tb, tt, _ = x_ref.shape
    rows = tb * tt
    ob = o_ref[...].reshape(rows, ATT_WIDTH).astype(BF16)
    yb = y_ref[...].reshape(rows, RWKV_WIDTH).astype(BF16)
    mix = _dot(ob, wa_ref[...]) + _dot(yb, wr_ref[...])
    x1 = x_ref[...] + g1_ref[...] * mix.reshape(tb, tt, D_MODEL)
    x1_ref[...] = x1
    ms = jnp.mean(x1 * x1, axis=-1, keepdims=True)
    h2 = x1 * lax.rsqrt(ms + NORM_EPS) * gn_ref[...]
    h2 = h2 * (1.0 + sc_ref[...]) + sh_ref[...]
    h2_ref[...] = h2.astype(BF16)
    if routed:
        hf = h2.reshape(rows, D_MODEL)
        hi, lo = _split2(hf)
        logits = _dot(hi, rh_ref[...]) + _dot(hi, rl_ref[...]) + _dot(lo, rh_ref[...])
        lane = lax.broadcasted_iota(jnp.int32, logits.shape, 1)
        logits = jnp.where(lane < N_EXPERTS, logits, NEG_BIG)
        m1 = jnp.max(logits, axis=-1, keepdims=True)
        i1 = jnp.min(jnp.where(logits == m1, lane, LANES), axis=-1, keepdims=True)
        rest = jnp.where(lane == i1, NEG_BIG, logits)
        m2 = jnp.max(rest, axis=-1, keepdims=True)
        i2 = jnp.min(jnp.where(rest == m2, lane, LANES), axis=-1, keepdims=True)
        ex = jnp.exp(m2 - m1)
        gate1 = 1.0 / (1.0 + ex)
        gate2 = ex / (1.0 + ex)
        comb = jnp.where(lane == i1, gate1, 0.0) + jnp.where(lane == i2, gate2, 0.0)
        comb_ref[...] = comb.reshape(tb, tt, LANES)


def _out_proj(o, y, x, g1, sc2, sh2, gn2, wa, wr, router, tb, tt):
    nb, t, _ = x.shape
    routed = router is not None
    row = lambda w: pl.BlockSpec((tb, tt, w), lambda b, i: (b, i, 0))
    modspec = pl.BlockSpec((tb, 1, D_MODEL), lambda b, i: (b, 0, 0))
    full = lambda a: pl.BlockSpec(a.shape, lambda b, i: (0,) * a.ndim)
    ins = [o, y, x, g1, sc2, sh2, gn2, wa, wr]
    in_specs = [row(ATT_WIDTH), row(RWKV_WIDTH), row(D_MODEL), modspec, modspec, modspec, full(gn2),
                full(wa), full(wr)]
    out_specs = [row(D_MODEL), row(D_MODEL)]
    out_shape = [jax.ShapeDtypeStruct((nb, t, D_MODEL), F32), jax.ShapeDtypeStruct((nb, t, D_MODEL), BF16)]
    if routed:
        ins += list(router)
        in_specs += [full(router[0]), full(router[1])]
        out_specs.append(row(LANES))
        out_shape.append(jax.ShapeDtypeStruct((nb, t, LANES), F32))
    return pl.pallas_call(
        functools.partial(_out_body, routed=routed),
        grid=(nb // tb, t // tt),
        in_specs=in_specs,
        out_specs=out_specs,
        out_shape=out_shape,
        compiler_params=_cparams(("arbitrary", "arbitrary")),
        name="out_proj_routed" if routed else "out_proj",
    )(*ins)


def _glu_body(h_ref, x_ref, g2_ref, wg_ref, wu_ref, wd_ref, o_ref, acc_ref):
    tb, tt, _ = x_ref.shape
    rows = tb * tt
    e = pl.program_id(2)
    ne = pl.num_programs(2)

    @pl.when(e == 0)
    def _():
        acc_ref[...] = jnp.zeros(acc_ref.shape, F32)

    hb = h_ref[...].reshape(rows, D_MODEL)
    gate = _dot(hb, wg_ref[...])
    up = _dot(hb, wu_ref[...])
    act = gate * _sigmoid(gate) * up
    acc_ref[...] += _dot(act.astype(BF16), wd_ref[...])

    @pl.when(e == ne - 1)
    def _():
        o_ref[...] = x_ref[...] + g2_ref[...] * acc_ref[...].reshape(tb, tt, D_MODEL)


def _glu(h2, x1, g2, wg, wu, wd, tb, tt):
    nb, t, _ = x1.shape
    row = lambda w: pl.BlockSpec((tb, tt, w), lambda b, i, e: (b, i, 0))
    modspec = pl.BlockSpec((tb, 1, D_MODEL), lambda b, i, e: (b, 0, 0))
    wspec_in = pl.BlockSpec((D_MODEL, D_FF_E), lambda b, i, e: (0, e))
    wspec_out = pl.BlockSpec((D_FF_E, D_MODEL), lambda b, i, e: (e, 0))
    return pl.pallas_call(
        _glu_body,
        grid=(nb // tb, t // tt, D_FF // D_FF_E),
        in_specs=[row(D_MODEL), row(D_MODEL), modspec, wspec_in, wspec_in, wspec_out],
        out_specs=row(D_MODEL),
        out_shape=jax.ShapeDtypeStruct((nb, t, D_MODEL), F32),
        scratch_shapes=[pltpu.VMEM((tb * tt, D_MODEL), F32)],
        compiler_params=_cparams(("arbitrary", "arbitrary", "arbitrary")),
        name="glu_dense",
    )(h2, x1, g2, wg, wu, wd)


def _moe_body(cnt_ref, h_ref, x_ref, g2_ref, comb_ref, wg_ref, wu_ref, wd_ref,
              o_ref, acc_ref, rank_col_ref, rank_row_ref, comb_t_ref, ltri_ref, *, cap, rs):
    tb, tt, _ = x_ref.shape
    rows = tb * tt
    nsub = rows // rs
    e = pl.program_id(2)
    ne = pl.num_programs(2)
    tile = pl.program_id(0) * pl.num_programs(1) + pl.program_id(1)
    cap_p = -(-cap // LANES) * LANES
    hb = h_ref[...].reshape(rows, D_MODEL)
    comb = comb_ref[...].reshape(rows, LANES)

    @pl.when((tile == 0) & (e == 0))
    def _():
        ri = lax.broadcasted_iota(jnp.int32, (rs, rs), 0)
        ci = lax.broadcasted_iota(jnp.int32, (rs, rs), 1)
        ltri_ref[...] = (ci < ri).astype(BF16)

    @pl.when(e == 0)
    def _():
        acc_ref[...] = jnp.zeros(acc_ref.shape, F32)
        for s in range(nsub):
            sub = slice(s * rs, (s + 1) * rs)
            rank = _dot(ltri_ref[...], (comb[sub] > 0.0).astype(BF16))
            rank_col_ref[sub, :] = rank
            rank_row_ref[:, sub] = rank.T
            comb_t_ref[:, sub] = comb[sub].T

    for s in range(nsub):
        sub = slice(s * rs, (s + 1) * rs)
        count = cnt_ref[(tile * nsub + s) * N_EXPERTS + e]

        @pl.when(count > 0)
        def _():
            lane = lax.broadcasted_iota(jnp.int32, (rs, LANES), 1)
            pick = lane == e
            w_col = jnp.sum(jnp.where(pick, comb[sub], 0.0), axis=-1, keepdims=True)
            r_col = jnp.sum(jnp.where(pick, rank_col_ref[sub, :], 0.0), axis=-1, keepdims=True)
            r_col = jnp.where(w_col > 0.0, r_col, -1.0)
            w_row = comb_t_ref[pl.ds(e, 1), sub]
            r_row = jnp.where(w_row > 0.0, rank_row_ref[pl.ds(e, 1), sub], -1.0)
            w_rep = jnp.broadcast_to(w_col, (rs, LANES))
            w_hi, w_lo = _split2(w_rep)
            for c in range(-(-rs // cap)):
                @pl.when(count > c * cap)
                def _():
                    slot_r = (lax.broadcasted_iota(jnp.int32, (cap, rs), 0) + c * cap).astype(F32)
                    p_mat = (r_row == slot_r).astype(BF16)
                    xc = _dot(p_mat, hb[sub]).astype(BF16)
                    wc = (_dot(p_mat, w_hi) + _dot(p_mat, w_lo))[:, :1]
                    gate = _dot(xc, wg_ref[...])
                    up = _dot(xc, wu_ref[...])
                    act = gate * _sigmoid(gate) * up * wc
                    yc = _dot(act.astype(BF16), wd_ref[...])
                    yc = yc.astype(BF16)
                    if cap_p > cap:
                        yc = jnp.concatenate([yc, jnp.zeros((cap_p - cap, D_MODEL), BF16)], axis=0)
                    slot_c = (lax.broadcasted_iota(jnp.int32, (rs, cap_p), 1) + c * cap).astype(F32)
                    pt_mat = (r_col == slot_c).astype(BF16)
                    acc_ref[sub, :] += _dot(pt_mat, yc)

    @pl.when(e == ne - 1)
    def _():
        o_ref[...] = x_ref[...] + g2_ref[...] * acc_ref[...].reshape(tb, tt, D_MODEL)


def _moe(h2, x1, g2, comb, wg, wu, wd, tb, tt, rs):
    nb, t, _ = x1.shape
    rows = tb * tt
    cap = -(-(rs * 5 // 16) // 16) * 16
    n_sub = (nb // tb) * (t // tt) * (rows // rs)
    cnt = (comb.reshape(n_sub, rs, LANES)[:, :, :N_EXPERTS] > 0.0).sum(axis=1).astype(jnp.int32)
    row = lambda w: pl.BlockSpec((tb, tt, w), lambda b, i, e, c: (b, i, 0))
    modspec = pl.BlockSpec((tb, 1, D_MODEL), lambda b, i, e, c: (b, 0, 0))
    wspec_in = pl.BlockSpec((None, D_MODEL, D_FF_E), lambda b, i, e, c: (e, 0, 0))
    wspec_out = pl.BlockSpec((None, D_FF_E, D_MODEL), lambda b, i, e, c: (e, 0, 0))
    grid_spec = pltpu.PrefetchScalarGridSpec(
        num_scalar_prefetch=1,
        grid=(nb // tb, t // tt, N_EXPERTS),
        in_specs=[row(D_MODEL), row(D_MODEL), modspec, row(LANES), wspec_in, wspec_in, wspec_out],
        out_specs=row(D_MODEL),
        scratch_shapes=[pltpu.VMEM((rows, D_MODEL), F32),
                        pltpu.VMEM((rows, LANES), F32),
                        pltpu.VMEM((LANES, rows), F32),
                        pltpu.VMEM((LANES, rows), F32),
                        pltpu.VMEM((rs, rs), BF16)])
    return pl.pallas_call(
        functools.partial(_moe_body, cap=cap, rs=rs),
        grid_spec=grid_spec,
        out_shape=jax.ShapeDtypeStruct((nb, t, D_MODEL), F32),
        compiler_params=_cparams(("arbitrary", "arbitrary", "arbitrary")),
        name="moe",
    )(cnt.reshape(-1), h2, x1, g2, comb, wg, wu, wd)


def _prepare_weights(P):
    W = {}
    W['w_mod'] = P['w_mod'].astype(BF16)
    w_in = P['w_in'].astype(BF16)
    W['wq'] = w_in[:, :, :ATT_WIDTH]
    W['wk'] = w_in[:, :, ATT_WIDTH:2 * ATT_WIDTH]
    W['wv'] = w_in[:, :, 2 * ATT_WIDTH:3 * ATT_WIDTH]
    W['wrw'] = w_in[:, :, 3 * ATT_WIDTH:]
    depth = P['w_in'].shape[0]
    zw = jnp.zeros((depth, LORA_A, RWKV_WIDTH), BF16)
    W['w2p'] = jnp.concatenate([P['w2'].astype(BF16), zw], axis=1)
    W['a2p'] = jnp.concatenate([zw, P['a2'].astype(BF16)], axis=1)
    W['g2'] = P['g2'].astype(BF16)
    w_out = P['w_out'].astype(BF16)
    W['wo_a'] = w_out[:, :ATT_WIDTH]
    W['wo_r'] = w_out[:, ATT_WIDTH:]
    W['w_ff_gate'] = P['w_ff_gate'].astype(BF16)
    W['w_ff_up'] = P['w_ff_up'].astype(BF16)
    W['w_ff_down'] = P['w_ff_down'].astype(BF16)
    W['w_moe_gate'] = P['w_moe_gate'].astype(BF16)
    W['w_moe_up'] = P['w_moe_up'].astype(BF16)
    W['w_moe_down'] = P['w_moe_down'].astype(BF16)
    wr = jnp.pad(P['w_router'], ((0, 0), (0, 0), (0, LANES - N_EXPERTS)))
    W['router_hi'] = wr.astype(BF16)
    W['router_lo'] = (wr - W['router_hi'].astype(F32)).astype(BF16)
    seg = jnp.arange(RWKV_WIDTH) // HEAD_R
    W['eseg'] = (seg[:, None] == seg[None, :]).astype(BF16)
    return W


def _run_group(x, c, cache_k, cache_v, wkv_init, shift_init, P, W, tb, tt, scan_chunk, scan_nb, scan_nsub, bq):
    nb, t, _ = x.shape
    depth = P['w_in'].shape[0]
    eseg = W['eseg']
    prompt = cache_k is None
    ks, vs, shifts = [], [], []
    kv_stacked = ()
    wkv_stacked = None
    for l in range(depth):
        mod = _mod(c, W['w_mod'][l], P['b_mod'][l])
        sh1, sc1, g1, sh2, sc2, g2 = [mod[:, None, i * D_MODEL:(i + 1) * D_MODEL] for i in range(6)]
        qg = jnp.tile(P['q_gain'][l].reshape(1, 2 * HD_QK), (1, N_HEADS_A))
        kg = jnp.tile(P['k_gain'][l].reshape(1, 2 * HD_QK), (1, N_HEADS_A))
        proj = _in_proj(x, sc1, sh1, P['g_norm1'][l].reshape(1, D_MODEL), W['wq'][l], W['wk'][l],
                        W['wv'][l], W['wrw'][l], qg, kg, eseg, tb, tt, l, depth,
                        kv_stacked if prompt else None)
        q, rw = proj[:2]
        lam_init = 0.8 - 0.6 * math.exp(-0.3 * l)
        lam_vecs = jnp.stack([P['lambda_q1'][l], P['lambda_k1'][l], P['lambda_q2'][l], P['lambda_k2'][l]])
        subln = P['subln'][l].reshape(1, HD_V)
        if prompt:
            kv_stacked = (proj[4], proj[5])
            o = _attn_prompt(q, proj[2], proj[3], lam_vecs, subln, lam_init, bq)
        else:
            k, v = proj[2], proj[3]
            ks.append(k.reshape(nb, t, N_HEADS_A, 2, HD_QK))
            vs.append(v.reshape(nb, t, N_HEADS_A, HD_V))
            o = _attn_sample(q, cache_k, cache_v, l, k, v, lam_vecs, subln, lam_init)
        vec = lambda a: a.reshape(1, RWKV_WIDTH)
        prep = _rwkv_prep(rw, shift_init[l], P['mu'][l].reshape(1, RW_COLS), vec(P['w0'][l]), W['w2p'][l],
                          vec(P['a0'][l]), W['a2p'][l], W['g2'][l], vec(P['k_k'][l]), vec(P['k_a'][l]),
                          vec(P['r_k'][l]), eseg, tb, tt)
        y_r, wkv_stacked = _rwkv_scan(prep, wkv_init, l, wkv_stacked, vec(P['lnx_w'][l]), vec(P['lnx_b'][l]),
                                      eseg, scan_chunk, scan_nb, scan_nsub)
        routed = (l % 2 == 1)
        j = l // 2
        router = (W['router_hi'][j], W['router_lo'][j]) if routed else None
        res = _out_proj(o, y_r, x, g1, sc2, sh2, P['g_norm2'][l].reshape(1, D_MODEL), W['wo_a'][l], W['wo_r'][l],
                        router, tb, tt)
        if routed:
            x1, h2, comb = res
            x = _moe(h2, x1, g2, comb, W['w_moe_gate'][j], W['w_moe_up'][j], W['w_moe_down'][j], tb,
                     tt if tb > 1 else 2 * tt, tb * tt)
        else:
            x1, h2 = res
            x = _glu(h2, x1, g2, W['w_ff_gate'][j], W['w_ff_up'][j], W['w_ff_down'][j], tb, tt)
        shifts.append(rw[:, t - 1:, :])
    if prompt:
        kt, vr = kv_stacked
        k_out = jnp.transpose(kt.reshape(depth, nb, N_HEADS_A, 2, HD_QK, t), (0, 1, 5, 2, 3, 4))
        v_out = vr.reshape(depth, nb, t, N_HEADS_A, HD_V)
    else:
        k_out, v_out = jnp.stack(ks), jnp.stack(vs)
    return x, k_out, v_out, wkv_stacked, jnp.stack(shifts)


def kernel(x_prompt, x_sample, cache_k, cache_v, state_wkv, state_shift, c_prompt, c_sample, w_mod, b_mod, g_norm1, g_norm2, w_in, q_gain, k_gain, lambda_q1, lambda_k1, lambda_q2, lambda_k2, subln, mu, w0, w2, a0, a2, g2, k_k, k_a, r_k, lnx_w, lnx_b, w_out, w_ff_gate, w_ff_up, w_ff_down, w_router, w_moe_gate, w_moe_up, w_moe_down):
    P = dict(w_mod=w_mod, b_mod=b_mod, g_norm1=g_norm1, g_norm2=g_norm2, w_in=w_in, q_gain=q_gain,
             k_gain=k_gain, lambda_q1=lambda_q1, lambda_k1=lambda_k1, lambda_q2=lambda_q2,
             lambda_k2=lambda_k2, subln=subln, mu=mu, w0=w0, w2=w2, a0=a0, a2=a2, g2=g2, k_k=k_k,
             k_a=k_a, r_k=r_k, lnx_w=lnx_w, lnx_b=lnx_b, w_out=w_out, w_ff_gate=w_ff_gate,
             w_ff_up=w_ff_up, w_ff_down=w_ff_down, w_router=w_router, w_moe_gate=w_moe_gate,
             w_moe_up=w_moe_up, w_moe_down=w_moe_down)
    W = _prepare_weights(P)
    depth = w_in.shape[0]
    bp, tp, _ = x_prompt.shape
    bs, ts, _ = x_sample.shape
    n_past = cache_k.shape[2]
    wkv0 = jnp.zeros((depth, bp, N_HEADS_R, HEAD_R, HEAD_R), F32)
    shift0 = jnp.zeros((depth, bp, 1, RW_COLS), F32)
    y_p, k_p, v_p, wkv_p, shift_p = _run_group(x_prompt, c_prompt, None, None, wkv0, shift0, P, W,
                                               tb=1, tt=512, scan_chunk=CHUNK, scan_nb=bp, scan_nsub=4, bq=512)
    ck = jnp.transpose(cache_k, (0, 1, 3, 4, 5, 2)).reshape(depth, bs, ATT_WIDTH, n_past)
    cv = cache_v.reshape(depth, bs, n_past * N_HEADS_A, HD_V)
    y_s, k_s, v_s, wkv_s, shift_s = _run_group(x_sample, c_sample, ck, cv, state_wkv, state_shift, P, W,
                                               tb=bs, tt=ts, scan_chunk=ts, scan_nb=4, scan_nsub=1, bq=None)
    return (y_p, y_s, k_p, v_p, wkv_p, shift_p, k_s, v_s, wkv_s, shift_s)
```

```python
import functools
import math

import jax
import jax.numpy as jnp
from jax import lax
from jax.experimental import pallas as pl
from jax.experimental.pallas import tpu as pltpu

F32 = jnp.float32
BF16 = jnp.bfloat16

D_MODEL = 1024
CHUNK = 64
N_HEADS_A = 4
HD_V = 128
HD_QK = 64
ATT_WIDTH = 512
RWKV_WIDTH = 512
HEAD_R = 64
N_HEADS_R = 8
LORA_W = 64
LORA_A = 64
LORA_G = 128
RW_COLS = 3 * RWKV_WIDTH + LORA_W + LORA_A + LORA_G
D_FF = 2816
N_EXPERTS = 8
D_FF_E = 1408
NORM_EPS = 1e-6
LNX_EPS = 64e-5
NEG_BIG = -1e30
LANES = 128
VMEM_LIMIT = 56 * 1024 * 1024
SCAN_GH = 2

def _cparams(sem):
    return pltpu.CompilerParams(dimension_semantics=sem, vmem_limit_bytes=VMEM_LIMIT)


def _dot(a, b):
    return jnp.dot(a, b, preferred_element_type=F32)


def _dot_nt(a, b):
    return lax.dot_general(a, b, (((1,), (1,)), ((), ())), preferred_element_type=F32)


def _dot_tn(a, b):
    return lax.dot_general(a, b, (((0,), (0,)), ((), ())), preferred_element_type=F32)


def _split2(x):
    hi = x.astype(BF16)
    lo = (x - hi.astype(F32)).astype(BF16)
    return hi, lo


def _seg_sum(x, e, split=True):
    if not split:
        return _dot(x.astype(BF16), e)
    hi, lo = _split2(x)
    return _dot(hi, e) + _dot(lo, e)


def _sigmoid(x):
    return 1.0 / (1.0 + jnp.exp(-x))


def _mod_body(c_ref, w_ref, b_ref, o_ref):
    c = c_ref[...]
    cs = c * _sigmoid(c)
    o_ref[...] = _dot(cs.astype(BF16), w_ref[...]) + b_ref[...]


def _mod(c, w_mod, b_mod):
    nb = c.shape[0]
    n = w_mod.shape[1]
    tn = 1536
    return pl.pallas_call(
        _mod_body,
        grid=(n // tn,),
        in_specs=[pl.BlockSpec((nb, D_MODEL), lambda j: (0, 0)),
                  pl.BlockSpec((D_MODEL, tn), lambda j: (0, j)),
                  pl.BlockSpec((1, tn), lambda j: (0, j))],
        out_specs=pl.BlockSpec((nb, tn), lambda j: (0, j)),
        out_shape=jax.ShapeDtypeStruct((nb, n), F32),
        compiler_params=_cparams(("arbitrary",)),
        name="mod",
    )(c, w_mod, b_mod.reshape(1, n))


def _in_body(*refs, prompt, n_alias):
    x_ref, sc_ref, sh_ref, gn_ref, wq_ref, wk_ref, wv_ref, wr_ref, qg_ref, kg_ref, e_ref = refs[:11]
    outs = refs[11 + n_alias:]
    tb, tt, _ = x_ref.shape
    x = x_ref[...]
    ms = jnp.mean(x * x, axis=-1, keepdims=True)
    h = x * lax.rsqrt(ms + NORM_EPS) * gn_ref[...]
    h = h * (1.0 + sc_ref[...]) + sh_ref[...]
    hb = h.reshape(tb * tt, D_MODEL).astype(BF16)
    e = e_ref[...]

    def group_norm(c, gain):
        msq = _seg_sum(c * c, e, split=False) * (1.0 / HD_QK)
        return c * lax.rsqrt(msq + NORM_EPS) * gain

    q = group_norm(_dot(hb, wq_ref[...]), qg_ref[...])
    k = group_norm(_dot(hb, wk_ref[...]), kg_ref[...])
    v = _dot(hb, wv_ref[...])
    rw = _dot(hb, wr_ref[...]).reshape(tb, tt, RW_COLS)
    if prompt:
        q_ref, rw_ref, kb_ref, vt_ref, kt_ref, vr_ref = outs
        kb_ref[...] = k.reshape(tb, tt, ATT_WIDTH).astype(BF16)
        vt_ref[0] = v.T.astype(BF16)
        kt_ref[0] = k.T
        for hd in range(N_HEADS_A):
            vr_ref[0, pl.ds(hd, tt, stride=N_HEADS_A), :] = v[:, hd * HD_V:(hd + 1) * HD_V]
    else:
        q_ref, rw_ref, k_ref, v_ref = outs
        k_ref[...] = k.reshape(tb, tt, ATT_WIDTH)
        v_ref[...] = v.reshape(tb, tt, ATT_WIDTH)
    q_ref[...] = q.reshape(tb, tt, ATT_WIDTH).astype(BF16)
    rw_ref[...] = rw


def _in_proj(x, sc, sh, gn, wq, wk, wv, wr, qg, kg, eseg, tb, tt, layer, depth, stacked):
    nb, t, _ = x.shape
    prompt = stacked is not None
    row = lambda w: pl.BlockSpec((tb, tt, w), lambda b, i: (b, i, 0))
    modspec = pl.BlockSpec((tb, 1, D_MODEL), lambda b, i: (b, 0, 0))
    full = lambda a: pl.BlockSpec(a.shape, lambda b, i: (0,) * a.ndim)
    ins = [x, sc, sh, gn, wq, wk, wv, wr, qg, kg, eseg]
    in_specs = [row(D_MODEL), modspec, modspec, full(gn), full(wq), full(wk), full(wv), full(wr),
                full(qg), full(kg), full(eseg)]
    out_specs = [row(ATT_WIDTH), row(RW_COLS)]
    out_shape = [jax.ShapeDtypeStruct((nb, t, ATT_WIDTH), BF16), jax.ShapeDtypeStruct((nb, t, RW_COLS), F32)]
    aliases = {}
    if prompt:
        assert tb == 1
        out_specs += [row(ATT_WIDTH), pl.BlockSpec((1, None, ATT_WIDTH, tt), lambda b, i: (b, i, 0, 0)),
                      pl.BlockSpec((None, 1, ATT_WIDTH, tt), lambda b, i: (layer, b, 0, i)),
                      pl.BlockSpec((None, 1, N_HEADS_A * tt, HD_V), lambda b, i: (layer, b, i, 0))]
        out_shape += [jax.ShapeDtypeStruct((nb, t, ATT_WIDTH), BF16),
                      jax.ShapeDtypeStruct((nb, t // tt, ATT_WIDTH, tt), BF16),
                      jax.ShapeDtypeStruct((depth, nb, ATT_WIDTH, t), F32),
                      jax.ShapeDtypeStruct((depth, nb, N_HEADS_A * t, HD_V), F32)]
        for n, prev in enumerate(stacked):
            aliases[len(ins)] = 4 + n
            ins.append(prev)
            in_specs.append(pl.BlockSpec(memory_space=pl.ANY))
    else:
        out_specs += [row(ATT_WIDTH), row(ATT_WIDTH)]
        out_shape += [jax.ShapeDtypeStruct((nb, t, ATT_WIDTH), F32)] * 2
    return pl.pallas_call(
        functools.partial(_in_body, prompt=prompt, n_alias=len(aliases)),
        grid=(nb // tb, t // tt),
        in_specs=in_specs,
        out_specs=out_specs,
        out_shape=out_shape,
        input_output_aliases=aliases,
        compiler_params=_cparams(("arbitrary", "arbitrary")),
        name="in_proj",
    )(*ins)


def _alibi_slope(h):
    return 2.0 ** (-8.0 * (h + 1) / N_HEADS_A)


def _lambda_value(lam_ref, lam_init):
    lv = lam_ref[...]
    s1 = jnp.sum(lv[0:1] * lv[1:2], axis=-1, keepdims=True)
    s2 = jnp.sum(lv[2:3] * lv[3:4], axis=-1, keepdims=True)
    return jnp.exp(s1) - jnp.exp(s2) + lam_init


def _stack_maps(qh):
    lane = lax.broadcasted_iota(jnp.int32, qh.shape, 1)
    qs = qh * (HD_QK ** -0.5)
    zero = jnp.zeros_like(qs)
    return jnp.concatenate([jnp.where(lane < HD_QK, qs, zero), jnp.where(lane >= HD_QK, qs, zero)],
                           axis=0)


def _sub_norm(o, gain, lam_init):
    ms = jnp.mean(o * o, axis=-1, keepdims=True)
    return o * lax.rsqrt(ms + NORM_EPS) * gain * (1.0 - lam_init)


def _attn_body(it_ref, jt_ref, q_ref, k_ref, vt_ref, lam_ref, sub_ref, o_ref, q2_ref, m_ref, l_ref,
               acc_ref, s_ref, p_ref, *, bq, lam_init):
    step_id = pl.program_id(1)
    i = it_ref[step_id]
    j = jt_ref[step_id]
    bk = bq
    nq2 = 2 * bq
    kc = 32
    lane = lax.broadcasted_iota(jnp.int32, (bq, LANES), 1)

    @pl.when(j == 0)
    def _():
        m_ref[...] = jnp.full(m_ref.shape, NEG_BIG, F32)
        l_ref[...] = jnp.zeros(l_ref.shape, F32)
        acc_ref[...] = jnp.zeros(acc_ref.shape, F32)
        lane2 = lax.broadcasted_iota(jnp.int32, (nq2, LANES), 1)
        for h in range(N_HEADS_A):
            slope = _alibi_slope(h)
            feat = jnp.where(lane2 == 0, slope * CHUNK, jnp.where(lane2 == 1, slope, 0.0)).astype(BF16)
            q2_ref[h] = jnp.concatenate([_stack_maps(q_ref[0, :, h * HD_V:(h + 1) * HD_V]), feat], axis=1)

    def step(diag):
        krel = lax.broadcasted_iota(jnp.int32, (bk, LANES), 0) + (j - i) * bq
        kfeat = jnp.where(lane == 0, krel >> 6, jnp.where(lane == 1, krel & (CHUNK - 1), 0))
        kfeat = kfeat.astype(F32).astype(BF16)
        subs = range(bk // kc)
        npair = nq2 // (2 * LANES)

        def qk(h, g):
            kh = jnp.concatenate([k_ref[0, :, h * HD_V:(h + 1) * HD_V], kfeat], axis=1)
            st2 = _dot_nt(kh, q2_ref[h, g * 2 * LANES:(g + 1) * 2 * LANES, :])
            if diag:
                c = lax.broadcasted_iota(jnp.int32, (bk, 2 * LANES), 0)
                r = lax.broadcasted_iota(jnp.int32, (bk, 2 * LANES), 1) + (g * 2 * LANES) % bq
                ahead = jnp.maximum(c - r, 0).astype(F32) * (-2.0 * _alibi_slope(h))
                st2 = jnp.where((c >> 6) <= (r >> 6), st2 + ahead, NEG_BIG)
            s_ref[h % 2, 2 * g] = st2[:, :LANES]
            s_ref[h % 2, 2 * g + 1] = st2[:, LANES:]
            return jnp.max(st2, axis=0, keepdims=True)

        def softmax(h, g, m_blk):
            alpha = []
            for n, st in enumerate((2 * g, 2 * g + 1)):
                sl = slice(st * LANES, (st + 1) * LANES)
                m_old = m_ref[h, :, sl]
                m_new = jnp.maximum(m_old, m_blk[:, n * LANES:(n + 1) * LANES])
                a = jnp.exp(m_old - m_new)
                psum = None
                for kb in subs:
                    p = jnp.exp(s_ref[h % 2, st, kb * kc:(kb + 1) * kc, :] - m_new)
                    psum = p if psum is None else psum + p
                    p_ref[st, kb * kc:(kb + 1) * kc, :] = p.astype(BF16)
                l_ref[h, :, sl] = a * l_ref[h, :, sl] + jnp.sum(psum, axis=0, keepdims=True)
                m_ref[h, :, sl] = m_new
                alpha.append(a)
            return jnp.concatenate(alpha, axis=1)

        def pv(h, g, a_row):
            p2 = jnp.concatenate([p_ref[2 * g], p_ref[2 * g + 1]], axis=1)
            sl = slice(g * 2 * LANES, (g + 1) * 2 * LANES)
            acc_ref[h, :, sl] = acc_ref[h, :, sl] * a_row + _dot(vt_ref[0, h * HD_V:(h + 1) * HD_V, :], p2)

        m_blk = {(0, g): qk(0, g) for g in range(npair)}
        for h in range(N_HEADS_A):
            for g in range(npair):
                if h + 1 < N_HEADS_A:
                    m_blk[(h + 1, g)] = qk(h + 1, g)
                a_row = softmax(h, g, m_blk.pop((h, g)))
                pv(h, g, a_row)

    @pl.when(j < i)
    def _():
        step(False)

    @pl.when(j == i)
    def _():
        step(True)
        lam = _lambda_value(lam_ref, lam_init)
        for h in range(N_HEADS_A):
            o1 = acc_ref[h, :, :bq] / l_ref[h, :, :bq]
            o2 = acc_ref[h, :, bq:] / l_ref[h, :, bq:]
            o = (o1 - lam * o2).T
            o_ref[0, :, h * HD_V:(h + 1) * HD_V] = _sub_norm(o, sub_ref[...], lam_init)


def _attn_prompt(q, kb, vt, lam_vecs, subln, lam_init, bq):
    nb, t, _ = q.shape
    nq = t // bq
    pairs = [(i, j) for i in range(nq) for j in range(i + 1)]
    i_tab = jnp.asarray([p[0] for p in pairs], jnp.int32)
    j_tab = jnp.asarray([p[1] for p in pairs], jnp.int32)
    qspec = pl.BlockSpec((1, bq, ATT_WIDTH), lambda b, s, it, jt: (b, it[s], 0))
    kspec = pl.BlockSpec((1, bq, ATT_WIDTH), lambda b, s, it, jt: (b, jt[s], 0))
    assert vt.shape == (nb, nq, ATT_WIDTH, bq)
    vspec = pl.BlockSpec((1, None, ATT_WIDTH, bq), lambda b, s, it, jt: (b, jt[s], 0, 0))
    grid_spec = pltpu.PrefetchScalarGridSpec(
        num_scalar_prefetch=2,
        grid=(nb, len(pairs)),
        in_specs=[qspec, kspec, vspec,
                  pl.BlockSpec(lam_vecs.shape, lambda b, s, it, jt: (0, 0)),
                  pl.BlockSpec(subln.shape, lambda b, s, it, jt: (0, 0))],
        out_specs=qspec,
        scratch_shapes=[pltpu.VMEM((N_HEADS_A, 2 * bq, 2 * HD_V), BF16),
                        pltpu.VMEM((N_HEADS_A, 1, 2 * bq), F32),
                        pltpu.VMEM((N_HEADS_A, 1, 2 * bq), F32),
                        pltpu.VMEM((N_HEADS_A, HD_V, 2 * bq), F32),
                        pltpu.VMEM((2, 2 * bq // LANES, bq, LANES), F32),
                        pltpu.VMEM((2 * bq // LANES, bq, LANES), BF16)])
    return pl.pallas_call(
        functools.partial(_attn_body, bq=bq, lam_init=lam_init),
        grid_spec=grid_spec,
        out_shape=jax.ShapeDtypeStruct((nb, t, ATT_WIDTH), F32),
        compiler_params=_cparams(("arbitrary", "arbitrary")),
        name="attn_prompt",
    )(i_tab, j_tab, q, kb, vt, lam_vecs, subln)


def _attn_dec_body(q_ref, ck_ref, cv_ref, kn_ref, vn_ref, lam_ref, sub_ref, o_ref, *, lam_init):
    tq = q_ref.shape[1]
    n_past = ck_ref.shape[1]
    lam = _lambda_value(lam_ref, lam_init)
    r = lax.broadcasted_iota(jnp.int32, (tq, n_past), 0)
    c = lax.broadcasted_iota(jnp.int32, (tq, n_past), 1)
    dist_c = jnp.abs((r + n_past - c).astype(F32))
    rn = lax.broadcasted_iota(jnp.int32, (tq, tq), 0)
    cn = lax.broadcasted_iota(jnp.int32, (tq, tq), 1)
    dist_n = jnp.abs((rn - cn).astype(F32))
    for h in range(N_HEADS_A):
        sl = slice(h * HD_V, (h + 1) * HD_V)
        q2 = _stack_maps(q_ref[0, :, sl])
        vc = cv_ref[pl.ds(h, n_past, stride=N_HEADS_A), :].astype(BF16)
        kn = kn_ref[0, :, sl].astype(BF16)
        vn = vn_ref[0, :, sl].astype(BF16)
        sn_all = _dot_nt(q2, kn)
        slope = -_alibi_slope(h)
        outs = []
        sc_all = _dot(q2, ck_ref[sl, :].astype(BF16))
        for mp in range(2):
            s_c = sc_all[mp * tq:(mp + 1) * tq] + dist_c * slope
            s_n = sn_all[mp * tq:(mp + 1) * tq] + dist_n * slope
            m = jnp.maximum(jnp.max(s_c, axis=-1, keepdims=True), jnp.max(s_n, axis=-1, keepdims=True))
            p_c = jnp.exp(s_c - m)
            p_n = jnp.exp(s_n - m)
            l = jnp.sum(p_c, axis=-1, keepdims=True) + jnp.sum(p_n, axis=-1, keepdims=True)
            acc = _dot(p_c.astype(BF16), vc) + _dot(p_n.astype(BF16), vn)
            outs.append(acc / l)
        o = outs[0] - lam * outs[1]
        o_ref[0, :, sl] = _sub_norm(o, sub_ref[...], lam_init)


def _attn_sample(q, cache_k, cache_v, layer, k_new, v_new, lam_vecs, subln, lam_init):
    nb, tq, _ = q.shape
    n_past = cache_k.shape[3]
    row = pl.BlockSpec((1, tq, ATT_WIDTH), lambda b: (b, 0, 0))
    kspec = pl.BlockSpec((None, None, ATT_WIDTH, n_past), lambda b: (layer, b, 0, 0))
    vspec = pl.BlockSpec((None, None, n_past * N_HEADS_A, HD_V), lambda b: (layer, b, 0, 0))
    return pl.pallas_call(
        functools.partial(_attn_dec_body, lam_init=lam_init),
        grid=(nb,),
        in_specs=[row, kspec, vspec, row, row,
                  pl.BlockSpec(lam_vecs.shape, lambda b: (0, 0)),
                  pl.BlockSpec(subln.shape, lambda b: (0, 0))],
        out_specs=row,
        out_shape=jax.ShapeDtypeStruct((nb, tq, ATT_WIDTH), F32),
        compiler_params=_cparams(("arbitrary",)),
        name="attn_sample",
    )(q, cache_k, cache_v, k_new, v_new, lam_vecs, subln)


def _prep_body(rw_ref, shift_ref, mu_ref, w0_ref, w2_ref, a0_ref, a2_ref, g2_ref, kk_ref, ka_ref,
               rk_ref, e_ref, r_o, k_o, v_o, kk_o, bh_o, lw_o, g_o, bon_o, carry_ref):
    tb, tt, w = rw_ref.shape
    i = pl.program_id(1)
    rows = tb * tt
    cols = rw_ref[...].reshape(rows, w)
    rolled = pltpu.roll(cols, 1, 0)
    rowi = lax.broadcasted_iota(jnp.int32, (rows, 1), 0)
    if tb == 1:
        @pl.when(i == 0)
        def _():
            carry_ref[...] = shift_ref[0]

        prev = jnp.where(rowi == 0, carry_ref[...], rolled)
        carry_ref[...] = cols[tt - 1:tt, :]
    else:
        sh = jnp.broadcast_to(shift_ref[...], (tb, tt, w)).reshape(rows, w)
        prev = jnp.where((rowi % tt) == 0, sh, rolled)
    xs = cols + (prev - cols) * mu_ref[...]
    o1, o2, o3 = RWKV_WIDTH, 2 * RWKV_WIDTH, 3 * RWKV_WIDTH
    r = xs[:, :o1]
    k = xs[:, o1:o2]
    v = xs[:, o2:o3]
    da = xs[:, o3:o3 + LORA_W + LORA_A]
    gd = xs[:, o3 + LORA_W + LORA_A:]
    e = e_ref[...]
    z = w0_ref[...] + _dot(jnp.tanh(da).astype(BF16), w2_ref[...])
    lw_o[...] = (-math.exp(-0.5) * _sigmoid(z)).reshape(tb, tt, o1)
    a = _sigmoid(a0_ref[...] + _dot(da.astype(BF16), a2_ref[...]))
    g_o[...] = _dot(_sigmoid(gd).astype(BF16), g2_ref[...]).reshape(tb, tt, o1)
    kk = k * kk_ref[...]
    nrm = jnp.sqrt(_seg_sum(kk * kk, e))
    kk = kk / jnp.maximum(nrm, 1e-12)
    kp = k * (1.0 + (a - 1.0) * ka_ref[...])
    bon = _seg_sum(r * kp * rk_ref[...], e, split=False) * v
    r_o[...] = r.reshape(tb, tt, o1)
    k_o[...] = kp.reshape(tb, tt, o1)
    v_o[...] = v.reshape(tb, tt, o1)
    kk_o[...] = kk.reshape(tb, tt, o1)
    bh_o[...] = (kk * a).reshape(tb, tt, o1)
    bon_o[...] = bon.reshape(tb, tt, o1)


def _rwkv_prep(rw, shift, mu, w0, w2p, a0, a2p, g2, k_k, k_a, r_k, eseg, tb, tt):
    nb, t, _ = rw.shape
    row = lambda w: pl.BlockSpec((tb, tt, w), lambda b, i: (b, i, 0))
    full = lambda a: pl.BlockSpec(a.shape, lambda b, i: (0,) * a.ndim)
    out = jax.ShapeDtypeStruct((nb, t, RWKV_WIDTH), F32)
    return pl.pallas_call(
        _prep_body,
        grid=(nb // tb, t // tt),
        in_specs=[row(RW_COLS), pl.BlockSpec((tb, 1, RW_COLS), lambda b, i: (b, 0, 0)), full(mu), full(w0),
                  full(w2p), full(a0), full(a2p), full(g2), full(k_k), full(k_a), full(r_k), full(eseg)],
        out_specs=[row(RWKV_WIDTH)] * 8,
        out_shape=[out] * 8,
        scratch_shapes=[pltpu.VMEM((1, RW_COLS), F32)],
        compiler_params=_cparams(("arbitrary", "arbitrary")),
        name="rwkv_prep",
    )(rw, shift, mu, w0, w2p, a0, a2p, g2, k_k, k_a, r_k, eseg)


def _scan_body(*refs, C, nsub):
    r_ref, k_ref, v_ref, kk_ref, bh_ref, lw_ref, g_ref, bon_ref, s0_ref, lnw_ref, lnb_ref, e_ref = refs[:12]
    y_ref, sout_ref, st_ref = refs[-3:]
    ci = pl.program_id(1)
    nc = pl.num_programs(1)

    nbk = lw_ref.shape[0]

    @pl.when(ci == 0)
    def _():
        z = jnp.zeros((HEAD_R, HEAD_R), F32)
        for b in range(nbk):
            for q in range(N_HEADS_R // SCAN_GH):
                blocks = [jnp.concatenate([s0_ref[b, SCAN_GH * q + hh] if hc == hh else z
                                           for hc in range(SCAN_GH)], axis=1) for hh in range(SCAN_GH)]
                st_ref[b, q] = jnp.concatenate(blocks, axis=0)

    ti = lax.broadcasted_iota(jnp.int32, (C, C), 0)
    si = lax.broadcasted_iota(jnp.int32, (C, C), 1)
    ltri = (si <= ti).astype(BF16)
    segs = [(b, u) for u in range(nsub) for b in range(nbk)]
    rows_of = lambda u: slice(u * C, (u + 1) * C)
    at, rt, bt, kt, bb, kb, g_c, vb = [], [], [], [], [], [], [], []
    for (b, u) in segs:
        rw = rows_of(u)
        lw = lw_ref[b, rw, :]
        hi = lw.astype(BF16)
        r1 = lw - hi.astype(F32)
        mid = r1.astype(BF16)
        lo = (r1 - mid.astype(F32)).astype(BF16)
        cs = _dot(ltri, hi) + _dot(ltri, mid) + _dot(ltri, lo)
        cend = cs[C - 1:C, :]
        kk = kk_ref[b, rw, :]
        bh = bh_ref[b, rw, :]
        kp = k_ref[b, rw, :]
        e_neg = jnp.exp(-cs)
        e_end = jnp.exp(cend - cs)
        at.append((-kk * jnp.exp(cs - lw)).astype(BF16))
        rt.append((r_ref[b, rw, :] * jnp.exp(cs)).astype(BF16))
        bt.append((bh * e_neg).astype(BF16))
        kt.append((kp * e_neg).astype(BF16))
        bb.append((bh * e_end).astype(BF16))
        kb.append((kp * e_end).astype(BF16))
        g_c.append(jnp.exp(cend))
        vb.append(v_ref[b, rw, :].astype(BF16))

    C2 = SCAN_GH * C
    gw = SCAN_GH * HEAD_R
    log_c = C.bit_length() - 1
    log_h = HEAD_R.bit_length() - 1
    hm = ((lax.broadcasted_iota(jnp.int32, (C2, gw), 0) >> log_c)
          == (lax.broadcasted_iota(jnp.int32, (C2, gw), 1) >> log_h))
    hm2 = ((lax.broadcasted_iota(jnp.int32, (C2, C2), 0) >> log_c)
           == (lax.broadcasted_iota(jnp.int32, (C2, C2), 1) >> log_c))
    hm_state = ((lax.broadcasted_iota(jnp.int32, (gw, gw), 0) >> log_h)
                == (lax.broadcasted_iota(jnp.int32, (gw, gw), 1) >> log_h))
    tt = lax.broadcasted_iota(jnp.int32, (C, C2), 0)
    ss = lax.broadcasted_iota(jnp.int32, (C, C2), 1) & (C - 1)
    strict = ss < tt
    incl = ss <= tt
    eye = (ss == tt).astype(F32)

    def bd(x, mask):
        xx = jnp.concatenate([x] * SCAN_GH, axis=0)
        return jnp.where(mask, xx, jnp.zeros_like(xx)).astype(BF16)

    def mm(x, y):
        return _dot(x.astype(BF16), bd(y, hm2))

    npair = N_HEADS_R // SCAN_GH
    chains = [(sg, q) for sg in range(len(segs)) for q in range(npair)]
    pairs = range(len(chains))
    sls = [slice(q * gw, (q + 1) * gw) for (_, q) in chains]
    zero = jnp.zeros((C, C2), F32)
    a_l = [at[sg][:, sls[p]] for p, (sg, _) in enumerate(chains)]
    r_l = [rt[sg][:, sls[p]] for p, (sg, _) in enumerate(chains)]
    v_l = [vb[sg][:, sls[p]] for p, (sg, _) in enumerate(chains)]
    ar_l = [jnp.concatenate([a_l[p], r_l[p]], axis=0) for p in pairs]
    gb_l = [_dot_nt(ar_l[p], bd(bt[chains[p][0]][:, sls[p]], hm)) for p in pairs]
    gk_l = [_dot_nt(ar_l[p], bd(kt[chains[p][0]][:, sls[p]], hm)) for p in pairs]
    n_l = [jnp.where(strict, gb_l[p][:C], zero) for p in pairs]
    aks_l = [jnp.where(strict, gk_l[p][:C], zero).astype(BF16) for p in pairs]
    rbi_l = [jnp.where(incl, gb_l[p][C:], zero).astype(BF16) for p in pairs]
    rki_l = [jnp.where(incl, gk_l[p][C:], zero).astype(BF16) for p in pairs]
    nd_l = [jnp.where((ss >> 3) == (tt >> 3), n_l[p], zero) for p in pairs]
    nd2_l = [mm(nd_l[p], nd_l[p]) for p in pairs]
    nd4_l = [mm(nd2_l[p], nd2_l[p]) for p in pairs]
    t_l = [mm(eye + nd_l[p], eye + nd2_l[p]) for p in pairs]
    t_l = [mm(t_l[p], eye + nd4_l[p]) for p in pairs]
    lvl = 3
    while (1 << lvl) < C:
        tb_ = tt >> lvl
        ml = ((tb_ & 1) == 1) & ((ss >> lvl) == tb_ - 1)
        tn_l = [mm(t_l[p], jnp.where(ml, n_l[p], zero)) for p in pairs]
        t_l = [t_l[p] + mm(tn_l[p], t_l[p]) for p in pairs]
        lvl += 1
    vbd_l = [bd(v_l[p], hm) for p in pairs]
    t_l = [t.astype(BF16) for t in t_l]
    ys = [None] * len(chains)
    for u in range(nsub):
        cur = [p for p in pairs if segs[chains[p][0]][1] == u]
        s_l = {p: st_ref[segs[chains[p][0]][0], chains[p][1]] for p in cur}
        sb_l = {p: s_l[p].astype(BF16) for p in cur}
        x_l = {p: _dot_nt(a_l[p], sb_l[p]) + _dot(aks_l[p], vbd_l[p]) for p in cur}
        u_l = {p: _dot(t_l[p], bd(x_l[p], hm)) for p in cur}
        for p in cur:
            ys[p] = _dot_nt(r_l[p], sb_l[p]) + _dot(rbi_l[p], bd(u_l[p], hm)) + _dot(rki_l[p], vbd_l[p])
        for p in cur:
            sg, q = chains[p]
            uv = jnp.concatenate([u_l[p].astype(BF16), v_l[p]], axis=0)
            bk = jnp.concatenate([bb[sg][:, sls[p]], kb[sg][:, sls[p]]], axis=0)
            s_add = _dot_tn(uv, bk)
            st_ref[segs[sg][0], q] = (s_l[p] * g_c[sg][:, sls[p]]
                                      + jnp.where(hm_state, s_add, jnp.zeros_like(s_add)))

    e = e_ref[...]
    ycat = [jnp.concatenate(ys[sg * npair:(sg + 1) * npair], axis=1) for sg in range(len(segs))]
    mean = [_seg_sum(y, e, split=False) * (1.0 / HEAD_R) for y in ycat]
    d = [ycat[sg] - mean[sg] for sg in range(len(segs))]
    var = [_seg_sum(x * x, e, split=False) * (1.0 / HEAD_R) for x in d]
    for sg, (b, u) in enumerate(segs):
        yn = d[sg] * lax.rsqrt(var[sg] + LNX_EPS) * lnw_ref[...] + lnb_ref[...]
        y_ref[b, rows_of(u), :] = (yn + bon_ref[b, rows_of(u), :]) * g_ref[b, rows_of(u), :]

    @pl.when(ci == nc - 1)
    def _():
        for b in range(nbk):
            for q in range(N_HEADS_R // SCAN_GH):
                s = st_ref[b, q]
                for hh in range(SCAN_GH):
                    blk = slice(hh * HEAD_R, (hh + 1) * HEAD_R)
                    sout_ref[b, SCAN_GH * q + hh] = s[blk, blk]


def _rwkv_scan(prep, wkv_init, layer, wkv_prev, lnx_w, lnx_b, eseg, C, nbk, nsub):
    nb, t, _ = prep[0].shape
    row = pl.BlockSpec((nbk, C * nsub, RWKV_WIDTH), lambda b, c: (b, c, 0))
    sspec = pl.BlockSpec((None, nbk, N_HEADS_R, HEAD_R, HEAD_R), lambda b, c: (layer, b, 0, 0, 0))
    full = lambda a: pl.BlockSpec(a.shape, lambda b, c: (0,) * a.ndim)
    ins = list(prep) + [wkv_init, lnx_w, lnx_b, eseg]
    in_specs = [row] * 8 + [sspec, full(lnx_w), full(lnx_b), full(eseg)]
    aliases = {}
    if wkv_prev is not None:
        aliases[len(ins)] = 1
        ins.append(wkv_prev)
        in_specs.append(pl.BlockSpec(memory_space=pl.ANY))
    return pl.pallas_call(
        functools.partial(_scan_body, C=C, nsub=nsub),
        grid=(nb // nbk, t // (C * nsub)),
        in_specs=in_specs,
        out_specs=[row, sspec],
        out_shape=[jax.ShapeDtypeStruct((nb, t, RWKV_WIDTH), F32),
                   jax.ShapeDtypeStruct(wkv_init.shape, F32)],
        scratch_shapes=[pltpu.VMEM((nbk, N_HEADS_R // SCAN_GH, SCAN_GH * HEAD_R, SCAN_GH * HEAD_R), F32)],
        input_output_aliases=aliases,
        compiler_params=_cparams(("arbitrary", "arbitrary")),
        name="rwkv_scan",
    )(*ins)


def _out_body(*refs, routed):
    if routed:
        (o_ref, y_ref, x_ref, g1_ref, sc_ref, sh_ref, gn_ref, wa_ref, wr_ref, rh_ref, rl_ref,
         x1_ref, h2_ref, comb_ref) = refs
    else:
        (o_ref, y_ref, x_ref, g1_ref, sc_ref, sh_ref, gn_ref, wa_ref, wr_ref, x1_ref, h2_ref) = refs
    tb, tt, _ = x_ref.shape
    rows = tb * tt
    ob = o_ref[...].reshape(rows, ATT_WIDTH).astype(BF16)
    yb = y_ref[...].reshape(rows, RWKV_WIDTH).astype(BF16)
    mix = _dot(ob, wa_ref[...]) + _dot(yb, wr_ref[...])
    x1 = x_ref[...] + g1_ref[...] * mix.reshape(tb, tt, D_MODEL)
    x1_ref[...] = x1
    ms = jnp.mean(x1 * x1, axis=-1, keepdims=True)
    h2 = x1 * lax.rsqrt(ms + NORM_EPS) * gn_ref[...]
    h2 = h2 * (1.0 + sc_ref[...]) + sh_ref[...]
    h2_ref[...] = h2.astype(BF16)
    if routed:
        hf = h2.reshape(rows, D_MODEL)
        hi, lo = _split2(hf)
        logits = _dot(hi, rh_ref[...]) + _dot(hi, rl_ref[...]) + _dot(lo, rh_ref[...])
        lane = lax.broadcasted_iota(jnp.int32, logits.shape, 1)
        logits = jnp.where(lane < N_EXPERTS, logits, NEG_BIG)
        m1 = jnp.max(logits, axis=-1, keepdims=True)
        i1 = jnp.min(jnp.where(logits == m1, lane, LANES), axis=-1, keepdims=True)
        rest = jnp.where(lane == i1, NEG_BIG, logits)
        m2 = jnp.max(rest, axis=-1, keepdims=True)
        i2 = jnp.min(jnp.where(rest == m2, lane, LANES), axis=-1, keepdims=True)
        ex = jnp.exp(m2 - m1)
        gate1 = 1.0 / (1.0 + ex)
        gate2 = ex / (1.0 + ex)
        comb = jnp.where(lane == i1, gate1, 0.0) + jnp.where(lane == i2, gate2, 0.0)
        comb_ref[...] = comb.reshape(tb, tt, LANES)


def _out_proj(o, y, x, g1, sc2, sh2, gn2, wa, wr, router, tb, tt):
    nb, t, _ = x.shape
    routed = router is not None
    row = lambda w: pl.BlockSpec((tb, tt, w), lambda b, i: (b, i, 0))
    modspec = pl.BlockSpec((tb, 1, D_MODEL), lambda b, i: (b, 0, 0))
    full = lambda a: pl.BlockSpec(a.shape, lambda b, i: (0,) * a.ndim)
    ins = [o, y, x, g1, sc2, sh2, gn2, wa, wr]
    in_specs = [row(ATT_WIDTH), row(RWKV_WIDTH), row(D_MODEL), modspec, modspec, modspec, full(gn2),
                full(wa), full(wr)]
    out_specs = [row(D_MODEL), row(D_MODEL)]
    out_shape = [jax.ShapeDtypeStruct((nb, t, D_MODEL), F32), jax.ShapeDtypeStruct((nb, t, D_MODEL), BF16)]
    if routed:
        ins += list(router)
        in_specs += [full(router[0]), full(router[1])]
        out_specs.append(row(LANES))
        out_shape.append(jax.ShapeDtypeStruct((nb, t, LANES), F32))
    return pl.pallas_call(
        functools.partial(_out_body, routed=routed),
        grid=(nb // tb, t // tt),
        in_specs=in_specs,
        out_specs=out_specs,
        out_shape=out_shape,
        compiler_params=_cparams(("arbitrary", "arbitrary")),
        name="out_proj_routed" if routed else "out_proj",
    )(*ins)


def _glu_body(h_ref, x_ref, g2_ref, wg_ref, wu_ref, wd_ref, o_ref, acc_ref):
    tb, tt, _ = x_ref.shape
    rows = tb * tt
    e = pl.program_id(2)
    ne = pl.num_programs(2)

    @pl.when(e == 0)
    def _():
        acc_ref[...] = jnp.zeros(acc_ref.shape, F32)

    hb = h_ref[...].reshape(rows, D_MODEL)
    gate = _dot(hb, wg_ref[...])
    up = _dot(hb, wu_ref[...])
    act = gate * _sigmoid(gate) * up
    acc_ref[...] += _dot(act.astype(BF16), wd_ref[...])

    @pl.when(e == ne - 1)
    def _():
        o_ref[...] = x_ref[...] + g2_ref[...] * acc_ref[...].reshape(tb, tt, D_MODEL)


def _glu(h2, x1, g2, wg, wu, wd, tb, tt):
    nb, t, _ = x1.shape
    row = lambda w: pl.BlockSpec((tb, tt, w), lambda b, i, e: (b, i, 0))
    modspec = pl.BlockSpec((tb, 1, D_MODEL), lambda b, i, e: (b, 0, 0))
    wspec_in = pl.BlockSpec((D_MODEL, D_FF_E), lambda b, i, e: (0, e))
    wspec_out = pl.BlockSpec((D_FF_E, D_MODEL), lambda b, i, e: (e, 0))
    return pl.pallas_call(
        _glu_body,
        grid=(nb // tb, t // tt, D_FF // D_FF_E),
        in_specs=[row(D_MODEL), row(D_MODEL), modspec, wspec_in, wspec_in, wspec_out],
        out_specs=row(D_MODEL),
        out_shape=jax.ShapeDtypeStruct((nb, t, D_MODEL), F32),
        scratch_shapes=[pltpu.VMEM((tb * tt, D_MODEL), F32)],
        compiler_params=_cparams(("arbitrary", "arbitrary", "arbitrary")),
        name="glu_dense",
    )(h2, x1, g2, wg, wu, wd)


def _moe_body(cnt_ref, h_ref, x_ref, g2_ref, comb_ref, wg_ref, wu_ref, wd_ref,
              o_ref, acc_ref, rank_col_ref, rank_row_ref, comb_t_ref, ltri_ref, *, cap, rs):
    tb, tt, _ = x_ref.shape
    rows = tb * tt
    nsub = rows // rs
    e = pl.program_id(2)
    ne = pl.num_programs(2)
    tile = pl.program_id(0) * pl.num_programs(1) + pl.program_id(1)
    cap_p = -(-cap // LANES) * LANES
    hb = h_ref[...].reshape(rows, D_MODEL)
    comb = comb_ref[...].reshape(rows, LANES)

    @pl.when((tile == 0) & (e == 0))
    def _():
        ri = lax.broadcasted_iota(jnp.int32, (rs, rs), 0)
        ci = lax.broadcasted_iota(jnp.int32, (rs, rs), 1)
        ltri_ref[...] = (ci < ri).astype(BF16)

    @pl.when(e == 0)
    def _():
        acc_ref[...] = jnp.zeros(acc_ref.shape, F32)
        for s in range(nsub):
            sub = slice(s * rs, (s + 1) * rs)
            rank = _dot(ltri_ref[...], (comb[sub] > 0.0).astype(BF16))
            rank_col_ref[sub, :] = rank
            rank_row_ref[:, sub] = rank.T
            comb_t_ref[:, sub] = comb[sub].T

    for s in range(nsub):
        sub = slice(s * rs, (s + 1) * rs)
        count = cnt_ref[(tile * nsub + s) * N_EXPERTS + e]

        @pl.when(count > 0)
        def _():
            lane = lax.broadcasted_iota(jnp.int32, (rs, LANES), 1)
            pick = lane == e
            w_col = jnp.sum(jnp.where(pick, comb[sub], 0.0), axis=-1, keepdims=True)
            r_col = jnp.sum(jnp.where(pick, rank_col_ref[sub, :], 0.0), axis=-1, keepdims=True)
            r_col = jnp.where(w_col > 0.0, r_col, -1.0)
            w_row = comb_t_ref[pl.ds(e, 1), sub]
            r_row = jnp.where(w_row > 0.0, rank_row_ref[pl.ds(e, 1), sub], -1.0)
            w_rep = jnp.broadcast_to(w_col, (rs, LANES))
            w_hi, w_lo = _split2(w_rep)
            for c in range(-(-rs // cap)):
                @pl.when(count > c * cap)
                def _():
                    slot_r = (lax.broadcasted_iota(jnp.int32, (cap, rs), 0) + c * cap).astype(F32)
                    p_mat = (r_row == slot_r).astype(BF16)
                    xc = _dot(p_mat, hb[sub]).astype(BF16)
                    wc = (_dot(p_mat, w_hi) + _dot(p_mat, w_lo))[:, :1]
                    gate = _dot(xc, wg_ref[...])
                    up = _dot(xc, wu_ref[...])
                    act = gate * _sigmoid(gate) * up * wc
                    yc = _dot(act.astype(BF16), wd_ref[...])
                    yc = yc.astype(BF16)
                    if cap_p > cap:
                        yc = jnp.concatenate([yc, jnp.zeros((cap_p - cap, D_MODEL), BF16)], axis=0)
                    slot_c = (lax.broadcasted_iota(jnp.int32, (rs, cap_p), 1) + c * cap).astype(F32)
                    pt_mat = (r_col == slot_c).astype(BF16)
                    acc_ref[sub, :] += _dot(pt_mat, yc)

    @pl.when(e == ne - 1)
    def _():
        o_ref[...] = x_ref[...] + g2_ref[...] * acc_ref[...].reshape(tb, tt, D_MODEL)


def _moe(h2, x1, g2, comb, wg, wu, wd, tb, tt, rs):
    nb, t, _ = x1.shape
    rows = tb * tt
    cap = -(-(rs * 5 // 16) // 16) * 16
    n_sub = (nb // tb) * (t // tt) * (rows // rs)
    cnt = (comb.reshape(n_sub, rs, LANES)[:, :, :N_EXPERTS] > 0.0).sum(axis=1).astype(jnp.int32)
    row = lambda w: pl.BlockSpec((tb, tt, w), lambda b, i, e, c: (b, i, 0))
    modspec = pl.BlockSpec((tb, 1, D_MODEL), lambda b, i, e, c: (b, 0, 0))
    wspec_in = pl.BlockSpec((None, D_MODEL, D_FF_E), lambda b, i, e, c: (e, 0, 0))
    wspec_out = pl.BlockSpec((None, D_FF_E, D_MODEL), lambda b, i, e, c: (e, 0, 0))
    grid_spec = pltpu.PrefetchScalarGridSpec(
        num_scalar_prefetch=1,
        grid=(nb // tb, t // tt, N_EXPERTS),
        in_specs=[row(D_MODEL), row(D_MODEL), modspec, row(LANES), wspec_in, wspec_in, wspec_out],
        out_specs=row(D_MODEL),
        scratch_shapes=[pltpu.VMEM((rows, D_MODEL), F32),
                        pltpu.VMEM((rows, LANES), F32),
                        pltpu.VMEM((LANES, rows), F32),
                        pltpu.VMEM((LANES, rows), F32),
                        pltpu.VMEM((rs, rs), BF16)])
    return pl.pallas_call(
        functools.partial(_moe_body, cap=cap, rs=rs),
        grid_spec=grid_spec,
        out_shape=jax.ShapeDtypeStruct((nb, t, D_MODEL), F32),
        compiler_params=_cparams(("arbitrary", "arbitrary", "arbitrary")),
        name="moe",
    )(cnt.reshape(-1), h2, x1, g2, comb, wg, wu, wd)


def _prepare_weights(P):
    W = {}
    W['w_mod'] = P['w_mod'].astype(BF16)
    w_in = P['w_in'].astype(BF16)
    W['wq'] = w_in[:, :, :ATT_WIDTH]
    W['wk'] = w_in[:, :, ATT_WIDTH:2 * ATT_WIDTH]
    W['wv'] = w_in[:, :, 2 * ATT_WIDTH:3 * ATT_WIDTH]
    W['wrw'] = w_in[:, :, 3 * ATT_WIDTH:]
    depth = P['w_in'].shape[0]
    zw = jnp.zeros((depth, LORA_A, RWKV_WIDTH), BF16)
    W['w2p'] = jnp.concatenate([P['w2'].astype(BF16), zw], axis=1)
    W['a2p'] = jnp.concatenate([zw, P['a2'].astype(BF16)], axis=1)
    W['g2'] = P['g2'].astype(BF16)
    w_out = P['w_out'].astype(BF16)
    W['wo_a'] = w_out[:, :ATT_WIDTH]
    W['wo_r'] = w_out[:, ATT_WIDTH:]
    W['w_ff_gate'] = P['w_ff_gate'].astype(BF16)
    W['w_ff_up'] = P['w_ff_up'].astype(BF16)
    W['w_ff_down'] = P['w_ff_down'].astype(BF16)
    W['w_moe_gate'] = P['w_moe_gate'].astype(BF16)
    W['w_moe_up'] = P['w_moe_up'].astype(BF16)
    W['w_moe_down'] = P['w_moe_down'].astype(BF16)
    wr = jnp.pad(P['w_router'], ((0, 0), (0, 0), (0, LANES - N_EXPERTS)))
    W['router_hi'] = wr.astype(BF16)
    W['router_lo'] = (wr - W['router_hi'].astype(F32)).astype(BF16)
    seg = jnp.arange(RWKV_WIDTH) // HEAD_R
    W['eseg'] = (seg[:, None] == seg[None, :]).astype(BF16)
    return W


def _run_group(x, c, cache_k, cache_v, wkv_init, shift_init, P, W, tb, tt, scan_chunk, scan_nb, scan_nsub, bq):
    nb, t, _ = x.shape
    depth = P['w_in'].shape[0]
    eseg = W['eseg']
    prompt = cache_k is None
    ks, vs, shifts = [], [], []
    kv_stacked = ()
    wkv_stacked = None
    for l in range(depth):
        mod = _mod(c, W['w_mod'][l], P['b_mod'][l])
        sh1, sc1, g1, sh2, sc2, g2 = [mod[:, None, i * D_MODEL:(i + 1) * D_MODEL] for i in range(6)]
        qg = jnp.tile(P['q_gain'][l].reshape(1, 2 * HD_QK), (1, N_HEADS_A))
        kg = jnp.tile(P['k_gain'][l].reshape(1, 2 * HD_QK), (1, N_HEADS_A))
        proj = _in_proj(x, sc1, sh1, P['g_norm1'][l].reshape(1, D_MODEL), W['wq'][l], W['wk'][l],
                        W['wv'][l], W['wrw'][l], qg, kg, eseg, tb, tt, l, depth,
                        kv_stacked if prompt else None)
        q, rw = proj[:2]
        lam_init = 0.8 - 0.6 * math.exp(-0.3 * l)
        lam_vecs = jnp.stack([P['lambda_q1'][l], P['lambda_k1'][l], P['lambda_q2'][l], P['lambda_k2'][l]])
        subln = P['subln'][l].reshape(1, HD_V)
        if prompt:
            kv_stacked = (proj[4], proj[5])
            o = _attn_prompt(q, proj[2], proj[3], lam_vecs, subln, lam_init, bq)
        else:
            k, v = proj[2], proj[3]
            ks.append(k.reshape(nb, t, N_HEADS_A, 2, HD_QK))
            vs.append(v.reshape(nb, t, N_HEADS_A, HD_V))
            o = _attn_sample(q, cache_k, cache_v, l, k, v, lam_vecs, subln, lam_init)
        vec = lambda a: a.reshape(1, RWKV_WIDTH)
        prep = _rwkv_prep(rw, shift_init[l], P['mu'][l].reshape(1, RW_COLS), vec(P['w0'][l]), W['w2p'][l],
                          vec(P['a0'][l]), W['a2p'][l], W['g2'][l], vec(P['k_k'][l]), vec(P['k_a'][l]),
                          vec(P['r_k'][l]), eseg, tb, tt)
        y_r, wkv_stacked = _rwkv_scan(prep, wkv_init, l, wkv_stacked, vec(P['lnx_w'][l]), vec(P['lnx_b'][l]),
                                      eseg, scan_chunk, scan_nb, scan_nsub)
        routed = (l % 2 == 1)
        j = l // 2
        router = (W['router_hi'][j], W['router_lo'][j]) if routed else None
        res = _out_proj(o, y_r, x, g1, sc2, sh2, P['g_norm2'][l].reshape(1, D_MODEL), W['wo_a'][l], W['wo_r'][l],
                        router, tb, tt)
        if routed:
            x1, h2, comb = res
            x = _moe(h2, x1, g2, comb, W['w_moe_gate'][j], W['w_moe_up'][j], W['w_moe_down'][j], tb,
                     tt if tb > 1 else 2 * tt, tb * tt)
        else:
            x1, h2 = res
            x = _glu(h2, x1, g2, W['w_ff_gate'][j], W['w_ff_up'][j], W['w_ff_down'][j], tb, tt)
        shifts.append(rw[:, t - 1:, :])
    if prompt:
        kt, vr = kv_stacked
        k_out = jnp.transpose(kt.reshape(depth, nb, N_HEADS_A, 2, HD_QK, t), (0, 1, 5, 2, 3, 4))
        v_out = vr.reshape(depth, nb, t, N_HEADS_A, HD_V)
    else:
        k_out, v_out = jnp.stack(ks), jnp.stack(vs)
    return x, k_out, v_out, wkv_stacked, jnp.stack(shifts)


def kernel(x_prompt, x_sample, cache_k, cache_v, state_wkv, state_shift, c_prompt, c_sample, w_mod, b_mod, g_norm1, g_norm2, w_in, q_gain, k_gain, lambda_q1, lambda_k1, lambda_q2, lambda_k2, subln, mu, w0, w2, a0, a2, g2, k_k, k_a, r_k, lnx_w, lnx_b, w_out, w_ff_gate, w_ff_up, w_ff_down, w_router, w_moe_gate, w_moe_up, w_moe_down):
    P = dict(w_mod=w_mod, b_mod=b_mod, g_norm1=g_norm1, g_norm2=g_norm2, w_in=w_in, q_gain=q_gain,
             k_gain=k_gain, lambda_q1=lambda_q1, lambda_k1=lambda_k1, lambda_q2=lambda_q2,
             lambda_k2=lambda_k2, subln=subln, mu=mu, w0=w0, w2=w2, a0=a0, a2=a2, g2=g2, k_k=k_k,
             k_a=k_a, r_k=r_k, lnx_w=lnx_w, lnx_b=lnx_b, w_out=w_out, w_ff_gate=w_ff_gate,
             w_ff_up=w_ff_up, w_ff_down=w_ff_down, w_router=w_router, w_moe_gate=w_moe_gate,
             w_moe_up=w_moe_up, w_moe_down=w_moe_down)
    W = _prepare_weights(P)
    depth = w_in.shape[0]
    bp, tp, _ = x_prompt.shape
    bs, ts, _ = x_sample.shape
    n_past = cache_k.shape[2]
    wkv0 = jnp.zeros((depth, bp, N_HEADS_R, HEAD_R, HEAD_R), F32)
    shift0 = jnp.zeros((depth, bp, 1, RW_COLS), F32)
    y_p, k_p, v_p, wkv_p, shift_p = _run_group(x_prompt, c_prompt, None, None, wkv0, shift0, P, W,
                                               tb=1, tt=512, scan_chunk=CHUNK, scan_nb=bp, scan_nsub=4, bq=512)
    ck = jnp.transpose(cache_k, (0, 1, 3, 4, 5, 2)).reshape(depth, bs, ATT_WIDTH, n_past)
    cv = cache_v.reshape(depth, bs, n_past * N_HEADS_A, HD_V)
    y_s, k_s, v_s, wkv_s, shift_s = _run_group(x_sample, c_sample, ck, cv, state_wkv, state_shift, P, W,
                                               tb=bs, tt=ts, scan_chunk=ts, scan_nb=4, scan_nsub=1, bq=None)
    return (y_p, y_s, k_p, v_p, wkv_p, shift_p, k_s, v_s, wkv_s, shift_s)
```
